```python
import math
import jax, jax.numpy as jnp
from jax import lax
import numpy as np

D_MODEL = 1024
BATCH = 8
SEQ = 4096
DEPTH = 4

DH = 64
QBLK = 128
NSA_HEADS = 8
NSA_GROUPS = 2
NSA_R = NSA_HEADS // NSA_GROUPS
L_CMP = 32
D_CMP = 16
CMP_HIDDEN = 128
L_SLC = 64
N_SEL = 8
WINDOW = 512
MLA_HEADS = 4
MLA_NOPE = 64
MLA_ROPE = 32
MLA_V = 64
MLA_Q_LORA = 384
MLA_KV_LORA = 128
ROPE_THETA = 10000.0
FOX_HEADS = 4
XA_HEADS = 4
MEM_LEN = 256
D_FF = 4 * D_MODEL
T5_BUCKETS = 32
T5_MAX_DIST = 128
NSA_W = NSA_HEADS * DH
NSA_KV = NSA_GROUPS * DH
MLA_W = MLA_HEADS * MLA_V
FOX_W = FOX_HEADS * DH
XA_W = XA_HEADS * DH
IN_SIZES = (NSA_W, NSA_KV, NSA_KV, NSA_KV, NSA_KV, NSA_KV, NSA_KV, 3 * NSA_HEADS,
            MLA_Q_LORA, MLA_KV_LORA, MLA_ROPE,
            FOX_W, FOX_W, FOX_W, FOX_HEADS)
N_IN = sum(IN_SIZES)
DN_ALPHA = (2 * DEPTH) ** 0.25
DN_BETA = (8 * DEPTH) ** -0.25
LN_EPS = 1e-5
RMS_EPS = 1e-6
NEG = -1e30
FORCE = 1e4

kernel_name = 'hybrid_nsa_mla_fox_block'


def layer_norm(x, g, b):
    x32 = x.astype(jnp.float32)
    mu = jnp.mean(x32, axis=-1, keepdims=True)
    var = jnp.mean(jnp.square(x32 - mu), axis=-1, keepdims=True)
    return ((x32 - mu) * lax.rsqrt(var + LN_EPS) * g + b).astype(x.dtype)


def rms_norm(x, g):
    x32 = x.astype(jnp.float32)
    return (x32 * lax.rsqrt(jnp.mean(jnp.square(x32), axis=-1, keepdims=True) + RMS_EPS) * g).astype(x.dtype)


def rope(x, pos):
    half = x.shape[-1] // 2
    inv = ROPE_THETA ** (-jnp.arange(half, dtype=jnp.float32) / half)
    ang = pos.astype(jnp.float32)[:, None] * inv[None, :]
    cos = jnp.cos(ang)[None, :, None, :].astype(x.dtype)
    sin = jnp.sin(ang)[None, :, None, :].astype(x.dtype)
    x1, x2 = x[..., :half], x[..., half:]
    return jnp.concatenate([x1 * cos - x2 * sin, x1 * sin + x2 * cos], axis=-1)


def t5_bucket(dist):
    n = jnp.maximum(dist, 0)
    max_exact = T5_BUCKETS // 2
    nf = jnp.maximum(n, 1).astype(jnp.float32)
    large = max_exact + (jnp.log(nf / max_exact) / math.log(T5_MAX_DIST / max_exact)
                         * (T5_BUCKETS - max_exact)).astype(jnp.int32)
    large = jnp.minimum(large, T5_BUCKETS - 1)
    return jnp.where(n < max_exact, n, large)


def causal_block_attention(q, k, v, cum_log_f=None):
    B, S, H, dk = q.shape
    dv = v.shape[-1]
    nqb = S // QBLK
    scale = dk ** -0.5
    qb = q.reshape(B, nqb, QBLK, H, dk).transpose(1, 0, 2, 3, 4)
    key_pos = jnp.arange(S)
    decay_k = None if cum_log_f is None else cum_log_f.transpose(0, 2, 1)

    def one_block(args):
        i, q_i = args
        q_pos = i * QBLK + jnp.arange(QBLK)
        s = jnp.einsum('bqhd,bkhd->bhqk', q_i, k).astype(jnp.float32) * scale
        if decay_k is not None:
            decay_q = lax.dynamic_slice_in_dim(decay_k, i * QBLK, QBLK, axis=2)
            s = s + decay_q[..., None] - decay_k[:, :, None, :]
        s = jnp.where(key_pos[None, :] <= q_pos[:, None], s, NEG)
        p = jax.nn.softmax(s, axis=-1).astype(v.dtype)
        return jnp.einsum('bhqk,bkhd->bqhd', p, v)

    out = lax.map(one_block, (jnp.arange(nqb), qb))
    return out.transpose(1, 0, 2, 3, 4).reshape(B, S, H * dv)


def nsa_attention(q, k_cmp, v_cmp, k_slc, v_slc, k_win, v_win, gate_logits,
                  cmp_pe, cmp_w1, cmp_w2, t5_table):
    B, S, _ = q.shape
    G, R = NSA_GROUPS, NSA_R
    nqb = S // QBLK
    n_cmp = (S - L_CMP) // D_CMP + 1
    n_slc = S // L_SLC
    n_sel = min(N_SEL, n_slc)
    scale = DH ** -0.5
    k_cmp, v_cmp, k_slc, v_slc, k_win, v_win = [a.reshape(B, S, G, DH) for a in
                                                  (k_cmp, v_cmp, k_slc, v_slc, k_win, v_win)]

    tok_idx = jnp.arange(n_cmp)[:, None] * D_CMP + jnp.arange(L_CMP)[None, :]

    def compress(kv, pe, w1, w2):
        blocks = kv[:, tok_idx] + pe[None, None, :, None, :]
        flat = blocks.transpose(0, 1, 3, 2, 4).reshape(B, n_cmp, G, L_CMP * DH)
        return jax.nn.gelu(flat @ w1) @ w2

    kc = compress(k_cmp, cmp_pe[0], cmp_w1[0], cmp_w2[0])
    vc = compress(v_cmp, cmp_pe[1], cmp_w1[1], cmp_w2[1])
    cmp_end = jnp.arange(n_cmp) * D_CMP + (L_CMP - 1)

    c_lo = jnp.arange(n_cmp)[:, None] * D_CMP
    s_lo = jnp.arange(n_slc)[None, :] * L_SLC
    overlap = (jnp.maximum(jnp.minimum(c_lo + L_CMP, s_lo + L_SLC) - jnp.maximum(c_lo, s_lo), 0)
               .astype(jnp.float32) / D_CMP)

    ks_blk = k_slc.reshape(B, n_slc, L_SLC, G, DH).transpose(0, 3, 1, 2, 4)
    vs_blk = v_slc.reshape(B, n_slc, L_SLC, G, DH).transpose(0, 3, 1, 2, 4)
    kw_pad = jnp.pad(k_win, ((0, 0), (WINDOW, 0), (0, 0), (0, 0)))
    vw_pad = jnp.pad(v_win, ((0, 0), (WINDOW, 0), (0, 0), (0, 0)))

    qb = q.reshape(B, nqb, QBLK, G, R, DH).transpose(1, 0, 3, 4, 2, 5)
    gb = jax.nn.sigmoid(gate_logits).reshape(B, nqb, QBLK, G, R, 3).transpose(1, 0, 3, 4, 2, 5)
    bias_tab = t5_table.reshape(T5_BUCKETS, G, R).transpose(1, 2, 0)
    b_idx = jnp.arange(B)[:, None, None, None]
    g_idx = jnp.arange(G)[None, :, None, None]
    g5 = jnp.arange(G)[None, :, None, None, None]
    r5 = jnp.arange(R)[None, None, :, None, None]
    blk_ids = jnp.arange(n_slc)

    def one_block(args):
        i, q_i, g_i = args
        t = i * QBLK + jnp.arange(QBLK)
        dist_c = t[:, None] - cmp_end[None, :]
        valid_c = dist_c >= 0
        s_c = (jnp.einsum('bgrqd,bcgd->bgrqc', q_i, kc).astype(jnp.float32) * scale
               + bias_tab[:, :, t5_bucket(dist_c)])
        p_c = jax.nn.softmax(jnp.where(valid_c, s_c, NEG), axis=-1) * valid_c
        o_c = jnp.einsum('bgrqc,bcgd->bgrqd', p_c.astype(vc.dtype), vc)
        imp = jnp.einsum('bgrqc,cn->bgqn', p_c, overlap)
        cur = t // L_SLC
        forced = ((blk_ids[None, :] == 0) | (blk_ids[None, :] == cur[:, None])
                  | (blk_ids[None, :] == cur[:, None] - 1))
        causal_blk = blk_ids[None, :] * L_SLC <= t[:, None]
        imp = jnp.where(causal_blk, imp + FORCE * forced, NEG)
        _, sel = lax.top_k(imp, n_sel)
        k_sel = ks_blk[b_idx, g_idx, sel].reshape(B, G, QBLK, n_sel * L_SLC, DH)
        v_sel = vs_blk[b_idx, g_idx, sel].reshape(B, G, QBLK, n_sel * L_SLC, DH)
        pos_sel = (sel[..., None] * L_SLC + jnp.arange(L_SLC)).reshape(B, G, QBLK, n_sel * L_SLC)
        dist_s = t[:, None] - pos_sel
        s_s = (jnp.einsum('bgrqd,bgqkd->bgrqk', q_i, k_sel).astype(jnp.float32) * scale
               + bias_tab[g5, r5, t5_bucket(dist_s)[:, :, None]])
        p_s = jax.nn.softmax(jnp.where((dist_s >= 0)[:, :, None], s_s, NEG), axis=-1)
        o_s = jnp.einsum('bgrqk,bgqkd->bgrqd', p_s.astype(v_sel.dtype), v_sel)
        pos_w = i * QBLK - WINDOW + jnp.arange(WINDOW + QBLK)
        dist_w = t[:, None] - pos_w[None, :]
        valid_w = (dist_w >= 0) & (dist_w < WINDOW) & (pos_w[None, :] >= 0)
        k_w = lax.dynamic_slice_in_dim(kw_pad, i * QBLK, WINDOW + QBLK, axis=1)
        v_w = lax.dynamic_slice_in_dim(vw_pad, i * QBLK, WINDOW + QBLK, axis=1)
        s_w = (jnp.einsum('bgrqd,bkgd->bgrqk', q_i, k_w).astype(jnp.float32) * scale
               + bias_tab[:, :, t5_bucket(dist_w)])
        p_w = jax.nn.softmax(jnp.where(valid_w, s_w, NEG), axis=-1)
        o_w = jnp.einsum('bgrqk,bkgd->bgrqd', p_w.astype(v_w.dtype), v_w)
        return g_i[..., 0:1] * o_c + g_i[..., 1:2] * o_s + g_i[..., 2:3] * o_w

    out = lax.map(one_block, (jnp.arange(nqb), qb, gb))
    return out.transpose(1, 0, 4, 2, 3, 5).reshape(B, S, NSA_W)


def mla_attention(c_q, c_kv, k_rope, q_norm, w_uq, kv_norm, w_ukv):
    B, S, _ = c_q.shape
    pos = jnp.arange(S)
    q = (rms_norm(c_q, q_norm) @ w_uq).reshape(B, S, MLA_HEADS, MLA_NOPE + MLA_ROPE)
    q = jnp.concatenate([q[..., :MLA_NOPE], rope(q[..., MLA_NOPE:], pos)], axis=-1)
    kv = (rms_norm(c_kv, kv_norm) @ w_ukv).reshape(B, S, MLA_HEADS, MLA_NOPE + MLA_V)
    k_r = jnp.broadcast_to(rope(k_rope[:, :, None, :], pos), (B, S, MLA_HEADS, MLA_ROPE))
    k = jnp.concatenate([kv[..., :MLA_NOPE], k_r], axis=-1)
    return causal_block_attention(q, k, kv[..., MLA_NOPE:])


def fox_attention(q, k, v, f_logit, b_f):
    B, S, _ = q.shape
    shp = (B, S, FOX_HEADS, DH)
    cum = jnp.cumsum(jax.nn.log_sigmoid((f_logit + b_f).astype(jnp.float32)), axis=1)
    return causal_block_attention(q.reshape(shp), k.reshape(shp), v.reshape(shp), cum_log_f=cum)


def hybrid_mixer(h, w_in, cmp_pe, cmp_w1, cmp_w2, t5_table, q_norm, w_uq, kv_norm, w_ukv,
                 b_f, w_gate, w_br_nsa, w_br_mla, w_br_fox, w_out):
    B, S, _ = h.shape
    split_points = np.cumsum(IN_SIZES)[:-1].tolist()
    (nq, nkc, nvc, nks, nvs, nkw, nvw, ngate,
     cq, ckv, kr, fq, fk, fv, ff) = jnp.split(h @ w_in, split_points, axis=-1)
    o_nsa = nsa_attention(nq, nkc, nvc, nks, nvs, nkw, nvw, ngate, cmp_pe, cmp_w1, cmp_w2, t5_table)
    o_mla = mla_attention(cq, ckv, kr, q_norm, w_uq, kv_norm, w_ukv)
    o_fox = fox_attention(fq, fk, fv, ff, b_f)
    gates = jax.nn.sigmoid(h @ w_gate).reshape(B, S, 3, D_MODEL)
    merged = (gates[:, :, 0] * (o_nsa @ w_br_nsa) + gates[:, :, 1] * (o_mla @ w_br_mla)
              + gates[:, :, 2] * (o_fox @ w_br_fox))
    return merged @ w_out


def memory_cross_attention(x, mem, w_q, w_kv, w_o):
    B, S, _ = x.shape
    M = mem.shape[1]
    q = (x @ w_q).reshape(B, S, XA_HEADS, DH)
    kv = (mem @ w_kv).reshape(B, M, 2, XA_HEADS, DH)
    k, v = kv[:, :, 0], kv[:, :, 1]
    s = jnp.einsum('bqhd,bmhd->bhqm', q, k).astype(jnp.float32) * (DH ** -0.5)
    p = jax.nn.softmax(s, axis=-1).astype(v.dtype)
    o = jnp.einsum('bhqm,bmhd->bqhd', p, v).reshape(B, S, XA_W)
    return o @ w_o


def setup_inputs(seed: int = 0) -> dict:
    key = jax.random.key(seed)
    ks = jax.random.split(key, 32)
    L = DEPTH

    def dense(k, shape, fan_in, scale=1.0):
        return jax.random.normal(k, shape, jnp.float32) * (scale * fan_in ** -0.5)

    def normal(k, shape, scale=1.0):
        return jax.random.normal(k, shape, jnp.float32) * scale

    return {
        'x': normal(ks[0], (BATCH, SEQ, D_MODEL)),
        'mem': normal(ks[1], (BATCH, MEM_LEN, D_MODEL)),
        'w_in': dense(ks[2], (L, D_MODEL, N_IN), D_MODEL),
        'cmp_pe': normal(ks[3], (L, 2, L_CMP, DH), 0.1),
        'cmp_w1': dense(ks[4], (L, 2, L_CMP * DH, CMP_HIDDEN), L_CMP * DH),
        'cmp_w2': dense(ks[5], (L, 2, CMP_HIDDEN, DH), CMP_HIDDEN),
        't5_table': normal(ks[6], (T5_BUCKETS, NSA_HEADS), 0.5),
        'mla_q_norm': 1.0 + normal(ks[7], (L, MLA_Q_LORA), 0.02),
        'mla_w_uq': dense(ks[8], (L, MLA_Q_LORA, MLA_HEADS * (MLA_NOPE + MLA_ROPE)), MLA_Q_LORA),
        'mla_kv_norm': 1.0 + normal(ks[9], (L, MLA_KV_LORA), 0.02),
        'mla_w_ukv': dense(ks[10], (L, MLA_KV_LORA, MLA_HEADS * (MLA_NOPE + MLA_V)), MLA_KV_LORA),
        'fox_b_f': jax.random.uniform(ks[11], (L, FOX_HEADS), jnp.float32, 1.0, 6.0),
        'w_gate': dense(ks[12], (L, D_MODEL, 3 * D_MODEL), D_MODEL),
        'w_br_nsa': dense(ks[13], (L, NSA_W, D_MODEL), NSA_W),
        'w_br_mla': dense(ks[14], (L, MLA_W, D_MODEL), MLA_W),
        'w_br_fox': dense(ks[15], (L, FOX_W, D_MODEL), FOX_W),
        'w_mix_out': dense(ks[16], (L, D_MODEL, D_MODEL), D_MODEL, DN_BETA),
        'xa_w_q': dense(ks[17], (L, D_MODEL, XA_W), D_MODEL),
        'xa_w_kv': dense(ks[18], (L, D_MODEL, 2 * XA_W), D_MODEL),
        'xa_w_o': dense(ks[19], (L, XA_W, D_MODEL), XA_W, DN_BETA),
        'mlp_w_up': dense(ks[20], (L, D_MODEL, D_FF), D_MODEL),
        'mlp_w_down': dense(ks[21], (L, D_FF, D_MODEL), D_FF, DN_BETA),
        'ln_g': 1.0 + normal(ks[22], (L, 3, D_MODEL), 0.02),
        'ln_b': normal(ks[23], (L, 3, D_MODEL), 0.02),
    }


def reference(x, mem, w_in, cmp_pe, cmp_w1, cmp_w2, t5_table, mla_q_norm, mla_w_uq, mla_kv_norm,
              mla_w_ukv, fox_b_f, w_gate, w_br_nsa, w_br_mla, w_br_fox, w_mix_out, xa_w_q, xa_w_kv,
              xa_w_o, mlp_w_up, mlp_w_down, ln_g, ln_b):
    for l in range(DEPTH):
        y = hybrid_mixer(x, w_in[l], cmp_pe[l], cmp_w1[l], cmp_w2[l], t5_table, mla_q_norm[l],
                         mla_w_uq[l], mla_kv_norm[l], mla_w_ukv[l], fox_b_f[l], w_gate[l],
                         w_br_nsa[l], w_br_mla[l], w_br_fox[l], w_mix_out[l])
        x = layer_norm(DN_ALPHA * x + y, ln_g[l, 0], ln_b[l, 0])
        y = memory_cross_attention(x, mem, xa_w_q[l], xa_w_kv[l], xa_w_o[l])
        x = layer_norm(DN_ALPHA * x + y, ln_g[l, 1], ln_b[l, 1])
        y = jnp.square(jax.nn.relu(x @ mlp_w_up[l])) @ mlp_w_down[l]
        x = layer_norm(DN_ALPHA * x + y, ln_g[l, 2], ln_b[l, 2])
    return x
```

```python
import functools
import math

import numpy as np
import jax
import jax.numpy as jnp
from jax import lax
from jax.experimental import pallas as pl
from jax.experimental.pallas import tpu as pltpu

D_MODEL = 1024
DH = 64
NSA_HEADS = 8
NSA_GROUPS = 2
NSA_R = NSA_HEADS // NSA_GROUPS
L_CMP = 32
D_CMP = 16
CMP_HIDDEN = 128
L_SLC = 64
N_SEL = 8
WINDOW = 512
MLA_HEADS = 4
MLA_NOPE = 64
MLA_ROPE = 32
MLA_V = 64
MLA_Q_LORA = 384
MLA_KV_LORA = 128
ROPE_THETA = 10000.0
FOX_HEADS = 4
XA_HEADS = 4
D_FF = 4 * D_MODEL
T5_BUCKETS = 32
T5_MAX_DIST = 128
NSA_W = NSA_HEADS * DH
NSA_KV = NSA_GROUPS * DH
MLA_W = MLA_HEADS * MLA_V
FOX_W = FOX_HEADS * DH
XA_W = XA_HEADS * DH
LN_EPS = 1e-5
RMS_EPS = 1e-6
NEG = -1e30
FORCE = 1e4

F32 = jnp.float32
MXU_DT = jnp.bfloat16
NSA_TQ = 128
ATT_T = 256
VMEM_LIMIT = 48 * 1024 * 1024

_NA = NSA_W + 4 * NSA_KV + 3 * FOX_W
_OFF_CQ = 0
_OFF_CKV = MLA_Q_LORA
_OFF_KR = _OFF_CKV + MLA_KV_LORA
_OFF_NG = _OFF_KR + MLA_ROPE
_OFF_FF = _OFF_NG + 3 * NSA_HEADS
_OFF_KC = 640
_OFF_VC = _OFF_KC + NSA_KV
_NB = _OFF_VC + NSA_KV


def _dot(a, b):
    return jnp.dot(a, b, preferred_element_type=F32)


def _dot_nt(a, b):
    return lax.dot_general(a, b, (((1,), (1,)), ((), ())), preferred_element_type=F32)


def _params(*sem):
    return pltpu.CompilerParams(dimension_semantics=sem, vmem_limit_bytes=VMEM_LIMIT)


def _layer_norm(z, g, b):
    mu = jnp.mean(z, axis=-1, keepdims=True)
    zc = z - mu
    var = jnp.mean(zc * zc, axis=-1, keepdims=True)
    return zc * lax.rsqrt(var + LN_EPS) * g + b


def _rms_norm(z, g):
    return z * lax.rsqrt(jnp.mean(z * z, axis=-1, keepdims=True) + RMS_EPS) * g


def _t5_bucket(dist):
    n = jnp.maximum(dist, 0)
    max_exact = T5_BUCKETS // 2
    nf = jnp.maximum(n, 1).astype(F32)
    large = max_exact + (jnp.log(nf / max_exact) / math.log(T5_MAX_DIST / max_exact)
                         * (T5_BUCKETS - max_exact)).astype(jnp.int32)
    large = jnp.minimum(large, T5_BUCKETS - 1)
    return jnp.where(n < max_exact, n, large)


def _bias_table_kernel(t5_ref, idx_ref, o_ref):
    h = pl.program_id(0)
    idx = idx_ref[...]
    acc = jnp.zeros(idx.shape, F32)
    for b in range(T5_BUCKETS):
        acc = jnp.where(idx == b, t5_ref[b, h], acc)
    o_ref[0] = acc


def bias_table(t5_table, idx, tr):
    rows, cols = idx.shape
    return pl.pallas_call(
        _bias_table_kernel,
        out_shape=jax.ShapeDtypeStruct((NSA_HEADS, rows, cols), F32),
        grid=(NSA_HEADS, rows // tr),
        in_specs=[pl.BlockSpec(memory_space=pltpu.SMEM),
                  pl.BlockSpec((tr, cols), lambda h, i: (i, 0))],
        out_specs=pl.BlockSpec((1, tr, cols), lambda h, i: (h, i, 0)),
        compiler_params=_params("parallel", "parallel"),
        name="bias_table",
    )(t5_table, idx)


def _in_proj_kernel(x_ref, wa_ref, wb_ref, ya_ref, yb_ref):
    xb = x_ref[...].astype(MXU_DT)
    ya_ref[...] = _dot(xb, wa_ref[...]).astype(ya_ref.dtype)
    yb_ref[...] = _dot(xb, wb_ref[...])


def in_proj(x2, wa, wb, tm=512):
    t, d = x2.shape
    return pl.pallas_call(
        _in_proj_kernel,
        out_shape=(jax.ShapeDtypeStruct((t, _NA), MXU_DT), jax.ShapeDtypeStruct((t, _NB), F32)),
        grid=(t // tm,),
        in_specs=[pl.BlockSpec((tm, d), lambda i: (i, 0)),
                  pl.BlockSpec((d, _NA), lambda i: (0, 0)),
                  pl.BlockSpec((d, _NB), lambda i: (0, 0))],
        out_specs=(pl.BlockSpec((tm, _NA), lambda i: (i, 0)),
                   pl.BlockSpec((tm, _NB), lambda i: (i, 0))),
        compiler_params=_params("parallel"),
        name="in_proj",
    )(x2, wa, wb)


def _mla_up_kernel(yb_ref, qn_ref, wuq_ref, kvn_ref, wukv_ref, cos_ref, sin_ref,
                   qnope_ref, q1_ref, q2_ref, kn_ref, v_ref, k1_ref, k2_ref):
    scale = (MLA_NOPE + MLA_ROPE) ** -0.5
    nn = MLA_HEADS * MLA_NOPE
    half = MLA_ROPE // 2
    nr = MLA_HEADS * half
    cos4 = cos_ref[...]
    sin4 = sin_ref[...]
    cq = yb_ref[:, _OFF_CQ:_OFF_CQ + MLA_Q_LORA]
    yq = _dot(_rms_norm(cq, qn_ref[...]).astype(MXU_DT), wuq_ref[...])
    x1 = yq[:, nn:nn + nr]
    x2 = yq[:, nn + nr:nn + 2 * nr]
    qnope_ref[...] = (yq[:, :nn] * scale).astype(qnope_ref.dtype)
    q1_ref[...] = ((x1 * cos4 - x2 * sin4) * scale).astype(q1_ref.dtype)
    q2_ref[...] = ((x1 * sin4 + x2 * cos4) * scale).astype(q2_ref.dtype)
    ckv = yb_ref[:, _OFF_CKV:_OFF_CKV + MLA_KV_LORA]
    ykv = _dot(_rms_norm(ckv, kvn_ref[...]).astype(MXU_DT), wukv_ref[...])
    kn_ref[...] = ykv[:, :nn].astype(kn_ref.dtype)
    v_ref[...] = ykv[:, nn:].astype(v_ref.dtype)
    c1 = cos4[:, :half]
    s1 = sin4[:, :half]
    r1 = yb_ref[:, _OFF_KR:_OFF_KR + half]
    r2 = yb_ref[:, _OFF_KR + half:_OFF_KR + 2 * half]
    k1_ref[...] = (r1 * c1 - r2 * s1).astype(k1_ref.dtype)
    k2_ref[...] = (r1 * s1 + r2 * c1).astype(k2_ref.dtype)


def mla_up(yb, q_norm, wuq, kv_norm, wukv, cos4, sin4, seq, tm=512):
    t = yb.shape[0]
    nn = MLA_HEADS * MLA_NOPE
    nr = MLA_HEADS * MLA_ROPE // 2
    half = MLA_ROPE // 2
    nps = seq // tm
    row = lambda i: (i, 0)
    fixed = lambda i: (0, 0)
    pos = lambda i: (i % nps, 0)
    shapes = [(nn, MXU_DT), (nr, MXU_DT), (nr, MXU_DT), (nn, MXU_DT),
              (MLA_HEADS * MLA_V, MXU_DT), (half, MXU_DT), (half, MXU_DT)]
    return pl.pallas_call(
        _mla_up_kernel,
        out_shape=tuple(jax.ShapeDtypeStruct((t, n), dt) for n, dt in shapes),
        grid=(t // tm,),
        in_specs=[pl.BlockSpec((tm, _NB), row),
                  pl.BlockSpec((1, MLA_Q_LORA), fixed),
                  pl.BlockSpec(wuq.shape, fixed),
                  pl.BlockSpec((1, MLA_KV_LORA), fixed),
                  pl.BlockSpec(wukv.shape, fixed),
                  pl.BlockSpec((tm, nr), pos),
                  pl.BlockSpec((tm, nr), pos)],
        out_specs=tuple(pl.BlockSpec((tm, n), row) for n, _ in shapes),
        compiler_params=_params("parallel"),
        name="mla_up",
    )(yb, q_norm, wuq, kv_norm, wukv, cos4, sin4)


def _fox_cum_kernel(f_ref, b_ref, o_ref):
    z = f_ref[...] + b_ref[...]
    x = jnp.minimum(z, 0.0) - jnp.log1p(jnp.exp(-jnp.abs(z)))
    n = x.shape[1]
    lane = lax.broadcasted_iota(jnp.int32, x.shape, 1)
    shift = 1
    while shift < n:
        x = x + jnp.where(lane >= shift, pltpu.roll(x, shift, 1), 0.0)
        shift *= 2
    o_ref[...] = x


def fox_cum(f_t, b_col):
    return pl.pallas_call(
        _fox_cum_kernel,
        out_shape=jax.ShapeDtypeStruct(f_t.shape, F32),
        compiler_params=pltpu.CompilerParams(vmem_limit_bytes=VMEM_LIMIT),
        name="fox_cum",
    )(f_t, b_col)


def _causal_attn_kernel(*refs, has_decay):
    if has_decay:
        q_ref, k_ref, v_ref, ck_ref, o_ref = refs
    else:
        q_ref, k_ref, v_ref, o_ref = refs
        ck_ref = None
    qi = pl.program_id(2)
    q = q_ref[0, 0]
    tq = q.shape[0]
    dv = v_ref.shape[-1]

    def scores(kt):
        s = _dot_nt(q, k_ref[0, 0, kt])
        if has_decay:
            s = s - ck_ref[0, 0, kt]
        return s

    def update(carry, s, kt):
        m, l, acc = carry
        m_new = jnp.maximum(m, jnp.max(s, axis=1, keepdims=True))
        a = jnp.exp(m - m_new)
        p = jnp.exp(s - m_new)
        l = a * l + jnp.sum(p, axis=1, keepdims=True)
        acc = a * acc + _dot(p.astype(MXU_DT), v_ref[0, 0, kt])
        return m_new, l, acc

    def body(kt, carry):
        return update(carry, scores(kt), kt)

    init = (jnp.full((tq, 1), NEG, F32), jnp.zeros((tq, 1), F32), jnp.zeros((tq, dv), F32))
    carry = lax.fori_loop(0, qi, body, init)
    rr = lax.broadcasted_iota(jnp.int32, (tq, tq), 0)
    cc = lax.broadcasted_iota(jnp.int32, (tq, tq), 1)
    s = jnp.where(rr >= cc, scores(qi), NEG)
    _, l, acc = update(carry, s, qi)
    o_ref[0, 0] = (acc / l).astype(o_ref.dtype)


def causal_attention(q, k, v, ck=None, t=ATT_T):
    b, h, s, dk = q.shape
    dv = v.shape[-1]
    nt = s // t
    k5 = k.reshape(b, h, nt, t, dk)
    v5 = v.reshape(b, h, nt, t, dv)
    args = [q, k5, v5]
    in_specs = [pl.BlockSpec((1, 1, t, dk), lambda bi, hi, qi: (bi, hi, qi, 0)),
                pl.BlockSpec((1, 1, nt, t, dk), lambda bi, hi, qi: (bi, hi, 0, 0, 0)),
                pl.BlockSpec((1, 1, nt, t, dv), lambda bi, hi, qi: (bi, hi, 0, 0, 0))]
    if ck is not None:
        args.append(ck.reshape(b, h, nt, 1, t))
        in_specs.append(pl.BlockSpec((1, 1, nt, 1, t), lambda bi, hi, qi: (bi, hi, 0, 0, 0)))
    return pl.pallas_call(
        functools.partial(_causal_attn_kernel, has_decay=ck is not None),
        out_shape=jax.ShapeDtypeStruct((b, h, s, dv), MXU_DT),
        grid=(b, h, nt),
        in_specs=in_specs,
        out_specs=pl.BlockSpec((1, 1, t, dv), lambda bi, hi, qi: (bi, hi, qi, 0)),
        compiler_params=_params("parallel", "parallel", "arbitrary"),
        name="causal_attn_decay" if ck is not None else "causal_attn",
    )(*args)


def _compress_kernel(x_ref, pe_ref, w1_ref, w2_ref, o_ref):
    x = x_ref[0, 0, 0]
    nc, half = x.shape
    pe = pe_ref[0]
    lo = _dot((x + pe[:, :half]).astype(MXU_DT), w1_ref[0, :half, :])
    hi = _dot((x + pe[:, half:]).astype(MXU_DT), w1_ref[0, half:, :])
    hid = lo + pltpu.roll(hi, nc - 1, 0)
    act = jax.nn.gelu(hid)
    o_ref[0, 0, 0] = _dot(act.astype(MXU_DT), w2_ref[0]).astype(o_ref.dtype)


def compress(xkv, pe, w1, w2):
    _, b, g, nc, width = xkv.shape
    return pl.pallas_call(
        _compress_kernel,
        out_shape=jax.ShapeDtypeStruct((2, b, g, nc, DH), MXU_DT),
        grid=(2, b, g),
        in_specs=[pl.BlockSpec((1, 1, 1, nc, width), lambda a, bi, gi: (a, bi, gi, 0, 0)),
                  pl.BlockSpec((1, 1, 2 * width), lambda a, bi, gi: (a, 0, 0)),
                  pl.BlockSpec((1, 2 * width, CMP_HIDDEN), lambda a, bi, gi: (a, 0, 0)),
                  pl.BlockSpec((1, CMP_HIDDEN, DH), lambda a, bi, gi: (a, 0, 0))],
        out_specs=pl.BlockSpec((1, 1, 1, nc, DH), lambda a, bi, gi: (a, bi, gi, 0, 0)),
        compiler_params=_params("parallel", "parallel", "parallel"),
        name="nsa_compress",
    )(xkv, pe, w1, w2)


def _nsa_kernel(q_ref, kc_ref, vc_ref, ks_ref, vs_ref, kw_ref, vw_ref, gate_ref,
                bc_ref, bw_ref, ov_ref, o_ref, *, n_sel):
    i = pl.program_id(1)
    tq = NSA_TQ
    ncp = kc_ref.shape[2]
    nslc = ov_ref.shape[1]
    nwt = WINDOW // tq + 1
    t0 = i * tq
    row = lax.broadcasted_iota(jnp.int32, (tq, 1), 0) + t0

    ccol = lax.broadcasted_iota(jnp.int32, (tq, ncp), 1)
    valid_c = (row - (ccol * D_CMP + (L_CMP - 1))) >= 0
    gates = jax.nn.sigmoid(gate_ref[0])
    ov = ov_ref[...]

    blk = lax.broadcasted_iota(jnp.int32, (tq, nslc), 1)
    blk_f = blk.astype(F32)
    cur = lax.shift_right_logical(row, int(math.log2(L_SLC)))
    forced = (blk == 0) | (blk == cur) | (blk == cur - 1)
    causal_blk = blk * L_SLC <= row

    lrow = lax.broadcasted_iota(jnp.int32, (tq, tq), 0)
    lcol = lax.broadcasted_iota(jnp.int32, (tq, tq), 1)
    erow = lax.broadcasted_iota(jnp.int32, (nslc, tq), 0)
    ecol = lax.shift_right_logical(lax.broadcasted_iota(jnp.int32, (nslc, tq), 1),
                                   int(math.log2(L_SLC)))
    blocks_per_tile = tq // L_SLC

    wrow = lax.broadcasted_iota(jnp.int32, (tq, nwt * tq), 0)
    wcol = lax.broadcasted_iota(jnp.int32, (tq, nwt * tq), 1)
    dist_w = wrow + WINDOW - wcol
    valid_w = (dist_w >= 0) & (dist_w < WINDOW) & (t0 - WINDOW + wcol >= 0)

    for g in range(NSA_GROUPS):
        heads = [g * NSA_R + r for r in range(NSA_R)]
        kc = kc_ref[0, g]
        vc = vc_ref[0, g]

        imp = jnp.zeros((tq, nslc), F32)
        o_cmp = []
        for h in heads:
            s = _dot_nt(q_ref[0, h], kc) + bc_ref[h]
            s = jnp.where(valid_c, s, NEG)
            p = jnp.exp(s - jnp.max(s, axis=1, keepdims=True))
            p = jnp.where(valid_c, p / jnp.sum(p, axis=1, keepdims=True), 0.0)
            pb = p.astype(MXU_DT)
            o_cmp.append(_dot(pb, vc))
            imp = imp + _dot(pb, ov)

        x = jnp.where(causal_blk, imp + jnp.where(forced, FORCE, 0.0), NEG)
        sel = jnp.zeros((tq, nslc), F32)
        for _ in range(n_sel):
            mx = jnp.max(x, axis=1, keepdims=True)
            first = jnp.min(jnp.where(x == mx, blk_f, float(nslc)), axis=1, keepdims=True)
            hit = blk_f == first
            sel = jnp.where(hit, 1.0, sel)
            x = jnp.where(hit, -3e38, x)
        selb = sel.astype(MXU_DT)

        def sel_body(kt, carry, g=g, heads=heads, selb=selb):
            expand = (erow == blocks_per_tile * kt + ecol).astype(MXU_DT)
            member = _dot(selb, expand)
            delta = i - kt
            keep = (member > 0.5) & (lrow + delta * tq >= lcol)
            bidx = jnp.minimum(delta, 2)
            k = ks_ref[0, g, kt]
            v = vs_ref[0, g, kt]
            out = []
            for (m, l, acc), h in zip(carry, heads):
                s = _dot_nt(q_ref[0, h], k) + bw_ref[h, bidx]
                s = jnp.where(keep, s, NEG)
                m_new = jnp.maximum(m, jnp.max(s, axis=1, keepdims=True))
                a = jnp.exp(m - m_new)
                p = jnp.exp(s - m_new)
                l = a * l + jnp.sum(p, axis=1, keepdims=True)
                acc = a * acc + _dot(p.astype(MXU_DT), v)
                out.append((m_new, l, acc))
            return tuple(out)

        init = tuple((jnp.full((tq, 1), NEG, F32), jnp.zeros((tq, 1), F32),
                      jnp.zeros((tq, DH), F32)) for _ in heads)
        res = lax.fori_loop(0, i + 1, sel_body, init)
        o_sel = [acc / l for (_, l, acc) in res]

        tiles = [jnp.maximum(i - d, 0) for d in range(nwt - 1, -1, -1)]
        kcat = jnp.concatenate([kw_ref[0, g, kt] for kt in tiles], axis=0)
        vcat = jnp.concatenate([vw_ref[0, g, kt] for kt in tiles], axis=0)
        for n, h in enumerate(heads):
            bias = jnp.concatenate([bw_ref[h, min(d, 2)] for d in range(nwt - 1, -1, -1)], axis=1)
            s = _dot_nt(q_ref[0, h], kcat) + bias
            s = jnp.where(valid_w, s, NEG)
            p = jnp.exp(s - jnp.max(s, axis=1, keepdims=True))
            l = jnp.sum(p, axis=1, keepdims=True)
            o_win = _dot(p.astype(MXU_DT), vcat) / l
            out = (gates[:, 3 * h:3 * h + 1] * o_cmp[n]
                   + gates[:, 3 * h + 1:3 * h + 2] * o_sel[n]
                   + gates[:, 3 * h + 2:3 * h + 3] * o_win)
            o_ref[0, :, h * DH:(h + 1) * DH] = out.astype(o_ref.dtype)


def nsa_attention(q, kc, vc, ks, vs, kw, vw, gate_logits, bias_c, bias_w, overlap):
    b, hh, s, _ = q.shape
    tq = NSA_TQ
    nt = s // tq
    ncp = kc.shape[2]
    nslc = overlap.shape[1]
    tile5 = lambda a: a.reshape(b, NSA_GROUPS, nt, tq, DH)
    per_b5 = pl.BlockSpec((1, NSA_GROUPS, nt, tq, DH), lambda bi, i: (bi, 0, 0, 0, 0))
    per_b4 = pl.BlockSpec((1, NSA_GROUPS, ncp, DH), lambda bi, i: (bi, 0, 0, 0))
    return pl.pallas_call(
        functools.partial(_nsa_kernel, n_sel=min(N_SEL, nslc)),
        out_shape=jax.ShapeDtypeStruct((b, s, NSA_W), MXU_DT),
        grid=(b, nt),
        in_specs=[pl.BlockSpec((1, hh, tq, DH), lambda bi, i: (bi, 0, i, 0)),
                  per_b4, per_b4, per_b5, per_b5, per_b5, per_b5,
                  pl.BlockSpec((1, tq, 3 * NSA_HEADS), lambda bi, i: (bi, i, 0)),
                  pl.BlockSpec((hh, tq, ncp), lambda bi, i: (0, i, 0)),
                  pl.BlockSpec(bias_w.shape, lambda bi, i: (0, 0, 0, 0)),
                  pl.BlockSpec(overlap.shape, lambda bi, i: (0, 0))],
        out_specs=pl.BlockSpec((1, tq, NSA_W), lambda bi, i: (bi, i, 0)),
        compiler_params=_params("parallel", "arbitrary"),
        name="nsa_attn",
    )(q, kc, vc, tile5(ks), tile5(vs), tile5(kw), tile5(vw), gate_logits, bias_c, bias_w, overlap)


def _mix_kernel(x_ref, on_ref, om_ref, of_ref, wg_ref, wn_ref, wm_ref, wf_ref, wo_ref,
                g_ref, b_ref, o_ref, *, alpha):
    x = x_ref[...]
    xb = x.astype(MXU_DT)
    d = x.shape[1]
    merged = (jax.nn.sigmoid(_dot(xb, wg_ref[:, 0:d])) * _dot(on_ref[...], wn_ref[...])
              + jax.nn.sigmoid(_dot(xb, wg_ref[:, d:2 * d])) * _dot(om_ref[...], wm_ref[...])
              + jax.nn.sigmoid(_dot(xb, wg_ref[:, 2 * d:3 * d])) * _dot(of_ref[...], wf_ref[...]))
    y = _dot(merged.astype(MXU_DT), wo_ref[...])
    o_ref[...] = _layer_norm(alpha * x + y, g_ref[...], b_ref[...])


def mixer_out(x2, o_nsa, o_mla, o_fox, wg, wn, wm, wf, wo, ln_g, ln_b, alpha, tm=256):
    t, d = x2.shape
    row = lambda i: (i, 0)
    fixed = lambda i: (0, 0)
    return pl.pallas_call(
        functools.partial(_mix_kernel, alpha=alpha),
        out_shape=jax.ShapeDtypeStruct((t, d), F32),
        grid=(t // tm,),
        in_specs=[pl.BlockSpec((tm, d), row),
                  pl.BlockSpec((tm, NSA_W), row),
                  pl.BlockSpec((tm, MLA_W), row),
                  pl.BlockSpec((tm, FOX_W), row),
                  pl.BlockSpec(wg.shape, fixed), pl.BlockSpec(wn.shape, fixed),
                  pl.BlockSpec(wm.shape, fixed), pl.BlockSpec(wf.shape, fixed),
                  pl.BlockSpec(wo.shape, fixed),
                  pl.BlockSpec((1, d), fixed), pl.BlockSpec((1, d), fixed)],
        out_specs=pl.BlockSpec((tm, d), row),
        compiler_params=_params("parallel"),
        name="mixer_out",
    )(x2, o_nsa, o_mla, o_fox, wg, wn, wm, wf, wo, ln_g, ln_b)


def _matmul_kernel(x_ref, w_ref, o_ref):
    o_ref[...] = _dot(x_ref[...].astype(MXU_DT), w_ref[...]).astype(o_ref.dtype)


def matmul(x2, w, out_dtype, tm):
    t, d = x2.shape
    n = w.shape[1]
    return pl.pallas_call(
        _matmul_kernel,
        out_shape=jax.ShapeDtypeStruct((t, n), out_dtype),
        grid=(t // tm,),
        in_specs=[pl.BlockSpec((tm, d), lambda i: (i, 0)), pl.BlockSpec((d, n), lambda i: (0, 0))],
        out_specs=pl.BlockSpec((tm, n), lambda i: (i, 0)),
        compiler_params=_params("parallel"),
        name="mem_kv_proj",
    )(x2, w)


def _xattn_kernel(x_ref, wq_ref, k_ref, v_ref, wo_ref, g_ref, b_ref, o_ref, *, alpha):
    x = x_ref[0]
    q = _dot(x.astype(MXU_DT), wq_ref[...]).astype(MXU_DT)
    outs = []
    for h in range(XA_HEADS):
        s = _dot_nt(q[:, h * DH:(h + 1) * DH], k_ref[0, h])
        p = jnp.exp(s - jnp.max(s, axis=1, keepdims=True))
        l = jnp.sum(p, axis=1, keepdims=True)
        outs.append(_dot(p.astype(MXU_DT), v_ref[0, h]) / l)
    o = jnp.concatenate(outs, axis=1).astype(MXU_DT)
    y = _dot(o, wo_ref[...])
    o_ref[0] = _layer_norm(alpha * x + y, g_ref[...], b_ref[...])


def cross_attention(x, k, v, wq, wo, ln_g, ln_b, alpha, tm=256):
    b, s, d = x.shape
    m = k.shape[2]
    fixed = lambda bi, i: (0, 0)
    return pl.pallas_call(
        functools.partial(_xattn_kernel, alpha=alpha),
        out_shape=jax.ShapeDtypeStruct((b, s, d), F32),
        grid=(b, s // tm),
        in_specs=[pl.BlockSpec((1, tm, d), lambda bi, i: (bi, i, 0)),
                  pl.BlockSpec(wq.shape, fixed),
                  pl.BlockSpec((1, XA_HEADS, m, DH), lambda bi, i: (bi, 0, 0, 0)),
                  pl.BlockSpec((1, XA_HEADS, m, DH), lambda bi, i: (bi, 0, 0, 0)),
                  pl.BlockSpec(wo.shape, fixed),
                  pl.BlockSpec((1, d), fixed), pl.BlockSpec((1, d), fixed)],
        out_specs=pl.BlockSpec((1, tm, d), lambda bi, i: (bi, i, 0)),
        compiler_params=_params("parallel", "parallel"),
        name="cross_attn",
    )(x, wq, k, v, wo, ln_g, ln_b)


def _mlp_kernel(x_ref, wu_ref, wd_ref, g_ref, b_ref, o_ref, acc_ref, *, alpha):
    j = pl.program_id(1)
    x = x_ref[...]
    hdn = jnp.square(jnp.maximum(_dot(x.astype(MXU_DT), wu_ref[...]), 0.0))
    part = _dot(hdn.astype(MXU_DT), wd_ref[...])

    @pl.when(j == 0)
    def _():
        acc_ref[...] = part

    @pl.when(j > 0)
    def _():
        acc_ref[...] += part

    @pl.when(j == pl.num_programs(1) - 1)
    def _():
        o_ref[...] = _layer_norm(alpha * x + acc_ref[...], g_ref[...], b_ref[...])


def mlp(x2, wu, wd, ln_g, ln_b, alpha, tm=512, tf=1024):
    t, d = x2.shape
    f = wu.shape[1]
    return pl.pallas_call(
        functools.partial(_mlp_kernel, alpha=alpha),
        out_shape=jax.ShapeDtypeStruct((t, d), F32),
        grid=(t // tm, f // tf),
        in_specs=[pl.BlockSpec((tm, d), lambda i, j: (i, 0)),
                  pl.BlockSpec((d, tf), lambda i, j: (0, j)),
                  pl.BlockSpec((tf, d), lambda i, j: (j, 0)),
                  pl.BlockSpec((1, d), lambda i, j: (0, 0)),
                  pl.BlockSpec((1, d), lambda i, j: (0, 0))],
        out_specs=pl.BlockSpec((tm, d), lambda i, j: (i, 0)),
        scratch_shapes=[pltpu.VMEM((tm, d), F32)],
        compiler_params=_params("parallel", "arbitrary"),
        name="mlp",
    )(x2, wu, wd, ln_g, ln_b)


def _heads(a, b, s, h):
    return a.reshape(b, s, h, -1).transpose(0, 2, 1, 3)


def _prep_layer_weights(w_in, cmp_pe, cmp_w1, cmp_w2, mla_q_norm, mla_w_uq, mla_kv_norm, mla_w_ukv,
                        fox_b_f, w_gate, w_br_nsa, w_br_mla, w_br_fox, w_mix_out, xa_w_q, xa_w_kv,
                        xa_w_o, mlp_w_up, mlp_w_down, ln_g, ln_b):
    depth, d, _ = w_in.shape
    sizes = (NSA_W, NSA_KV, NSA_KV, NSA_KV, NSA_KV, NSA_KV, NSA_KV, 3 * NSA_HEADS,
             MLA_Q_LORA, MLA_KV_LORA, MLA_ROPE, FOX_W, FOX_W, FOX_W, FOX_HEADS)
    offs = np.concatenate([[0], np.cumsum(sizes)])
    col = lambda n: w_in[:, :, offs[n]:offs[n + 1]]
    (nq, nkc, nvc, nks, nvs, nkw, nvw, ngate, cq, ckv, kr, fq, fk, fv, ff) = [col(n) for n in range(15)]
    qk_scale = DH ** -0.5
    wa = jnp.concatenate([nq * qk_scale, nks, nvs, nkw, nvw, fq * qk_scale, fk, fv], axis=2)
    pad = jnp.zeros((depth, d, _OFF_KC - (_OFF_FF + FOX_HEADS)), F32)
    wb = jnp.concatenate([cq, ckv, kr, ngate, ff, pad, nkc, nvc], axis=2)

    dq = MLA_NOPE + MLA_ROPE
    half = MLA_ROPE // 2
    nope_cols = np.concatenate([np.arange(h * dq, h * dq + MLA_NOPE) for h in range(MLA_HEADS)])
    x1_cols = np.concatenate([np.arange(h * dq + MLA_NOPE, h * dq + MLA_NOPE + half) for h in range(MLA_HEADS)])
    x2_cols = x1_cols + half
    wuq = mla_w_uq[:, :, np.concatenate([nope_cols, x1_cols, x2_cols])]
    dkv = MLA_NOPE + MLA_V
    kn_cols = np.concatenate([np.arange(h * dkv, h * dkv + MLA_NOPE) for h in range(MLA_HEADS)])
    wukv = mla_w_ukv[:, :, np.concatenate([kn_cols, kn_cols + MLA_NOPE])]

    c = lambda a: a.astype(MXU_DT)
    return dict(
        wa=c(wa), wb=c(wb),
        pe=cmp_pe.reshape(depth, 2, 1, L_CMP * DH), w1=c(cmp_w1), w2=c(cmp_w2),
        q_norm=mla_q_norm[:, None, :], wuq=c(wuq), kv_norm=mla_kv_norm[:, None, :], wukv=c(wukv),
        b_f=fox_b_f, wg=c(w_gate), wn=c(w_br_nsa), wm=c(w_br_mla), wf=c(w_br_fox), wo=c(w_mix_out),
        xq=c(xa_w_q * qk_scale), xkv=c(xa_w_kv), xo=c(xa_w_o),
        wu=c(mlp_w_up), wd=c(mlp_w_down),
        ln_g=ln_g[:, :, None, :], ln_b=ln_b[:, :, None, :],
    )


def _layer(x, mem2, w, tabs, alpha):
    b, s, d = x.shape
    t = b * s
    x2 = x.reshape(t, d)

    ya, yb = in_proj(x2, w["wa"], w["wb"])

    o = 0
    q_nsa = _heads(ya[:, o:o + NSA_W], b, s, NSA_HEADS); o += NSA_W
    kvs = []
    for _ in range(4):
        kvs.append(_heads(ya[:, o:o + NSA_KV], b, s, NSA_GROUPS)); o += NSA_KV
    ks, vs, kw, vw = kvs
    fq = _heads(ya[:, o:o + FOX_W], b, s, FOX_HEADS); o += FOX_W
    fk = _heads(ya[:, o:o + FOX_W], b, s, FOX_HEADS); o += FOX_W
    fv = _heads(ya[:, o:o + FOX_W], b, s, FOX_HEADS)

    nc = s // D_CMP
    xkv = jnp.stack([_heads(yb[:, _OFF_KC:_OFF_KC + NSA_KV], b, s, NSA_GROUPS),
                     _heads(yb[:, _OFF_VC:_OFF_VC + NSA_KV], b, s, NSA_GROUPS)])
    xkv = xkv.reshape(2, b, NSA_GROUPS, nc, D_CMP * DH)
    kvc = compress(xkv, w["pe"], w["w1"], w["w2"])
    gate_logits = yb[:, _OFF_NG:_OFF_NG + 3 * NSA_HEADS].reshape(b, s, 3 * NSA_HEADS)
    o_nsa = nsa_attention(q_nsa, kvc[0], kvc[1], ks, vs, kw, vw, gate_logits,
                          tabs["bias_c"], tabs["bias_w"], tabs["overlap"]).reshape(t, NSA_W)

    qnope, q1, q2, kn, vm, k1, k2 = mla_up(yb, w["q_norm"], w["wuq"], w["kv_norm"], w["wukv"],
                                           tabs["cos4"], tabs["sin4"], s)
    half = MLA_ROPE // 2
    dpad = 128 - (MLA_NOPE + MLA_ROPE)
    q_mla = jnp.concatenate([_heads(qnope, b, s, MLA_HEADS), _heads(q1, b, s, MLA_HEADS),
                             _heads(q2, b, s, MLA_HEADS),
                             jnp.zeros((b, MLA_HEADS, s, dpad), MXU_DT)], axis=-1)
    kr = jnp.concatenate([k1, k2], axis=-1).reshape(b, 1, s, 2 * half)
    k_mla = jnp.concatenate([_heads(kn, b, s, MLA_HEADS),
                             jnp.broadcast_to(kr, (b, MLA_HEADS, s, 2 * half)),
                             jnp.zeros((b, MLA_HEADS, s, dpad), MXU_DT)], axis=-1)
    o_mla = causal_attention(q_mla, k_mla, _heads(vm, b, s, MLA_HEADS))
    o_mla = o_mla.transpose(0, 2, 1, 3).reshape(t, MLA_W)

    f_t = yb[:, _OFF_FF:_OFF_FF + FOX_HEADS].reshape(b, s, FOX_HEADS).transpose(0, 2, 1)
    f_t = f_t.reshape(b * FOX_HEADS, s)
    b_col = jnp.tile(w["b_f"], b).reshape(b * FOX_HEADS, 1)
    ck = fox_cum(f_t, b_col).reshape(b, FOX_HEADS, s)
    o_fox = causal_attention(fq, fk, fv, ck)
    o_fox = o_fox.transpose(0, 2, 1, 3).reshape(t, FOX_W)

    x2 = mixer_out(x2, o_nsa, o_mla, o_fox, w["wg"], w["wn"], w["wm"], w["wf"], w["wo"],
                   w["ln_g"][0], w["ln_b"][0], alpha)

    m = mem2.shape[0] // b
    kv = matmul(mem2, w["xkv"], MXU_DT, tm=min(512, mem2.shape[0]))
    kx = _heads(kv[:, :XA_W], b, m, XA_HEADS)
    vx = _heads(kv[:, XA_W:], b, m, XA_HEADS)
    x3 = cross_attention(x2.reshape(b, s, d), kx, vx, w["xq"], w["xo"], w["ln_g"][1], w["ln_b"][1], alpha)

    x2 = mlp(x3.reshape(t, d), w["wu"], w["wd"], w["ln_g"][2], w["ln_b"][2], alpha)
    return x2.reshape(b, s, d)


def _tables(t5_table, s):
    tq = NSA_TQ
    qi = jnp.arange(tq)[:, None]
    ki = jnp.arange(tq)[None, :]
    idx_w = jnp.concatenate([_t5_bucket(d * tq + qi - ki) for d in range(3)], axis=0).astype(jnp.int32)
    bias_w = bias_table(t5_table, idx_w, tq).reshape(NSA_HEADS, 3, tq, tq)
    nc = s // D_CMP
    cmp_end = jnp.arange(nc) * D_CMP + (L_CMP - 1)
    idx_c = _t5_bucket(jnp.arange(s)[:, None] - cmp_end[None, :]).astype(jnp.int32)
    bias_c = bias_table(t5_table, idx_c, min(512, s))
    n_slc = s // L_SLC
    c_lo = np.arange(nc)[:, None] * D_CMP
    s_lo = np.arange(n_slc)[None, :] * L_SLC
    overlap = np.maximum(np.minimum(c_lo + L_CMP, s_lo + L_SLC) - np.maximum(c_lo, s_lo), 0) / D_CMP
    overlap[nc - 1, :] = 0.0
    half = MLA_ROPE // 2
    inv = ROPE_THETA ** (-jnp.arange(half, dtype=F32) / half)
    ang = jnp.arange(s).astype(F32)[:, None] * inv[None, :]
    return dict(bias_w=bias_w, bias_c=bias_c, overlap=jnp.asarray(overlap, MXU_DT),
                cos4=jnp.tile(jnp.cos(ang), (1, MLA_HEADS)), sin4=jnp.tile(jnp.sin(ang), (1, MLA_HEADS)))


def kernel(x, mem, w_in, cmp_pe, cmp_w1, cmp_w2, t5_table, mla_q_norm, mla_w_uq, mla_kv_norm, mla_w_ukv, fox_b_f, w_gate, w_br_nsa, w_br_mla, w_br_fox, w_mix_out, xa_w_q, xa_w_kv, xa_w_o, mlp_w_up, mlp_w_down, ln_g, ln_b):
    depth = w_in.shape[0]
    b, s, d = x.shape
    alpha = (2 * depth) ** 0.25
    weights = _prep_layer_weights(w_in, cmp_pe, cmp_w1, cmp_w2, mla_q_norm, mla_w_uq, mla_kv_norm,
                                  mla_w_ukv, fox_b_f, w_gate, w_br_nsa, w_br_mla, w_br_fox, w_mix_out,
                                  xa_w_q, xa_w_kv, xa_w_o, mlp_w_up, mlp_w_down, ln_g, ln_b)
    tabs = _tables(t5_table, s)
    mem2 = mem.reshape(-1, d)

    def step(xc, w):
        return _layer(xc, mem2, w, tabs, alpha), None

    out, _ = lax.scan(step, x, weights)
    return out
```

```python
import functools
import math

import numpy as np
import jax
import jax.numpy as jnp
from jax import lax
from jax.experimental import pallas as pl
from jax.experimental.pallas import tpu as pltpu

D_MODEL = 1024
DH = 64
NSA_HEADS = 8
NSA_GROUPS = 2
NSA_R = NSA_HEADS // NSA_GROUPS
L_CMP = 32
D_CMP = 16
CMP_HIDDEN = 128
L_SLC = 64
N_SEL = 8
WINDOW = 512
MLA_HEADS = 4
MLA_NOPE = 64
MLA_ROPE = 32
MLA_V = 64
MLA_Q_LORA = 384
MLA_KV_LORA = 128
ROPE_THETA = 10000.0
FOX_HEADS = 4
XA_HEADS = 4
D_FF = 4 * D_MODEL
T5_BUCKETS = 32
T5_MAX_DIST = 128
NSA_W = NSA_HEADS * DH
NSA_KV = NSA_GROUPS * DH
MLA_W = MLA_HEADS * MLA_V
FOX_W = FOX_HEADS * DH
XA_W = XA_HEADS * DH
LN_EPS = 1e-5
RMS_EPS = 1e-6
NEG = -1e30
FORCE = 1e4

F32 = jnp.float32
MXU_DT = jnp.bfloat16
NSA_TQ = 128
NSA_RQ = NSA_R * NSA_TQ
SEL_TK = 512
ATT_T = 512
VMEM_LIMIT = 48 * 1024 * 1024

TILE_DIAG = 0
TILE_PREV = 1
TILE_FAR = 2
TILE_EDGE = 3
TILE_NONE = 4
N_TILE_KINDS = 5

_NA = NSA_W + 4 * NSA_KV + 3 * FOX_W
_OFF_CQ = 0
_OFF_CKV = MLA_Q_LORA
_OFF_KR = _OFF_CKV + MLA_KV_LORA
_OFF_NG = _OFF_KR + MLA_ROPE
_OFF_FF = _OFF_NG + 3 * NSA_HEADS
_OFF_KC = 640
_OFF_VC = _OFF_KC + NSA_KV
_NB = _OFF_VC + NSA_KV


def _dot(a, b):
    return jnp.dot(a, b, preferred_element_type=F32)


def _dot_nt(a, b):
    return lax.dot_general(a, b, (((1,), (1,)), ((), ())), preferred_element_type=F32)


def _params(*sem):
    return pltpu.CompilerParams(dimension_semantics=sem, vmem_limit_bytes=VMEM_LIMIT)


def _layer_norm(z, g, b):
    mu = jnp.mean(z, axis=-1, keepdims=True)
    zc = z - mu
    var = jnp.mean(zc * zc, axis=-1, keepdims=True)
    return zc * lax.rsqrt(var + LN_EPS) * g + b


def _rms_norm(z, g):
    return z * lax.rsqrt(jnp.mean(z * z, axis=-1, keepdims=True) + RMS_EPS) * g


def _t5_bucket(dist):
    n = jnp.maximum(dist, 0)
    max_exact = T5_BUCKETS // 2
    nf = jnp.maximum(n, 1).astype(F32)
    large = max_exact + (jnp.log(nf / max_exact) / math.log(T5_MAX_DIST / max_exact)
                         * (T5_BUCKETS - max_exact)).astype(jnp.int32)
    large = jnp.minimum(large, T5_BUCKETS - 1)
    return jnp.where(n < max_exact, n, large)


def _bias_table_kernel(t5_ref, idx_ref, o_ref):
    h = pl.program_id(0)
    idx = idx_ref[...]
    acc = jnp.full(idx.shape, NEG, F32)
    for b in range(T5_BUCKETS):
        acc = jnp.where(idx == b, t5_ref[b, h], acc)
    o_ref[0] = acc


def bias_table(t5_table, idx, tr):
    rows, cols = idx.shape
    return pl.pallas_call(
        _bias_table_kernel,
        out_shape=jax.ShapeDtypeStruct((NSA_HEADS, rows, cols), F32),
        grid=(NSA_HEADS, rows // tr),
        in_specs=[pl.BlockSpec(memory_space=pltpu.SMEM),
                  pl.BlockSpec((tr, cols), lambda h, i: (i, 0))],
        out_specs=pl.BlockSpec((1, tr, cols), lambda h, i: (h, i, 0)),
        compiler_params=_params("parallel", "parallel"),
        name="bias_table",
    )(t5_table, idx)


def _in_proj_kernel(x_ref, wa_ref, wb_ref, ya_ref, yb_ref):
    xb = x_ref[...].astype(MXU_DT)
    ya_ref[...] = _dot(xb, wa_ref[...]).astype(ya_ref.dtype)
    yb_ref[...] = _dot(xb, wb_ref[...])


def in_proj(x2, wa, wb, tm=512):
    t, d = x2.shape
    return pl.pallas_call(
        _in_proj_kernel,
        out_shape=(jax.ShapeDtypeStruct((t, _NA), MXU_DT), jax.ShapeDtypeStruct((t, _NB), F32)),
        grid=(t // tm,),
        in_specs=[pl.BlockSpec((tm, d), lambda i: (i, 0)),
                  pl.BlockSpec((d, _NA), lambda i: (0, 0)),
                  pl.BlockSpec((d, _NB), lambda i: (0, 0))],
        out_specs=(pl.BlockSpec((tm, _NA), lambda i: (i, 0)),
                   pl.BlockSpec((tm, _NB), lambda i: (i, 0))),
        compiler_params=_params("parallel"),
        name="in_proj",
    )(x2, wa, wb)


def _mla_up_kernel(yb_ref, qn_ref, wuq_ref, kvn_ref, wukv_ref, cos_ref, sin_ref,
                   qnope_ref, q1_ref, q2_ref, kn_ref, v_ref, k1_ref, k2_ref):
    scale = (MLA_NOPE + MLA_ROPE) ** -0.5
    nn = MLA_HEADS * MLA_NOPE
    half = MLA_ROPE // 2
    nr = MLA_HEADS * half
    cos4 = cos_ref[...]
    sin4 = sin_ref[...]
    cq = yb_ref[:, _OFF_CQ:_OFF_CQ + MLA_Q_LORA]
    yq = _dot(_rms_norm(cq, qn_ref[...]).astype(MXU_DT), wuq_ref[...])
    x1 = yq[:, nn:nn + nr]
    x2 = yq[:, nn + nr:nn + 2 * nr]
    qnope_ref[...] = (yq[:, :nn] * scale).astype(qnope_ref.dtype)
    q1_ref[...] = ((x1 * cos4 - x2 * sin4) * scale).astype(q1_ref.dtype)
    q2_ref[...] = ((x1 * sin4 + x2 * cos4) * scale).astype(q2_ref.dtype)
    ckv = yb_ref[:, _OFF_CKV:_OFF_CKV + MLA_KV_LORA]
    ykv = _dot(_rms_norm(ckv, kvn_ref[...]).astype(MXU_DT), wukv_ref[...])
    kn_ref[...] = ykv[:, :nn].astype(kn_ref.dtype)
    v_ref[...] = ykv[:, nn:].astype(v_ref.dtype)
    c1 = cos4[:, :half]
    s1 = sin4[:, :half]
    r1 = yb_ref[:, _OFF_KR:_OFF_KR + half]
    r2 = yb_ref[:, _OFF_KR + half:_OFF_KR + 2 * half]
    k1_ref[...] = (r1 * c1 - r2 * s1).astype(k1_ref.dtype)
    k2_ref[...] = (r1 * s1 + r2 * c1).astype(k2_ref.dtype)


def mla_up(yb, q_norm, wuq, kv_norm, wukv, cos4, sin4, seq, tm=512):
    t = yb.shape[0]
    nn = MLA_HEADS * MLA_NOPE
    nr = MLA_HEADS * MLA_ROPE // 2
    half = MLA_ROPE // 2
    nps = seq // tm
    row = lambda i: (i, 0)
    fixed = lambda i: (0, 0)
    pos = lambda i: (i % nps, 0)
    shapes = [(nn, MXU_DT), (nr, MXU_DT), (nr, MXU_DT), (nn, MXU_DT),
              (MLA_HEADS * MLA_V, MXU_DT), (half, MXU_DT), (half, MXU_DT)]
    return pl.pallas_call(
        _mla_up_kernel,
        out_shape=tuple(jax.ShapeDtypeStruct((t, n), dt) for n, dt in shapes),
        grid=(t // tm,),
        in_specs=[pl.BlockSpec((tm, _NB), row),
                  pl.BlockSpec((1, MLA_Q_LORA), fixed),
                  pl.BlockSpec(wuq.shape, fixed),
                  pl.BlockSpec((1, MLA_KV_LORA), fixed),
                  pl.BlockSpec(wukv.shape, fixed),
                  pl.BlockSpec((tm, nr), pos),
                  pl.BlockSpec((tm, nr), pos)],
        out_specs=tuple(pl.BlockSpec((tm, n), row) for n, _ in shapes),
        compiler_params=_params("parallel"),
        name="mla_up",
    )(yb, q_norm, wuq, kv_norm, wukv, cos4, sin4)


def _fox_cum_kernel(f_ref, b_ref, o_ref):
    z = f_ref[...] + b_ref[...]
    x = jnp.minimum(z, 0.0) - jnp.log1p(jnp.exp(-jnp.abs(z)))
    n = x.shape[1]
    lane = lax.broadcasted_iota(jnp.int32, x.shape, 1)
    shift = 1
    while shift < n:
        x = x + jnp.where(lane >= shift, pltpu.roll(x, shift, 1), 0.0)
        shift *= 2
    o_ref[...] = x


def fox_cum(f_t, b_col):
    return pl.pallas_call(
        _fox_cum_kernel,
        out_shape=jax.ShapeDtypeStruct(f_t.shape, F32),
        compiler_params=pltpu.CompilerParams(vmem_limit_bytes=VMEM_LIMIT),
        name="fox_cum",
    )(f_t, b_col)


def _online_softmax_step(carry, s, v):
    m, l, acc = carry
    m_new = jnp.maximum(m, jnp.max(s, axis=1, keepdims=True))
    a = jnp.exp(m - m_new)
    p = jnp.exp(s - m_new)
    l = a * l + jnp.sum(p, axis=1, keepdims=True)
    acc = a * acc + _dot(p.astype(MXU_DT), v)
    return m_new, l, acc


def _softmax_init(rows, dv):
    return (jnp.full((rows, 1), NEG, F32), jnp.zeros((rows, 1), F32), jnp.zeros((rows, dv), F32))


def _causal_attn_kernel(*refs, has_decay):
    if has_decay:
        q_ref, k_ref, v_ref, ck_ref, o_ref = refs
    else:
        q_ref, k_ref, v_ref, o_ref = refs
        ck_ref = None
    qi = pl.program_id(2)
    q = q_ref[0, 0]
    tq = q.shape[0]
    dv = v_ref.shape[-1]

    def scores(kt):
        s = _dot_nt(q, k_ref[0, 0, kt])
        if has_decay:
            s = s - ck_ref[0, 0, kt]
        return s

    def body(kt, carry):
        return _online_softmax_step(carry, scores(kt), v_ref[0, 0, kt])

    carry = lax.fori_loop(0, qi, body, _softmax_init(tq, dv))
    rr = lax.broadcasted_iota(jnp.int32, (tq, tq), 0)
    cc = lax.broadcasted_iota(jnp.int32, (tq, tq), 1)
    s = jnp.where(rr >= cc, scores(qi), NEG)
    _, l, acc = _online_softmax_step(carry, s, v_ref[0, 0, qi])
    o_ref[0, 0] = (acc * (1.0 / l)).astype(o_ref.dtype)


def causal_attention(q, k, v, ck=None, t=ATT_T):
    b, h, s, dk = q.shape
    dv = v.shape[-1]
    nt = s // t
    k5 = k.reshape(b, h, nt, t, dk)
    v5 = v.reshape(b, h, nt, t, dv)
    args = [q, k5, v5]
    in_specs = [pl.BlockSpec((1, 1, t, dk), lambda bi, hi, qi: (bi, hi, qi, 0)),
                pl.BlockSpec((1, 1, nt, t, dk), lambda bi, hi, qi: (bi, hi, 0, 0, 0)),
                pl.BlockSpec((1, 1, nt, t, dv), lambda bi, hi, qi: (bi, hi, 0, 0, 0))]
    if ck is not None:
        args.append(ck.reshape(b, h, nt, 1, t))
        in_specs.append(pl.BlockSpec((1, 1, nt, 1, t), lambda bi, hi, qi: (bi, hi, 0, 0, 0)))
    return pl.pallas_call(
        functools.partial(_causal_attn_kernel, has_decay=ck is not None),
        out_shape=jax.ShapeDtypeStruct((b, h, s, dv), MXU_DT),
        grid=(b, h, nt),
        in_specs=in_specs,
        out_specs=pl.BlockSpec((1, 1, t, dv), lambda bi, hi, qi: (bi, hi, qi, 0)),
        compiler_params=_params("parallel", "parallel", "arbitrary"),
        name="causal_attn_decay" if ck is not None else "causal_attn",
    )(*args)


def _compress_kernel(x_ref, pe_ref, w1_ref, w2_ref, o_ref):
    x = x_ref[0, 0, 0]
    nc, half = x.shape
    pe = pe_ref[0]
    lo = _dot((x + pe[:, :half]).astype(MXU_DT), w1_ref[0, :half, :])
    hi = _dot((x + pe[:, half:]).astype(MXU_DT), w1_ref[0, half:, :])
    hid = lo + pltpu.roll(hi, nc - 1, 0)
    act = jax.nn.gelu(hid)
    o_ref[0, 0, 0] = _dot(act.astype(MXU_DT), w2_ref[0]).astype(o_ref.dtype)


def compress(xkv, pe, w1, w2):
    _, b, g, nc, width = xkv.shape
    return pl.pallas_call(
        _compress_kernel,
        out_shape=jax.ShapeDtypeStruct((2, b, g, nc, DH), MXU_DT),
        grid=(2, b, g),
        in_specs=[pl.BlockSpec((1, 1, 1, nc, width), lambda a, bi, gi: (a, bi, gi, 0, 0)),
                  pl.BlockSpec((1, 1, 2 * width), lambda a, bi, gi: (a, 0, 0)),
                  pl.BlockSpec((1, 2 * width, CMP_HIDDEN), lambda a, bi, gi: (a, 0, 0)),
                  pl.BlockSpec((1, CMP_HIDDEN, DH), lambda a, bi, gi: (a, 0, 0))],
        out_specs=pl.BlockSpec((1, 1, 1, nc, DH), lambda a, bi, gi: (a, bi, gi, 0, 0)),
        compiler_params=_params("parallel", "parallel", "parallel"),
        name="nsa_compress",
    )(xkv, pe, w1, w2)


def _nsa_kernel(q_ref, kc_ref, vc_ref, ks_ref, vs_ref, kw_ref, vw_ref, gate_ref,
                bc_ref, bw_ref, ovt_ref, o_ref, *, n_sel):
    i = pl.program_id(1)
    tq, rq = NSA_TQ, NSA_RQ
    nslc = ovt_ref.shape[0]
    nwt = WINDOW // tq + 1
    sub = SEL_TK // tq
    blocks_per_tile = SEL_TK // L_SLC
    t0 = i * tq
    gates = jax.nn.sigmoid(gate_ref[0])

    srow = lax.broadcasted_iota(jnp.int32, (rq, 1), 0)
    has_cmp = (t0 + (srow & (tq - 1))) >= (L_CMP - 1)

    blk = lax.broadcasted_iota(jnp.int32, (nslc, tq), 0)
    blk_f = blk.astype(F32)
    tpos = lax.broadcasted_iota(jnp.int32, (nslc, tq), 1) + t0
    cur = lax.shift_right_logical(tpos, int(math.log2(L_SLC)))
    forced = (blk == 0) | (blk == cur) | (blk == cur - 1)
    causal_blk = blk * L_SLC <= tpos
    eye = (lax.broadcasted_iota(jnp.int32, (tq, tq), 0)
           == lax.broadcasted_iota(jnp.int32, (tq, tq), 1)).astype(MXU_DT)

    def window_tile_kind(d):
        static = {0: TILE_DIAG, 1: TILE_PREV, nwt - 1: TILE_EDGE}.get(d, TILE_FAR)
        return jnp.where(i - d < 0, TILE_NONE, static)

    o_fixed = []
    sel_rows = []
    for g in range(NSA_GROUPS):
        qg = q_ref[0, g, 0]

        s = _dot_nt(qg, kc_ref[0, g]) + bc_ref[g, 0]
        e = jnp.exp(s - jnp.max(s, axis=1, keepdims=True))
        inv = jnp.where(has_cmp, 1.0 / jnp.sum(e, axis=1, keepdims=True), 0.0)
        pb = (e * inv).astype(MXU_DT)
        o_cmp = _dot(pb, vc_ref[0, g])
        imp4 = _dot_nt(ovt_ref[...], pb)
        imp = imp4[:, 0:tq]
        for r in range(1, NSA_R):
            imp = imp + imp4[:, r * tq:(r + 1) * tq]

        x = jnp.where(causal_blk, imp + jnp.where(forced, FORCE, 0.0), NEG)
        sel_t = jnp.zeros((nslc, tq), F32)
        for _ in range(n_sel):
            mx = jnp.max(x, axis=0, keepdims=True)
            first = jnp.min(jnp.where(x == mx, blk_f, float(nslc)), axis=0, keepdims=True)
            hit = blk_f == first
            sel_t = jnp.where(hit, 1.0, sel_t)
            x = jnp.where(hit, -3e38, x)
        sel_rows.append(_dot_nt(eye, sel_t.astype(MXU_DT)).astype(MXU_DT))

        tiles = [jnp.maximum(i - d, 0) for d in range(nwt - 1, -1, -1)]
        kcat = jnp.concatenate([kw_ref[0, g, kt] for kt in tiles], axis=0)
        vcat = jnp.concatenate([vw_ref[0, g, kt] for kt in tiles], axis=0)
        bias = jnp.concatenate([bw_ref[g, window_tile_kind(d)] for d in range(nwt - 1, -1, -1)], axis=1)
        s = _dot_nt(qg, kcat) + bias
        e = jnp.exp(s - jnp.max(s, axis=1, keepdims=True))
        o_win = _dot(e.astype(MXU_DT), vcat) * (1.0 / jnp.sum(e, axis=1, keepdims=True))

        gcol = lambda j, g=g: jnp.concatenate(
            [gates[:, 3 * (g * NSA_R + r) + j:3 * (g * NSA_R + r) + j + 1] for r in range(NSA_R)], axis=0)
        o_fixed.append((gcol(0) * o_cmp + gcol(2) * o_win, gcol(1)))

    erow = lax.broadcasted_iota(jnp.int32, (nslc, SEL_TK), 0)
    ecol = lax.shift_right_logical(lax.broadcasted_iota(jnp.int32, (nslc, SEL_TK), 1),
                                   int(math.log2(L_SLC)))

    def sel_body(kt, carry):
        expand = (erow == blocks_per_tile * kt + ecol).astype(MXU_DT)
        kinds = []
        for j in range(sub):
            d = i - (kt * sub + j)
            kinds.append(jnp.where(d < 0, TILE_NONE, jnp.minimum(d, TILE_FAR)))
        out = []
        for g in range(NSA_GROUPS):
            member = _dot(sel_rows[g], expand) > 0.5
            bias = jnp.concatenate([bw_ref[g, kd] for kd in kinds], axis=1)
            s = _dot_nt(q_ref[0, g, 0], ks_ref[0, g, kt]) + bias
            s = jnp.concatenate([jnp.where(member, s[r * tq:(r + 1) * tq], NEG)
                                 for r in range(NSA_R)], axis=0)
            out.append(_online_softmax_step(carry[g], s, vs_ref[0, g, kt]))
        return tuple(out)

    init = tuple(_softmax_init(rq, DH) for _ in range(NSA_GROUPS))
    res = lax.fori_loop(0, i // sub + 1, sel_body, init)

    for g in range(NSA_GROUPS):
        _, l, acc = res[g]
        fixed, g_sel = o_fixed[g]
        out = fixed + g_sel * (acc * (1.0 / l))
        for r in range(NSA_R):
            h = g * NSA_R + r
            o_ref[0, :, h * DH:(h + 1) * DH] = out[r * tq:(r + 1) * tq].astype(o_ref.dtype)


def nsa_attention(q, kc, vc, ks, vs, kw, vw, gate_logits, bias_c, bias_w, overlap_t):
    b, _, nt, rq, _ = q.shape
    tq = NSA_TQ
    s = nt * tq
    ncp = kc.shape[2]
    nslc = overlap_t.shape[0]
    nst = s // SEL_TK
    sel5 = lambda a: a.reshape(b, NSA_GROUPS, nst, SEL_TK, DH)
    win5 = lambda a: a.reshape(b, NSA_GROUPS, nt, tq, DH)
    per_b_sel = pl.BlockSpec((1, NSA_GROUPS, nst, SEL_TK, DH), lambda bi, i: (bi, 0, 0, 0, 0))
    per_b_win = pl.BlockSpec((1, NSA_GROUPS, nt, tq, DH), lambda bi, i: (bi, 0, 0, 0, 0))
    per_b_cmp = pl.BlockSpec((1, NSA_GROUPS, ncp, DH), lambda bi, i: (bi, 0, 0, 0))
    return pl.pallas_call(
        functools.partial(_nsa_kernel, n_sel=min(N_SEL, nslc)),
        out_shape=jax.ShapeDtypeStruct((b, s, NSA_W), MXU_DT),
        grid=(b, nt),
        in_specs=[pl.BlockSpec((1, NSA_GROUPS, 1, rq, DH), lambda bi, i: (bi, 0, i, 0, 0)),
                  per_b_cmp, per_b_cmp, per_b_sel, per_b_sel, per_b_win, per_b_win,
                  pl.BlockSpec((1, tq, 3 * NSA_HEADS), lambda bi, i: (bi, i, 0)),
                  pl.BlockSpec((NSA_GROUPS, 1, rq, ncp), lambda bi, i: (0, i, 0, 0)),
                  pl.BlockSpec(bias_w.shape, lambda bi, i: (0, 0, 0, 0)),
                  pl.BlockSpec(overlap_t.shape, lambda bi, i: (0, 0))],
        out_specs=pl.BlockSpec((1, tq, NSA_W), lambda bi, i: (bi, i, 0)),
        compiler_params=_params("parallel", "arbitrary"),
        name="nsa_attn",
    )(q, kc, vc, sel5(ks), sel5(vs), win5(kw), win5(vw), gate_logits, bias_c, bias_w, overlap_t)


def _mix_kernel(x_ref, on_ref, om_ref, of_ref, wg_ref, wn_ref, wm_ref, wf_ref, wo_ref,
                g_ref, b_ref, o_ref, *, alpha):
    x = x_ref[...]
    xb = x.astype(MXU_DT)
    d = x.shape[1]
    merged = (jax.nn.sigmoid(_dot(xb, wg_ref[:, 0:d])) * _dot(on_ref[...], wn_ref[...])
              + jax.nn.sigmoid(_dot(xb, wg_ref[:, d:2 * d])) * _dot(om_ref[...], wm_ref[...])
              + jax.nn.sigmoid(_dot(xb, wg_ref[:, 2 * d:3 * d])) * _dot(of_ref[...], wf_ref[...]))
    y = _dot(merged.astype(MXU_DT), wo_ref[...])
    o_ref[...] = _layer_norm(alpha * x + y, g_ref[...], b_ref[...])


def mixer_out(x2, o_nsa, o_mla, o_fox, wg, wn, wm, wf, wo, ln_g, ln_b, alpha, tm=256):
    t, d = x2.shape
    row = lambda i: (i, 0)
    fixed = lambda i: (0, 0)
    return pl.pallas_call(
        functools.partial(_mix_kernel, alpha=alpha),
        out_shape=jax.ShapeDtypeStruct((t, d), F32),
        grid=(t // tm,),
        in_specs=[pl.BlockSpec((tm, d), row),
                  pl.BlockSpec((tm, NSA_W), row),
                  pl.BlockSpec((tm, MLA_W), row),
                  pl.BlockSpec((tm, FOX_W), row),
                  pl.BlockSpec(wg.shape, fixed), pl.BlockSpec(wn.shape, fixed),
                  pl.BlockSpec(wm.shape, fixed), pl.BlockSpec(wf.shape, fixed),
                  pl.BlockSpec(wo.shape, fixed),
                  pl.BlockSpec((1, d), fixed), pl.BlockSpec((1, d), fixed)],
        out_specs=pl.BlockSpec((tm, d), row),
        compiler_params=_params("parallel"),
        name="mixer_out",
    )(x2, o_nsa, o_mla, o_fox, wg, wn, wm, wf, wo, ln_g, ln_b)


def _matmul_kernel(x_ref, w_ref, o_ref):
    o_ref[...] = _dot(x_ref[...].astype(MXU_DT), w_ref[...]).astype(o_ref.dtype)


def matmul(x2, w, out_dtype, tm):
    t, d = x2.shape
    n = w.shape[1]
    return pl.pallas_call(
        _matmul_kernel,
        out_shape=jax.ShapeDtypeStruct((t, n), out_dtype),
        grid=(t // tm,),
        in_specs=[pl.BlockSpec((tm, d), lambda i: (i, 0)), pl.BlockSpec((d, n), lambda i: (0, 0))],
        out_specs=pl.BlockSpec((tm, n), lambda i: (i, 0)),
        compiler_params=_params("parallel"),
        name="mem_kv_proj",
    )(x2, w)


def _xattn_kernel(x_ref, wq_ref, k_ref, v_ref, wo_ref, g_ref, b_ref, o_ref, *, alpha):
    x = x_ref[0]
    q = _dot(x.astype(MXU_DT), wq_ref[...]).astype(MXU_DT)
    outs = []
    for h in range(XA_HEADS):
        s = _dot_nt(q[:, h * DH:(h + 1) * DH], k_ref[0, h])
        p = jnp.exp(s - jnp.max(s, axis=1, keepdims=True))
        l = jnp.sum(p, axis=1, keepdims=True)
        outs.append(_dot(p.astype(MXU_DT), v_ref[0, h]) * (1.0 / l))
    o = jnp.concatenate(outs, axis=1).astype(MXU_DT)
    y = _dot(o, wo_ref[...])
    o_ref[0] = _layer_norm(alpha * x + y, g_ref[...], b_ref[...])


def cross_attention(x, k, v, wq, wo, ln_g, ln_b, alpha, tm=256):
    b, s, d = x.shape
    m = k.shape[2]
    fixed = lambda bi, i: (0, 0)
    return pl.pallas_call(
        functools.partial(_xattn_kernel, alpha=alpha),
        out_shape=jax.ShapeDtypeStruct((b, s, d), F32),
        grid=(b, s // tm),
        in_specs=[pl.BlockSpec((1, tm, d), lambda bi, i: (bi, i, 0)),
                  pl.BlockSpec(wq.shape, fixed),
                  pl.BlockSpec((1, XA_HEADS, m, DH), lambda bi, i: (bi, 0, 0, 0)),
                  pl.BlockSpec((1, XA_HEADS, m, DH), lambda bi, i: (bi, 0, 0, 0)),
                  pl.BlockSpec(wo.shape, fixed),
                  pl.BlockSpec((1, d), fixed), pl.BlockSpec((1, d), fixed)],
        out_specs=pl.BlockSpec((1, tm, d), lambda bi, i: (bi, i, 0)),
        compiler_params=_params("parallel", "parallel"),
        name="cross_attn",
    )(x, wq, k, v, wo, ln_g, ln_b)


def _mlp_kernel(x_ref, wu_ref, wd_ref, g_ref, b_ref, o_ref, acc_ref, *, alpha):
    j = pl.program_id(1)
    x = x_ref[...]
    hdn = jnp.square(jnp.maximum(_dot(x.astype(MXU_DT), wu_ref[...]), 0.0))
    part = _dot(hdn.astype(MXU_DT), wd_ref[...])

    @pl.when(j == 0)
    def _():
        acc_ref[...] = part

    @pl.when(j > 0)
    def _():
        acc_ref[...] += part

    @pl.when(j == pl.num_programs(1) - 1)
    def _():
        o_ref[...] = _layer_norm(alpha * x + acc_ref[...], g_ref[...], b_ref[...])


def mlp(x2, wu, wd, ln_g, ln_b, alpha, tm=512, tf=1024):
    t, d = x2.shape
    f = wu.shape[1]
    return pl.pallas_call(
        functools.partial(_mlp_kernel, alpha=alpha),
        out_shape=jax.ShapeDtypeStruct((t, d), F32),
        grid=(t // tm, f // tf),
        in_specs=[pl.BlockSpec((tm, d), lambda i, j: (i, 0)),
                  pl.BlockSpec((d, tf), lambda i, j: (0, j)),
                  pl.BlockSpec((tf, d), lambda i, j: (j, 0)),
                  pl.BlockSpec((1, d), lambda i, j: (0, 0)),
                  pl.BlockSpec((1, d), lambda i, j: (0, 0))],
        out_specs=pl.BlockSpec((tm, d), lambda i, j: (i, 0)),
        scratch_shapes=[pltpu.VMEM((tm, d), F32)],
        compiler_params=_params("parallel", "arbitrary"),
        name="mlp",
    )(x2, wu, wd, ln_g, ln_b)


def _heads(a, b, s, h):
    return a.reshape(b, s, h, -1).transpose(0, 2, 1, 3)


def _prep_layer_weights(w_in, cmp_pe, cmp_w1, cmp_w2, mla_q_norm, mla_w_uq, mla_kv_norm, mla_w_ukv,
                        fox_b_f, w_gate, w_br_nsa, w_br_mla, w_br_fox, w_mix_out, xa_w_q, xa_w_kv,
                        xa_w_o, mlp_w_up, mlp_w_down, ln_g, ln_b):
    depth, d, _ = w_in.shape
    sizes = (NSA_W, NSA_KV, NSA_KV, NSA_KV, NSA_KV, NSA_KV, NSA_KV, 3 * NSA_HEADS,
             MLA_Q_LORA, MLA_KV_LORA, MLA_ROPE, FOX_W, FOX_W, FOX_W, FOX_HEADS)
    offs = np.concatenate([[0], np.cumsum(sizes)])
    col = lambda n: w_in[:, :, offs[n]:offs[n + 1]]
    (nq, nkc, nvc, nks, nvs, nkw, nvw, ngate, cq, ckv, kr, fq, fk, fv, ff) = [col(n) for n in range(15)]
    qk_scale = DH ** -0.5
    wa = jnp.concatenate([nq * qk_scale, nks, nvs, nkw, nvw, fq * qk_scale, fk, fv], axis=2)
    pad = jnp.zeros((depth, d, _OFF_KC - (_OFF_FF + FOX_HEADS)), F32)
    wb = jnp.concatenate([cq, ckv, kr, ngate, ff, pad, nkc, nvc], axis=2)

    dq = MLA_NOPE + MLA_ROPE
    half = MLA_ROPE // 2
    nope_cols = np.concatenate([np.arange(h * dq, h * dq + MLA_NOPE) for h in range(MLA_HEADS)])
    x1_cols = np.concatenate([np.arange(h * dq + MLA_NOPE, h * dq + MLA_NOPE + half) for h in range(MLA_HEADS)])
    x2_cols = x1_cols + half
    wuq = mla_w_uq[:, :, np.concatenate([nope_cols, x1_cols, x2_cols])]
    dkv = MLA_NOPE + MLA_V
    kn_cols = np.concatenate([np.arange(h * dkv, h * dkv + MLA_NOPE) for h in range(MLA_HEADS)])
    wukv = mla_w_ukv[:, :, np.concatenate([kn_cols, kn_cols + MLA_NOPE])]

    c = lambda a: a.astype(MXU_DT)
    return dict(
        wa=c(wa), wb=c(wb),
        pe=cmp_pe.reshape(depth, 2, 1, L_CMP * DH), w1=c(cmp_w1), w2=c(cmp_w2),
        q_norm=mla_q_norm[:, None, :], wuq=c(wuq), kv_norm=mla_kv_norm[:, None, :], wukv=c(wukv),
        b_f=fox_b_f, wg=c(w_gate), wn=c(w_br_nsa), wm=c(w_br_mla), wf=c(w_br_fox), wo=c(w_mix_out),
        xq=c(xa_w_q * qk_scale), xkv=c(xa_w_kv), xo=c(xa_w_o),
        wu=c(mlp_w_up), wd=c(mlp_w_down),
        ln_g=ln_g[:, :, None, :], ln_b=ln_b[:, :, None, :],
    )


def _layer(x, mem2, w, tabs, alpha):
    b, s, d = x.shape
    t = b * s
    x2 = x.reshape(t, d)

    ya, yb = in_proj(x2, w["wa"], w["wb"])

    nt = s // NSA_TQ
    q_nsa = ya[:, :NSA_W].reshape(b, nt, NSA_TQ, NSA_GROUPS, NSA_R, DH)
    q_nsa = q_nsa.transpose(0, 3, 1, 4, 2, 5).reshape(b, NSA_GROUPS, nt, NSA_RQ, DH)
    o = NSA_W
    kvs = []
    for _ in range(4):
        kvs.append(_heads(ya[:, o:o + NSA_KV], b, s, NSA_GROUPS)); o += NSA_KV
    ks, vs, kw, vw = kvs
    fq = _heads(ya[:, o:o + FOX_W], b, s, FOX_HEADS); o += FOX_W
    fk = _heads(ya[:, o:o + FOX_W], b, s, FOX_HEADS); o += FOX_W
    fv = _heads(ya[:, o:o + FOX_W], b, s, FOX_HEADS)

    nc = s // D_CMP
    xkv = jnp.stack([_heads(yb[:, _OFF_KC:_OFF_KC + NSA_KV], b, s, NSA_GROUPS),
                     _heads(yb[:, _OFF_VC:_OFF_VC + NSA_KV], b, s, NSA_GROUPS)])
    xkv = xkv.reshape(2, b, NSA_GROUPS, nc, D_CMP * DH)
    kvc = compress(xkv, w["pe"], w["w1"], w["w2"])
    gate_logits = yb[:, _OFF_NG:_OFF_NG + 3 * NSA_HEADS].reshape(b, s, 3 * NSA_HEADS)
    o_nsa = nsa_attention(q_nsa, kvc[0], kvc[1], ks, vs, kw, vw, gate_logits,
                          tabs["bias_c"], tabs["bias_w"], tabs["overlap_t"]).reshape(t, NSA_W)

    qnope, q1, q2, kn, vm, k1, k2 = mla_up(yb, w["q_norm"], w["wuq"], w["kv_norm"], w["wukv"],
                                           tabs["cos4"], tabs["sin4"], s)
    half = MLA_ROPE // 2
    dpad = 128 - (MLA_NOPE + MLA_ROPE)
    q_mla = jnp.concatenate([_heads(qnope, b, s, MLA_HEADS), _heads(q1, b, s, MLA_HEADS),
                             _heads(q2, b, s, MLA_HEADS),
                             jnp.zeros((b, MLA_HEADS, s, dpad), MXU_DT)], axis=-1)
    kr = jnp.concatenate([k1, k2], axis=-1).reshape(b, 1, s, 2 * half)
    k_mla = jnp.concatenate([_heads(kn, b, s, MLA_HEADS),
                             jnp.broadcast_to(kr, (b, MLA_HEADS, s, 2 * half)),
                             jnp.zeros((b, MLA_HEADS, s, dpad), MXU_DT)], axis=-1)
    o_mla = causal_attention(q_mla, k_mla, _heads(vm, b, s, MLA_HEADS))
    o_mla = o_mla.transpose(0, 2, 1, 3).reshape(t, MLA_W)

    f_t = yb[:, _OFF_FF:_OFF_FF + FOX_HEADS].reshape(b, s, FOX_HEADS).transpose(0, 2, 1)
    f_t = f_t.reshape(b * FOX_HEADS, s)
    b_col = jnp.tile(w["b_f"], b).reshape(b * FOX_HEADS, 1)
    ck = fox_cum(f_t, b_col).reshape(b, FOX_HEADS, s)
    o_fox = causal_attention(fq, fk, fv, ck)
    o_fox = o_fox.transpose(0, 2, 1, 3).reshape(t, FOX_W)

    x2 = mixer_out(x2, o_nsa, o_mla, o_fox, w["wg"], w["wn"], w["wm"], w["wf"], w["wo"],
                   w["ln_g"][0], w["ln_b"][0], alpha)

    m = mem2.shape[0] // b
    kv = matmul(mem2, w["xkv"], MXU_DT, tm=min(512, mem2.shape[0]))
    kx = _heads(kv[:, :XA_W], b, m, XA_HEADS)
    vx = _heads(kv[:, XA_W:], b, m, XA_HEADS)
    x3 = cross_attention(x2.reshape(b, s, d), kx, vx, w["xq"], w["xo"], w["ln_g"][1], w["ln_b"][1], alpha)

    x2 = mlp(x3.reshape(t, d), w["wu"], w["wd"], w["ln_g"][2], w["ln_b"][2], alpha)
    return x2.reshape(b, s, d)


def _group_stack(a):
    _, n, tq, c = a.shape
    a = a.reshape(NSA_GROUPS, NSA_R, n, tq, c).transpose(0, 2, 1, 3, 4)
    return a.reshape(NSA_GROUPS, n, NSA_R * tq, c)


def _tables(t5_table, s):
    tq = NSA_TQ
    qi = jnp.arange(tq)[:, None]
    ki = jnp.arange(tq)[None, :]
    last = jnp.full((tq, tq), T5_BUCKETS - 1, jnp.int32)
    masked = jnp.full((tq, tq), -1, jnp.int32)
    kinds = [None] * N_TILE_KINDS
    kinds[TILE_DIAG] = jnp.where(qi >= ki, _t5_bucket(qi - ki), -1)
    kinds[TILE_PREV] = _t5_bucket(tq + qi - ki)
    kinds[TILE_FAR] = last
    kinds[TILE_EDGE] = jnp.where(qi < ki, last, masked)
    kinds[TILE_NONE] = masked
    idx_w = jnp.concatenate(kinds, axis=0).astype(jnp.int32)
    bias_w = bias_table(t5_table, idx_w, tq).reshape(NSA_HEADS, N_TILE_KINDS, tq, tq)
    bias_w = _group_stack(bias_w)

    nc = s // D_CMP
    cmp_end = jnp.arange(nc) * D_CMP + (L_CMP - 1)
    dist_c = jnp.arange(s)[:, None] - cmp_end[None, :]
    idx_c = jnp.where(dist_c >= 0, _t5_bucket(dist_c), -1).astype(jnp.int32)
    bias_c = bias_table(t5_table, idx_c, min(512, s)).reshape(NSA_HEADS, s // tq, tq, nc)
    bias_c = _group_stack(bias_c)

    n_slc = s // L_SLC
    c_lo = np.arange(nc)[:, None] * D_CMP
    s_lo = np.arange(n_slc)[None, :] * L_SLC
    overlap = np.maximum(np.minimum(c_lo + L_CMP, s_lo + L_SLC) - np.maximum(c_lo, s_lo), 0) / D_CMP
    overlap[nc - 1, :] = 0.0
    half = MLA_ROPE // 2
    inv = ROPE_THETA ** (-jnp.arange(half, dtype=F32) / half)
    ang = jnp.arange(s).astype(F32)[:, None] * inv[None, :]
    return dict(bias_w=bias_w, bias_c=bias_c, overlap_t=jnp.asarray(overlap.T, MXU_DT),
                cos4=jnp.tile(jnp.cos(ang), (1, MLA_HEADS)), sin4=jnp.tile(jnp.sin(ang), (1, MLA_HEADS)))


def kernel(x, mem, w_in, cmp_pe, cmp_w1, cmp_w2, t5_table, mla_q_norm, mla_w_uq, mla_kv_norm, mla_w_ukv, fox_b_f, w_gate, w_br_nsa, w_br_mla, w_br_fox, w_mix_out, xa_w_q, xa_w_kv, xa_w_o, mlp_w_up, mlp_w_down, ln_g, ln_b):
    depth = w_in.shape[0]
    b, s, d = x.shape
    alpha = (2 * depth) ** 0.25
    weights = _prep_layer_weights(w_in, cmp_pe, cmp_w1, cmp_w2, mla_q_norm, mla_w_uq, mla_kv_norm,
                                  mla_w_ukv, fox_b_f, w_gate, w_br_nsa, w_br_mla, w_br_fox, w_mix_out,
                                  xa_w_q, xa_w_kv, xa_w_o, mlp_w_up, mlp_w_down, ln_g, ln_b)
    tabs = _tables(t5_table, s)
    mem2 = mem.reshape(-1, d)

    def step(xc, w):
        return _layer(xc, mem2, w, tabs, alpha), None

    out, _ = lax.scan(step, x, weights)
    return out
```

```python
import functools
import math

import numpy as np
import jax
import jax.numpy as jnp
from jax import lax
from jax.experimental import pallas as pl
from jax.experimental.pallas import tpu as pltpu

D_MODEL = 1024
DH = 64
NSA_HEADS = 8
NSA_GROUPS = 2
NSA_R = NSA_HEADS // NSA_GROUPS
L_CMP = 32
D_CMP = 16
CMP_HIDDEN = 128
L_SLC = 64
N_SEL = 8
WINDOW = 512
MLA_HEADS = 4
MLA_NOPE = 64
MLA_ROPE = 32
MLA_V = 64
MLA_Q_LORA = 384
MLA_KV_LORA = 128
ROPE_THETA = 10000.0
FOX_HEADS = 4
XA_HEADS = 4
D_FF = 4 * D_MODEL
T5_BUCKETS = 32
T5_MAX_DIST = 128
NSA_W = NSA_HEADS * DH
NSA_KV = NSA_GROUPS * DH
MLA_W = MLA_HEADS * MLA_V
FOX_W = FOX_HEADS * DH
XA_W = XA_HEADS * DH
LN_EPS = 1e-5
RMS_EPS = 1e-6
NEG = -1e30
FORCE = 1e4
LOG2E = math.log2(math.e)

F32 = jnp.float32
MXU_DT = jnp.bfloat16
LANES = 128
NSA_TQ = 128
NSA_RQ = NSA_R * NSA_TQ
SEL_TK = 512
ATT_T = 512
VMEM_LIMIT = 48 * 1024 * 1024

TILE_DIAG = 0
TILE_PREV = 1
TILE_FAR = 2
TILE_EDGE = 3
TILE_NONE = 4
N_TILE_KINDS = 5

_NA = NSA_W + 4 * NSA_KV + 3 * FOX_W
_COL_KS = NSA_W // NSA_KV
_COL_FQ = (NSA_W + 4 * NSA_KV) // FOX_W
_OFF_CQ = 0
_OFF_CKV = MLA_Q_LORA
_OFF_KR = _OFF_CKV + MLA_KV_LORA
_OFF_NG = _OFF_KR + MLA_ROPE
_OFF_FF = _OFF_NG + 3 * NSA_HEADS
_NB = 640
_COL_MISC = _OFF_KR // LANES


def _dot(a, b):
    return jnp.dot(a, b, preferred_element_type=F32)


def _dot_nt(a, b):
    return lax.dot_general(a, b, (((1,), (1,)), ((), ())), preferred_element_type=F32)


def _params(*sem):
    return pltpu.CompilerParams(dimension_semantics=sem, vmem_limit_bytes=VMEM_LIMIT)


def _layer_norm(z, g, b):
    mu = jnp.mean(z, axis=-1, keepdims=True)
    zc = z - mu
    var = jnp.mean(zc * zc, axis=-1, keepdims=True)
    return zc * lax.rsqrt(var + LN_EPS) * g + b


def _rms_norm(z, g):
    return z * lax.rsqrt(jnp.mean(z * z, axis=-1, keepdims=True) + RMS_EPS) * g


def _t5_bucket(dist):
    n = jnp.maximum(dist, 0)
    max_exact = T5_BUCKETS // 2
    nf = jnp.maximum(n, 1).astype(F32)
    large = max_exact + (jnp.log(nf / max_exact) / math.log(T5_MAX_DIST / max_exact)
                         * (T5_BUCKETS - max_exact)).astype(jnp.int32)
    large = jnp.minimum(large, T5_BUCKETS - 1)
    return jnp.where(n < max_exact, n, large)


def _with_ones(v):
    return jnp.concatenate([v, jnp.ones(v.shape, v.dtype)], axis=1)


def _softmax_update(m, acc, s, v_aug):
    m_new = jnp.maximum(m, jnp.max(s, axis=1, keepdims=True))
    a = jnp.exp2(m - m_new)
    p = jnp.exp2(s - m_new)
    return m_new, a * acc + _dot(p.astype(MXU_DT), v_aug)


def _softmax_once(s, v_aug):
    p = jnp.exp2(s - jnp.max(s, axis=1, keepdims=True))
    return _dot(p.astype(MXU_DT), v_aug)


def _normalise(acc):
    return acc[:, :DH] * (1.0 / acc[:, DH:])


def _bias_table_kernel(t5_ref, idx_ref, o_ref):
    h = pl.program_id(0)
    idx = idx_ref[...]
    acc = jnp.full(idx.shape, NEG, F32)
    for b in range(T5_BUCKETS):
        acc = jnp.where(idx == b, t5_ref[b, h] * LOG2E, acc)
    o_ref[0] = acc


def bias_table(t5_table, idx, tr):
    rows, cols = idx.shape
    return pl.pallas_call(
        _bias_table_kernel,
        out_shape=jax.ShapeDtypeStruct((NSA_HEADS, rows, cols), F32),
        grid=(NSA_HEADS, rows // tr),
        in_specs=[pl.BlockSpec(memory_space=pltpu.SMEM),
                  pl.BlockSpec((tr, cols), lambda h, i: (i, 0))],
        out_specs=pl.BlockSpec((1, tr, cols), lambda h, i: (h, i, 0)),
        compiler_params=_params("parallel", "parallel"),
        name="bias_table",
    )(t5_table, idx)


def _in_proj_kernel(x_ref, wa_ref, wb_ref, wc_ref, sc_ref, ya_ref, yb_ref, ykc_ref, yvc_ref):
    xb = x_ref[...].astype(MXU_DT)
    ya_ref[...] = (_dot(xb, wa_ref[...]) * sc_ref[...]).astype(ya_ref.dtype)
    yb_ref[...] = _dot(xb, wb_ref[...])
    yc = _dot(xb, wc_ref[...])
    ykc_ref[...] = yc[:, :NSA_KV]
    yvc_ref[...] = yc[:, NSA_KV:]


def in_proj(x2, wa, wb, wc, col_scale, tm=512):
    t, d = x2.shape
    row = lambda i: (i, 0)
    fixed = lambda i: (0, 0)
    return pl.pallas_call(
        _in_proj_kernel,
        out_shape=(jax.ShapeDtypeStruct((t, _NA), MXU_DT), jax.ShapeDtypeStruct((t, _NB), F32),
                   jax.ShapeDtypeStruct((t, NSA_KV), F32), jax.ShapeDtypeStruct((t, NSA_KV), F32)),
        grid=(t // tm,),
        in_specs=[pl.BlockSpec((tm, d), row), pl.BlockSpec(wa.shape, fixed),
                  pl.BlockSpec(wb.shape, fixed), pl.BlockSpec(wc.shape, fixed),
                  pl.BlockSpec((1, _NA), fixed)],
        out_specs=(pl.BlockSpec((tm, _NA), row), pl.BlockSpec((tm, _NB), row),
                   pl.BlockSpec((tm, NSA_KV), row), pl.BlockSpec((tm, NSA_KV), row)),
        compiler_params=_params("parallel"),
        name="in_proj",
    )(x2, wa, wb, wc, col_scale)


def _mla_up_kernel(yb_ref, qn_ref, wqa_ref, wqb_ref, kvn_ref, wk_ref, wv_ref, pa_ref, pb_ref,
                   ct_ref, st_ref, q_ref, k_ref, v_ref):
    scale = (MLA_NOPE + MLA_ROPE) ** -0.5 * LOG2E
    ct = ct_ref[...]
    st = st_ref[...]
    cq = _rms_norm(yb_ref[:, _OFF_CQ:_OFF_CQ + MLA_Q_LORA], qn_ref[...]).astype(MXU_DT)
    q = _dot(cq, wqa_ref[...]) * ct + _dot(cq, wqb_ref[...]) * st
    q_ref[...] = (q * scale).astype(q_ref.dtype)
    ckv = _rms_norm(yb_ref[:, _OFF_CKV:_OFF_CKV + MLA_KV_LORA], kvn_ref[...]).astype(MXU_DT)
    v_ref[...] = _dot(ckv, wv_ref[...]).astype(v_ref.dtype)
    kr = yb_ref[:, _OFF_KR:_OFF_KR + MLA_ROPE]
    kr_hi = kr.astype(MXU_DT)
    kr_lo = (kr - kr_hi.astype(F32)).astype(MXU_DT)
    ka = _dot(kr_hi, pa_ref[...]) + _dot(kr_lo, pa_ref[...])
    kb = _dot(kr_hi, pb_ref[...]) + _dot(kr_lo, pb_ref[...])
    k_ref[...] = (_dot(ckv, wk_ref[...]) + ka * ct + kb * st).astype(k_ref.dtype)


def mla_up(yb, w, tabs, seq, tm=512):
    t = yb.shape[0]
    nps = seq // tm
    row = lambda i: (i, 0)
    fixed = lambda i: (0, 0)
    pos = lambda i: (i % nps, 0)
    wide = MLA_HEADS * LANES
    ins = [w["q_norm"], w["wqa"], w["wqb"], w["kv_norm"], w["wk"], w["wv"], tabs["rope_pa"], tabs["rope_pb"]]
    return pl.pallas_call(
        _mla_up_kernel,
        out_shape=(jax.ShapeDtypeStruct((t, wide), MXU_DT), jax.ShapeDtypeStruct((t, wide), MXU_DT),
                   jax.ShapeDtypeStruct((t, MLA_W), MXU_DT)),
        grid=(t // tm,),
        in_specs=[pl.BlockSpec((tm, _NB), row)] + [pl.BlockSpec(a.shape, fixed) for a in ins]
        + [pl.BlockSpec((tm, wide), pos), pl.BlockSpec((tm, wide), pos)],
        out_specs=(pl.BlockSpec((tm, wide), row), pl.BlockSpec((tm, wide), row),
                   pl.BlockSpec((tm, MLA_W), row)),
        compiler_params=_params("parallel"),
        name="mla_up",
    )(yb, *ins, tabs["rope_cos"], tabs["rope_sin"])


def _fox_cum_kernel(f_ref, b_ref, o_ref):
    z = f_ref[...] + b_ref[...]
    x = jnp.minimum(z, 0.0) - jnp.log1p(jnp.exp(-jnp.abs(z)))
    n = x.shape[1]
    lane = lax.broadcasted_iota(jnp.int32, x.shape, 1)
    shift = 1
    while shift < n:
        x = x + jnp.where(lane >= shift, pltpu.roll(x, shift, 1), 0.0)
        shift *= 2
    o_ref[...] = x * LOG2E


def fox_cum(f_t, b_col):
    return pl.pallas_call(
        _fox_cum_kernel,
        out_shape=jax.ShapeDtypeStruct(f_t.shape, F32),
        compiler_params=pltpu.CompilerParams(vmem_limit_bytes=VMEM_LIMIT),
        name="fox_cum",
    )(f_t, b_col)


def _causal_attn_kernel(*refs, n_heads, dq, has_decay):
    if has_decay:
        q_ref, k_ref, v_ref, ck_ref, o_ref = refs
    else:
        q_ref, k_ref, v_ref, o_ref = refs
        ck_ref = None
    qi = pl.program_id(1)
    t = q_ref.shape[1]
    rr = lax.broadcasted_iota(jnp.int32, (t, t), 0)
    cc = lax.broadcasted_iota(jnp.int32, (t, t), 1)

    for h in range(n_heads):
        q = q_ref[0, :, h * dq:(h + 1) * dq]

        def scores(kt, h=h, q=q):
            rows = pl.ds(pl.multiple_of(kt * t, t), t)
            s = _dot_nt(q, k_ref[0, rows, h * dq:(h + 1) * dq])
            if has_decay:
                s = s - ck_ref[0, h, kt]
            return s

        def values(kt, h=h):
            rows = pl.ds(pl.multiple_of(kt * t, t), t)
            return _with_ones(v_ref[0, rows, h * DH:(h + 1) * DH])

        def body(kt, carry, scores=scores, values=values):
            m, acc, s_cur = carry
            s_next = scores(kt + 1)
            m, acc = _softmax_update(m, acc, s_cur, values(kt))
            return m, acc, s_next

        init = (jnp.full((t, 1), NEG, F32), jnp.zeros((t, 2 * DH), F32), scores(0))
        m, acc, s_diag = lax.fori_loop(0, qi, body, init)
        _, acc = _softmax_update(m, acc, jnp.where(rr >= cc, s_diag, NEG), values(qi))
        o_ref[0, :, h * DH:(h + 1) * DH] = _normalise(acc).astype(o_ref.dtype)


def causal_attention(q, q_col, k, k_col, v, v_col, n_heads, dq, ck=None, t=ATT_T):
    b, s, _ = q.shape
    nt = s // t
    args = [q, k, v]
    in_specs = [pl.BlockSpec((1, t, n_heads * dq), lambda bi, qi: (bi, qi, q_col)),
                pl.BlockSpec((1, s, n_heads * dq), lambda bi, qi: (bi, 0, k_col)),
                pl.BlockSpec((1, s, n_heads * DH), lambda bi, qi: (bi, 0, v_col))]
    if ck is not None:
        args.append(ck.reshape(b, n_heads, nt, 1, t))
        in_specs.append(pl.BlockSpec((1, n_heads, nt, 1, t), lambda bi, qi: (bi, 0, 0, 0, 0)))
    return pl.pallas_call(
        functools.partial(_causal_attn_kernel, n_heads=n_heads, dq=dq, has_decay=ck is not None),
        out_shape=jax.ShapeDtypeStruct((b, s, n_heads * DH), MXU_DT),
        grid=(b, nt),
        in_specs=in_specs,
        out_specs=pl.BlockSpec((1, t, n_heads * DH), lambda bi, qi: (bi, qi, 0)),
        compiler_params=_params("parallel", "arbitrary"),
        name="causal_attn_decay" if ck is not None else "causal_attn",
    )(*args)


def _compress_kernel(x_ref, pe_ref, wlo_ref, whi_ref, w2_ref, o_ref):
    x = x_ref[0]
    nc = x.shape[0]
    lo = _dot((x + pe_ref[0:1, :]).astype(MXU_DT), wlo_ref[...])
    hi = _dot((x + pe_ref[1:2, :]).astype(MXU_DT), whi_ref[...])
    hid = lo + pltpu.roll(hi, nc - 1, 0)
    act = jax.nn.gelu(hid)
    o_ref[0] = _dot(act.astype(MXU_DT), w2_ref[...]).astype(o_ref.dtype)


def compress(x, pe, wlo, whi, w2):
    b, nc, width = x.shape
    fixed = lambda bi: (0, 0)
    return pl.pallas_call(
        _compress_kernel,
        out_shape=jax.ShapeDtypeStruct((b, nc, NSA_KV), MXU_DT),
        grid=(b,),
        in_specs=[pl.BlockSpec((1, nc, width), lambda bi: (bi, 0, 0)),
                  pl.BlockSpec(pe.shape, fixed), pl.BlockSpec(wlo.shape, fixed),
                  pl.BlockSpec(whi.shape, fixed), pl.BlockSpec(w2.shape, fixed)],
        out_specs=pl.BlockSpec((1, nc, NSA_KV), lambda bi: (bi, 0, 0)),
        compiler_params=_params("parallel"),
        name="nsa_compress",
    )(x, pe, wlo, whi, w2)


def _nsa_kernel(q_ref, kc_ref, vc_ref, ks_ref, vs_ref, kw_ref, vw_ref, misc_ref,
                bc_ref, bw_ref, ovt_ref, o_ref, *, n_sel):
    i = pl.program_id(1)
    tq, rq = NSA_TQ, NSA_RQ
    nslc = ovt_ref.shape[0]
    nwt = WINDOW // tq + 1
    sub = SEL_TK // tq
    blocks_per_tile = SEL_TK // L_SLC
    t0 = i * tq
    g_off = _OFF_NG - _COL_MISC * LANES
    gates = jax.nn.sigmoid(misc_ref[0, :, g_off:g_off + 3 * NSA_HEADS])

    srow = lax.broadcasted_iota(jnp.int32, (rq, 1), 0)
    has_cmp = (t0 + (srow & (tq - 1))) >= (L_CMP - 1)

    blk = lax.broadcasted_iota(jnp.int32, (nslc, tq), 0)
    blk_f = blk.astype(F32)
    tpos = lax.broadcasted_iota(jnp.int32, (nslc, tq), 1) + t0
    cur = lax.shift_right_logical(tpos, int(math.log2(L_SLC)))
    forced = (blk == 0) | (blk == cur) | (blk == cur - 1)
    causal_blk = blk * L_SLC <= tpos
    eye = (lax.broadcasted_iota(jnp.int32, (tq, tq), 0)
           == lax.broadcasted_iota(jnp.int32, (tq, tq), 1)).astype(MXU_DT)

    def window_tile_kind(d):
        static = {0: TILE_DIAG, 1: TILE_PREV, nwt - 1: TILE_EDGE}.get(d, TILE_FAR)
        return jnp.where(i - d < 0, TILE_NONE, static)

    def group_cols(g):
        return slice(g * DH, (g + 1) * DH)

    q_groups = []
    o_fixed = []
    sel_rows = []
    for g in range(NSA_GROUPS):
        qg = jnp.concatenate([q_ref[0, :, (g * NSA_R + r) * DH:(g * NSA_R + r + 1) * DH]
                              for r in range(NSA_R)], axis=0)
        q_groups.append(qg)

        s = _dot_nt(qg, kc_ref[0, :, group_cols(g)]) + bc_ref[g, 0]
        e = jnp.exp2(s - jnp.max(s, axis=1, keepdims=True))
        inv = jnp.where(has_cmp, 1.0 / jnp.sum(e, axis=1, keepdims=True), 0.0)
        pb = (e * inv).astype(MXU_DT)
        o_cmp = _dot(pb, vc_ref[0, :, group_cols(g)])
        imp4 = _dot_nt(ovt_ref[...], pb)
        imp = imp4[:, 0:tq]
        for r in range(1, NSA_R):
            imp = imp + imp4[:, r * tq:(r + 1) * tq]

        x = jnp.where(causal_blk, imp + jnp.where(forced, FORCE, 0.0), NEG)
        sel_t = jnp.zeros((nslc, tq), F32)
        for _ in range(n_sel):
            mx = jnp.max(x, axis=0, keepdims=True)
            first = jnp.min(jnp.where(x == mx, blk_f, float(nslc)), axis=0, keepdims=True)
            hit = blk_f == first
            sel_t = jnp.where(hit, 1.0, sel_t)
            x = jnp.where(hit, -3e38, x)
        sel_rows.append(_dot_nt(eye, sel_t.astype(MXU_DT)).astype(MXU_DT))

        rows = [pl.ds(pl.multiple_of(jnp.maximum(i - d, 0) * tq, tq), tq) for d in range(nwt - 1, -1, -1)]
        kcat = jnp.concatenate([kw_ref[0, r, group_cols(g)] for r in rows], axis=0)
        vcat = jnp.concatenate([vw_ref[0, r, group_cols(g)] for r in rows], axis=0)
        bias = jnp.concatenate([bw_ref[g, window_tile_kind(d)] for d in range(nwt - 1, -1, -1)], axis=1)
        o_win = _normalise(_softmax_once(_dot_nt(qg, kcat) + bias, _with_ones(vcat)))

        gcol = lambda j, g=g: jnp.concatenate(
            [gates[:, 3 * (g * NSA_R + r) + j:3 * (g * NSA_R + r) + j + 1] for r in range(NSA_R)], axis=0)
        o_fixed.append((gcol(0) * o_cmp + gcol(2) * o_win, gcol(1)))

    erow = lax.broadcasted_iota(jnp.int32, (nslc, SEL_TK), 0)
    ecol = lax.shift_right_logical(lax.broadcasted_iota(jnp.int32, (nslc, SEL_TK), 1),
                                   int(math.log2(L_SLC)))

    def sel_scores(kt):
        expand = (erow == blocks_per_tile * kt + ecol).astype(MXU_DT)
        kinds = []
        for j in range(sub):
            d = i - (kt * sub + j)
            kinds.append(jnp.where(d < 0, TILE_NONE, jnp.minimum(d, TILE_FAR)))
        rows = pl.ds(pl.multiple_of(kt * SEL_TK, SEL_TK), SEL_TK)
        out = []
        for g in range(NSA_GROUPS):
            member = _dot(sel_rows[g], expand) > 0.5
            bias = jnp.concatenate([bw_ref[g, kd] for kd in kinds], axis=1)
            s = _dot_nt(q_groups[g], ks_ref[0, rows, group_cols(g)]) + bias
            out.append(jnp.concatenate([jnp.where(member, s[r * tq:(r + 1) * tq], NEG)
                                        for r in range(NSA_R)], axis=0))
        return tuple(out)

    def sel_update(kt, state, s_cur):
        rows = pl.ds(pl.multiple_of(kt * SEL_TK, SEL_TK), SEL_TK)
        return tuple(_softmax_update(m, acc, s, _with_ones(vs_ref[0, rows, group_cols(g)]))
                     for g, ((m, acc), s) in enumerate(zip(state, s_cur)))

    def sel_body(kt, carry):
        state, s_cur = carry
        s_next = sel_scores(kt + 1)
        return sel_update(kt, state, s_cur), s_next

    last = i // sub
    init = tuple((jnp.full((rq, 1), NEG, F32), jnp.zeros((rq, 2 * DH), F32)) for _ in range(NSA_GROUPS))
    state, s_last = lax.fori_loop(0, last, sel_body, (init, sel_scores(0)))
    state = sel_update(last, state, s_last)

    for g in range(NSA_GROUPS):
        fixed, g_sel = o_fixed[g]
        out = fixed + g_sel * _normalise(state[g][1])
        for r in range(NSA_R):
            h = g * NSA_R + r
            o_ref[0, :, h * DH:(h + 1) * DH] = out[r * tq:(r + 1) * tq].astype(o_ref.dtype)


def nsa_attention(ya, yb, kc, vc, bias_c, bias_w, overlap_t):
    b, s, _ = ya.shape
    tq = NSA_TQ
    ncp = kc.shape[1]
    per_b_kv = lambda col: pl.BlockSpec((1, s, NSA_KV), lambda bi, i: (bi, 0, col))
    per_b_cmp = pl.BlockSpec((1, ncp, NSA_KV), lambda bi, i: (bi, 0, 0))
    return pl.pallas_call(
        functools.partial(_nsa_kernel, n_sel=min(N_SEL, overlap_t.shape[0])),
        out_shape=jax.ShapeDtypeStruct((b, s, NSA_W), MXU_DT),
        grid=(b, s // tq),
        in_specs=[pl.BlockSpec((1, tq, NSA_W), lambda bi, i: (bi, i, 0)),
                  per_b_cmp, per_b_cmp,
                  per_b_kv(_COL_KS), per_b_kv(_COL_KS + 1), per_b_kv(_COL_KS + 2), per_b_kv(_COL_KS + 3),
                  pl.BlockSpec((1, tq, LANES), lambda bi, i: (bi, i, _COL_MISC)),
                  pl.BlockSpec((NSA_GROUPS, 1, NSA_RQ, ncp), lambda bi, i: (0, i, 0, 0)),
                  pl.BlockSpec(bias_w.shape, lambda bi, i: (0, 0, 0, 0)),
                  pl.BlockSpec(overlap_t.shape, lambda bi, i: (0, 0))],
        out_specs=pl.BlockSpec((1, tq, NSA_W), lambda bi, i: (bi, i, 0)),
        compiler_params=_params("parallel", "arbitrary"),
        name="nsa_attn",
    )(ya, kc, vc, ya, ya, ya, ya, yb, bias_c, bias_w, overlap_t)


def _mix_kernel(x_ref, on_ref, om_ref, of_ref, wg_ref, wn_ref, wm_ref, wf_ref, wo_ref,
                g_ref, b_ref, o_ref, *, alpha):
    x = x_ref[...]
    xb = x.astype(MXU_DT)
    d = x.shape[1]
    merged = (jax.nn.sigmoid(_dot(xb, wg_ref[:, 0:d])) * _dot(on_ref[...], wn_ref[...])
              + jax.nn.sigmoid(_dot(xb, wg_ref[:, d:2 * d])) * _dot(om_ref[...], wm_ref[...])
              + jax.nn.sigmoid(_dot(xb, wg_ref[:, 2 * d:3 * d])) * _dot(of_ref[...], wf_ref[...]))
    y = _dot(merged.astype(MXU_DT), wo_ref[...])
    o_ref[...] = _layer_norm(alpha * x + y, g_ref[...], b_ref[...])


def mixer_out(x2, o_nsa, o_mla, o_fox, wg, wn, wm, wf, wo, ln_g, ln_b, alpha, tm=256):
    t, d = x2.shape
    row = lambda i: (i, 0)
    fixed = lambda i: (0, 0)
    return pl.pallas_call(
        functools.partial(_mix_kernel, alpha=alpha),
        out_shape=jax.ShapeDtypeStruct((t, d), F32),
        grid=(t // tm,),
        in_specs=[pl.BlockSpec((tm, d), row),
                  pl.BlockSpec((tm, NSA_W), row),
                  pl.BlockSpec((tm, MLA_W), row),
                  pl.BlockSpec((tm, FOX_W), row),
                  pl.BlockSpec(wg.shape, fixed), pl.BlockSpec(wn.shape, fixed),
                  pl.BlockSpec(wm.shape, fixed), pl.BlockSpec(wf.shape, fixed),
                  pl.BlockSpec(wo.shape, fixed),
                  pl.BlockSpec((1, d), fixed), pl.BlockSpec((1, d), fixed)],
        out_specs=pl.BlockSpec((tm, d), row),
        compiler_params=_params("parallel"),
        name="mixer_out",
    )(x2, o_nsa, o_mla, o_fox, wg, wn, wm, wf, wo, ln_g, ln_b)


def _matmul_kernel(x_ref, w_ref, o_ref):
    o_ref[...] = _dot(x_ref[...].astype(MXU_DT), w_ref[...]).astype(o_ref.dtype)


def matmul(x2, w, out_dtype, tm):
    t, d = x2.shape
    n = w.shape[1]
    return pl.pallas_call(
        _matmul_kernel,
        out_shape=jax.ShapeDtypeStruct((t, n), out_dtype),
        grid=(t // tm,),
        in_specs=[pl.BlockSpec((tm, d), lambda i: (i, 0)), pl.BlockSpec((d, n), lambda i: (0, 0))],
        out_specs=pl.BlockSpec((tm, n), lambda i: (i, 0)),
        compiler_params=_params("parallel"),
        name="mem_kv_proj",
    )(x2, w)


def _xattn_kernel(x_ref, wq_ref, kv_ref, wo_ref, g_ref, b_ref, o_ref, *, alpha):
    x = x_ref[0]
    q = (_dot(x.astype(MXU_DT), wq_ref[...]) * LOG2E).astype(MXU_DT)
    outs = []
    for h in range(XA_HEADS):
        s = _dot_nt(q[:, h * DH:(h + 1) * DH], kv_ref[0, :, h * DH:(h + 1) * DH])
        v = kv_ref[0, :, XA_W + h * DH:XA_W + (h + 1) * DH]
        outs.append(_normalise(_softmax_once(s, _with_ones(v))))
    o = jnp.concatenate(outs, axis=1).astype(MXU_DT)
    y = _dot(o, wo_ref[...])
    o_ref[0] = _layer_norm(alpha * x + y, g_ref[...], b_ref[...])


def cross_attention(x, kv, wq, wo, ln_g, ln_b, alpha, tm=256):
    b, s, d = x.shape
    m = kv.shape[1]
    fixed = lambda bi, i: (0, 0)
    return pl.pallas_call(
        functools.partial(_xattn_kernel, alpha=alpha),
        out_shape=jax.ShapeDtypeStruct((b, s, d), F32),
        grid=(b, s // tm),
        in_specs=[pl.BlockSpec((1, tm, d), lambda bi, i: (bi, i, 0)),
                  pl.BlockSpec(wq.shape, fixed),
                  pl.BlockSpec((1, m, 2 * XA_W), lambda bi, i: (bi, 0, 0)),
                  pl.BlockSpec(wo.shape, fixed),
                  pl.BlockSpec((1, d), fixed), pl.BlockSpec((1, d), fixed)],
        out_specs=pl.BlockSpec((1, tm, d), lambda bi, i: (bi, i, 0)),
        compiler_params=_params("parallel", "parallel"),
        name="cross_attn",
    )(x, wq, kv, wo, ln_g, ln_b)


def _mlp_kernel(x_ref, wu_ref, wd_ref, g_ref, b_ref, o_ref, acc_ref, *, alpha):
    j = pl.program_id(1)
    x = x_ref[...]
    hdn = jnp.square(jnp.maximum(_dot(x.astype(MXU_DT), wu_ref[...]), 0.0))
    part = _dot(hdn.astype(MXU_DT), wd_ref[...])

    @pl.when(j == 0)
    def _():
        acc_ref[...] = part

    @pl.when(j > 0)
    def _():
        acc_ref[...] += part

    @pl.when(j == pl.num_programs(1) - 1)
    def _():
        o_ref[...] = _layer_norm(alpha * x + acc_ref[...], g_ref[...], b_ref[...])


def mlp(x2, wu, wd, ln_g, ln_b, alpha, tm=512, tf=1024):
    t, d = x2.shape
    f = wu.shape[1]
    return pl.pallas_call(
        functools.partial(_mlp_kernel, alpha=alpha),
        out_shape=jax.ShapeDtypeStruct((t, d), F32),
        grid=(t // tm, f // tf),
        in_specs=[pl.BlockSpec((tm, d), lambda i, j: (i, 0)),
                  pl.BlockSpec((d, tf), lambda i, j: (0, j)),
                  pl.BlockSpec((tf, d), lambda i, j: (j, 0)),
                  pl.BlockSpec((1, d), lambda i, j: (0, 0)),
                  pl.BlockSpec((1, d), lambda i, j: (0, 0))],
        out_specs=pl.BlockSpec((tm, d), lambda i, j: (i, 0)),
        scratch_shapes=[pltpu.VMEM((tm, d), F32)],
        compiler_params=_params("parallel", "arbitrary"),
        name="mlp",
    )(x2, wu, wd, ln_g, ln_b)


def _segment_cols(w, cols_per_head, head_offset):
    out = jnp.zeros(w.shape[:-1] + (len(cols_per_head) * LANES,), w.dtype)
    for h, cols in enumerate(cols_per_head):
        start = h * LANES + head_offset
        out = out.at[..., start:start + len(cols)].set(w[..., np.asarray(cols)])
    return out


def _prep_layer_weights(w_in, cmp_pe, cmp_w1, cmp_w2, mla_q_norm, mla_w_uq, mla_kv_norm, mla_w_ukv,
                        fox_b_f, w_gate, w_br_nsa, w_br_mla, w_br_fox, w_mix_out, xa_w_q, xa_w_kv,
                        xa_w_o, mlp_w_up, mlp_w_down, ln_g, ln_b):
    depth, d, _ = w_in.shape
    sizes = (NSA_W, NSA_KV, NSA_KV, NSA_KV, NSA_KV, NSA_KV, NSA_KV, 3 * NSA_HEADS,
             MLA_Q_LORA, MLA_KV_LORA, MLA_ROPE, FOX_W, FOX_W, FOX_W, FOX_HEADS)
    offs = np.concatenate([[0], np.cumsum(sizes)])
    col = lambda n: w_in[:, :, offs[n]:offs[n + 1]]
    (nq, nkc, nvc, nks, nvs, nkw, nvw, ngate, cq, ckv, kr, fq, fk, fv, ff) = [col(n) for n in range(15)]
    qk_scale = DH ** -0.5
    wa = jnp.concatenate([nq * qk_scale, nks, nvs, nkw, nvw, fq * qk_scale, fk, fv], axis=2)
    pad = jnp.zeros((depth, d, _NB - (_OFF_FF + FOX_HEADS)), F32)
    wb = jnp.concatenate([cq, ckv, kr, ngate, ff, pad], axis=2)
    wc = jnp.concatenate([nkc, nvc], axis=2)

    dq = MLA_NOPE + MLA_ROPE
    half = MLA_ROPE // 2
    heads = range(MLA_HEADS)
    nope = [np.arange(h * dq, h * dq + MLA_NOPE) for h in heads]
    x1 = [np.arange(h * dq + MLA_NOPE, h * dq + MLA_NOPE + half) for h in heads]
    x2 = [c + half for c in x1]
    wqa = (_segment_cols(mla_w_uq, nope, 0) + _segment_cols(mla_w_uq, x1, MLA_NOPE)
           + _segment_cols(mla_w_uq, x2, MLA_NOPE + half))
    wqb = _segment_cols(mla_w_uq, x2, MLA_NOPE) + _segment_cols(mla_w_uq, x1, MLA_NOPE + half)
    dkv = MLA_NOPE + MLA_V
    wk = _segment_cols(mla_w_ukv, [np.arange(h * dkv, h * dkv + MLA_NOPE) for h in heads], 0)
    wv = mla_w_ukv[:, :, np.concatenate([np.arange(h * dkv + MLA_NOPE, (h + 1) * dkv) for h in heads])]

    eye_g = jnp.eye(NSA_GROUPS, dtype=F32)
    w1r = cmp_w1.reshape(depth, 2, L_CMP, DH, CMP_HIDDEN)
    def chunk_weights(w1_half):
        blk = jnp.einsum('ealdj,gh->ealgdhj', w1_half, eye_g)
        return blk.reshape(depth, 2, D_CMP * NSA_KV, NSA_GROUPS * CMP_HIDDEN)
    w1lo = chunk_weights(w1r[:, :, :D_CMP])
    w1hi = chunk_weights(w1r[:, :, D_CMP:])
    w2b = jnp.einsum('eajd,gh->eagjhd', cmp_w2, eye_g).reshape(depth, 2, NSA_GROUPS * CMP_HIDDEN, NSA_KV)
    pe = jnp.broadcast_to(cmp_pe.reshape(depth, 2, 2, D_CMP, 1, DH),
                          (depth, 2, 2, D_CMP, NSA_GROUPS, DH)).reshape(depth, 2, 2, D_CMP * NSA_KV)

    c = lambda a: a.astype(MXU_DT)
    return dict(
        wa=c(wa), wb=c(wb), wc=c(wc),
        pe=pe, w1lo=c(w1lo), w1hi=c(w1hi), w2b=c(w2b),
        q_norm=mla_q_norm[:, None, :], wqa=c(wqa), wqb=c(wqb),
        kv_norm=mla_kv_norm[:, None, :], wk=c(wk), wv=c(wv),
        b_f=fox_b_f, wg=c(w_gate), wn=c(w_br_nsa), wm=c(w_br_mla), wf=c(w_br_fox), wo=c(w_mix_out),
        xq=c(xa_w_q * qk_scale), xkv=c(xa_w_kv), xo=c(xa_w_o),
        wu=c(mlp_w_up), wd=c(mlp_w_down),
        ln_g=ln_g[:, :, None, :], ln_b=ln_b[:, :, None, :],
    )


def _layer(x, mem2, w, tabs, alpha):
    b, s, d = x.shape
    t = b * s
    x2 = x.reshape(t, d)

    ya, yb, ykc, yvc = in_proj(x2, w["wa"], w["wb"], w["wc"], tabs["col_scale"])
    ya3 = ya.reshape(b, s, _NA)
    yb3 = yb.reshape(b, s, _NB)

    nc = s // D_CMP
    kc = compress(ykc.reshape(b, nc, D_CMP * NSA_KV), w["pe"][0], w["w1lo"][0], w["w1hi"][0], w["w2b"][0])
    vc = compress(yvc.reshape(b, nc, D_CMP * NSA_KV), w["pe"][1], w["w1lo"][1], w["w1hi"][1], w["w2b"][1])
    o_nsa = nsa_attention(ya3, yb3, kc, vc, tabs["bias_c"], tabs["bias_w"], tabs["overlap_t"])

    q_mla, k_mla, v_mla = mla_up(yb, w, tabs, s)
    wide = MLA_HEADS * LANES
    o_mla = causal_attention(q_mla.reshape(b, s, wide), 0, k_mla.reshape(b, s, wide), 0,
                             v_mla.reshape(b, s, MLA_W), 0, MLA_HEADS, LANES)

    f_t = yb3[:, :, _OFF_FF:_OFF_FF + FOX_HEADS].transpose(0, 2, 1).reshape(b * FOX_HEADS, s)
    b_col = jnp.tile(w["b_f"], b).reshape(b * FOX_HEADS, 1)
    ck = fox_cum(f_t, b_col).reshape(b, FOX_HEADS, s)
    o_fox = causal_attention(ya3, _COL_FQ, ya3, _COL_FQ + 1, ya3, _COL_FQ + 2, FOX_HEADS, DH, ck=ck)

    x2 = mixer_out(x2, o_nsa.reshape(t, NSA_W), o_mla.reshape(t, MLA_W), o_fox.reshape(t, FOX_W),
                   w["wg"], w["wn"], w["wm"], w["wf"], w["wo"], w["ln_g"][0], w["ln_b"][0], alpha)

    m = mem2.shape[0] // b
    kv = matmul(mem2, w["xkv"], MXU_DT, tm=min(512, mem2.shape[0])).reshape(b, m, 2 * XA_W)
    x3 = cross_attention(x2.reshape(b, s, d), kv, w["xq"], w["xo"], w["ln_g"][1], w["ln_b"][1], alpha)

    x2 = mlp(x3.reshape(t, d), w["wu"], w["wd"], w["ln_g"][2], w["ln_b"][2], alpha)
    return x2.reshape(b, s, d)


def _group_stack(a):
    _, n, tq, c = a.shape
    a = a.reshape(NSA_GROUPS, NSA_R, n, tq, c).transpose(0, 2, 1, 3, 4)
    return a.reshape(NSA_GROUPS, n, NSA_R * tq, c)


def _tables(t5_table, s):
    tq = NSA_TQ
    qi = jnp.arange(tq)[:, None]
    ki = jnp.arange(tq)[None, :]
    last = jnp.full((tq, tq), T5_BUCKETS - 1, jnp.int32)
    masked = jnp.full((tq, tq), -1, jnp.int32)
    kinds = [None] * N_TILE_KINDS
    kinds[TILE_DIAG] = jnp.where(qi >= ki, _t5_bucket(qi - ki), -1)
    kinds[TILE_PREV] = _t5_bucket(tq + qi - ki)
    kinds[TILE_FAR] = last
    kinds[TILE_EDGE] = jnp.where(qi < ki, last, masked)
    kinds[TILE_NONE] = masked
    idx_w = jnp.concatenate(kinds, axis=0).astype(jnp.int32)
    bias_w = bias_table(t5_table, idx_w, tq).reshape(NSA_HEADS, N_TILE_KINDS, tq, tq)
    bias_w = _group_stack(bias_w)

    nc = s // D_CMP
    cmp_end = jnp.arange(nc) * D_CMP + (L_CMP - 1)
    dist_c = jnp.arange(s)[:, None] - cmp_end[None, :]
    idx_c = jnp.where(dist_c >= 0, _t5_bucket(dist_c), -1).astype(jnp.int32)
    bias_c = bias_table(t5_table, idx_c, min(512, s)).reshape(NSA_HEADS, s // tq, tq, nc)
    bias_c = _group_stack(bias_c)

    n_slc = s // L_SLC
    c_lo = np.arange(nc)[:, None] * D_CMP
    s_lo = np.arange(n_slc)[None, :] * L_SLC
    overlap = np.maximum(np.minimum(c_lo + L_CMP, s_lo + L_SLC) - np.maximum(c_lo, s_lo), 0) / D_CMP
    overlap[nc - 1, :] = 0.0

    half = MLA_ROPE // 2
    inv = ROPE_THETA ** (-jnp.arange(half, dtype=F32) / half)
    ang = jnp.arange(s).astype(F32)[:, None] * inv[None, :]
    cos, sin = jnp.cos(ang), jnp.sin(ang)
    tail = jnp.zeros((s, LANES - MLA_NOPE - MLA_ROPE), F32)
    seg_c = jnp.concatenate([jnp.ones((s, MLA_NOPE), F32), cos, cos, tail], axis=1)
    seg_s = jnp.concatenate([jnp.zeros((s, MLA_NOPE), F32), -sin, sin, tail], axis=1)
    pa = np.zeros((MLA_ROPE, MLA_HEADS * LANES), np.float32)
    pb = np.zeros((MLA_ROPE, MLA_HEADS * LANES), np.float32)
    for h in range(MLA_HEADS):
        for j in range(MLA_ROPE):
            pa[j, h * LANES + MLA_NOPE + j] = 1.0
            pb[j, h * LANES + MLA_NOPE + (j + half) % MLA_ROPE] = 1.0

    col_scale = np.ones((1, _NA), np.float32)
    col_scale[0, :NSA_W] = LOG2E
    col_scale[0, _COL_FQ * FOX_W:(_COL_FQ + 1) * FOX_W] = LOG2E
    return dict(bias_w=bias_w, bias_c=bias_c, overlap_t=jnp.asarray(overlap.T, MXU_DT),
                rope_cos=jnp.tile(seg_c, (1, MLA_HEADS)), rope_sin=jnp.tile(seg_s, (1, MLA_HEADS)),
                rope_pa=jnp.asarray(pa, MXU_DT), rope_pb=jnp.asarray(pb, MXU_DT),
                col_scale=jnp.asarray(col_scale))


def kernel(x, mem, w_in, cmp_pe, cmp_w1, cmp_w2, t5_table, mla_q_norm, mla_w_uq, mla_kv_norm, mla_w_ukv, fox_b_f, w_gate, w_br_nsa, w_br_mla, w_br_fox, w_mix_out, xa_w_q, xa_w_kv, xa_w_o, mlp_w_up, mlp_w_down, ln_g, ln_b):
    depth = w_in.shape[0]
    b, s, d = x.shape
    alpha = (2 * depth) ** 0.25
    weights = _prep_layer_weights(w_in, cmp_pe, cmp_w1, cmp_w2, mla_q_norm, mla_w_uq, mla_kv_norm,
                                  mla_w_ukv, fox_b_f, w_gate, w_br_nsa, w_br_mla, w_br_fox, w_mix_out,
                                  xa_w_q, xa_w_kv, xa_w_o, mlp_w_up, mlp_w_down, ln_g, ln_b)
    tabs = _tables(t5_table, s)
    mem2 = mem.reshape(-1, d)

    def step(xc, w):
        return _layer(xc, mem2, w, tabs, alpha), None

    out, _ = lax.scan(step, x, weights)
    return out
```

```python
import functools
import math

import numpy as np
import jax
import jax.numpy as jnp
from jax import lax
from jax.experimental import pallas as pl
from jax.experimental.pallas import tpu as pltpu

D_MODEL = 1024
DH = 64
NSA_HEADS = 8
NSA_GROUPS = 2
NSA_R = NSA_HEADS // NSA_GROUPS
L_CMP = 32
D_CMP = 16
CMP_HIDDEN = 128
L_SLC = 64
N_SEL = 8
WINDOW = 512
MLA_HEADS = 4
MLA_NOPE = 64
MLA_ROPE = 32
MLA_V = 64
MLA_Q_LORA = 384
MLA_KV_LORA = 128
ROPE_THETA = 10000.0
FOX_HEADS = 4
XA_HEADS = 4
D_FF = 4 * D_MODEL
T5_BUCKETS = 32
T5_MAX_DIST = 128
NSA_W = NSA_HEADS * DH
NSA_KV = NSA_GROUPS * DH
MLA_W = MLA_HEADS * MLA_V
FOX_W = FOX_HEADS * DH
XA_W = XA_HEADS * DH
LN_EPS = 1e-5
RMS_EPS = 1e-6
NEG = -1e30
FORCE = 1e4
LOG2E = math.log2(math.e)

F32 = jnp.float32
MXU_DT = jnp.bfloat16
LANES = 128
NSA_TQ = 128
NSA_RQ = NSA_R * NSA_TQ
SEL_TK = 512
ATT_T = 512
VMEM_LIMIT = 48 * 1024 * 1024

TILE_DIAG = 0
TILE_PREV = 1
TILE_FAR = 2
TILE_EDGE = 3
TILE_NONE = 4
TILE_SEL_DIAG = 5
TILE_SEL_PREV = 6
TILE_ZERO = 7
N_TILE_KINDS = 8

_NA = NSA_W + 4 * NSA_KV + 3 * FOX_W
_COL_KS = NSA_W // NSA_KV
_COL_FQ = (NSA_W + 4 * NSA_KV) // FOX_W
_OFF_CQ = 0
_OFF_CKV = MLA_Q_LORA
_OFF_KR = _OFF_CKV + MLA_KV_LORA
_OFF_NG = _OFF_KR + MLA_ROPE
_OFF_FF = _OFF_NG + 3 * NSA_HEADS
_NB = 640
_COL_MISC = _OFF_KR // LANES


def _dot(a, b):
    return jnp.dot(a, b, preferred_element_type=F32)


def _dot_nt(a, b):
    return lax.dot_general(a, b, (((1,), (1,)), ((), ())), preferred_element_type=F32)


def _params(*sem):
    return pltpu.CompilerParams(dimension_semantics=sem, vmem_limit_bytes=VMEM_LIMIT)


def _layer_norm(z, g, b):
    mu = jnp.mean(z, axis=-1, keepdims=True)
    zc = z - mu
    var = jnp.mean(zc * zc, axis=-1, keepdims=True)
    return zc * lax.rsqrt(var + LN_EPS) * g + b


def _rms_norm(z, g):
    return z * lax.rsqrt(jnp.mean(z * z, axis=-1, keepdims=True) + RMS_EPS) * g


def _t5_bucket(dist):
    n = jnp.maximum(dist, 0)
    max_exact = T5_BUCKETS // 2
    nf = jnp.maximum(n, 1).astype(F32)
    large = max_exact + (jnp.log(nf / max_exact) / math.log(T5_MAX_DIST / max_exact)
                         * (T5_BUCKETS - max_exact)).astype(jnp.int32)
    large = jnp.minimum(large, T5_BUCKETS - 1)
    return jnp.where(n < max_exact, n, large)


def _with_ones(v):
    return jnp.concatenate([v, jnp.ones(v.shape, v.dtype)], axis=1)


def _softmax_update(m, acc, s, v_aug):
    m_new = jnp.maximum(m, jnp.max(s, axis=1, keepdims=True))
    a = jnp.exp2(m - m_new)
    p = jnp.exp2(s - m_new)
    return m_new, a * acc + _dot(p.astype(MXU_DT), v_aug)


def _softmax_once(s, v_aug):
    p = jnp.exp2(s - jnp.max(s, axis=1, keepdims=True))
    return _dot(p.astype(MXU_DT), v_aug)


def _normalise(acc):
    return acc[:, :DH] * (1.0 / acc[:, DH:])


def _bias_table_kernel(t5_ref, idx_ref, rel_ref, o_ref):
    h = pl.program_id(0)
    idx = idx_ref[...]
    acc = jnp.full(idx.shape, NEG, F32)
    for b in range(T5_BUCKETS):
        acc = jnp.where(idx == b, t5_ref[b, h] * LOG2E, acc)
    o_ref[0] = acc - jnp.where(rel_ref[...] > 0, t5_ref[T5_BUCKETS - 1, h] * LOG2E, 0.0)


def bias_table(t5_table, idx, rel, tr):
    rows, cols = idx.shape
    return pl.pallas_call(
        _bias_table_kernel,
        out_shape=jax.ShapeDtypeStruct((NSA_HEADS, rows, cols), F32),
        grid=(NSA_HEADS, rows // tr),
        in_specs=[pl.BlockSpec(memory_space=pltpu.SMEM),
                  pl.BlockSpec((tr, cols), lambda h, i: (i, 0)),
                  pl.BlockSpec((tr, cols), lambda h, i: (i, 0))],
        out_specs=pl.BlockSpec((1, tr, cols), lambda h, i: (h, i, 0)),
        compiler_params=_params("parallel", "parallel"),
        name="bias_table",
    )(t5_table, idx, rel)


def _in_proj_kernel(x_ref, wa_ref, wb_ref, wc_ref, sc_ref, ya_ref, yb_ref, ykc_ref, yvc_ref):
    xb = x_ref[...].astype(MXU_DT)
    ya_ref[...] = (_dot(xb, wa_ref[...]) * sc_ref[...]).astype(ya_ref.dtype)
    yb_ref[...] = _dot(xb, wb_ref[...])
    yc = _dot(xb, wc_ref[...])
    ykc_ref[...] = yc[:, :NSA_KV]
    yvc_ref[...] = yc[:, NSA_KV:]


def in_proj(x2, wa, wb, wc, col_scale, tm=512):
    t, d = x2.shape
    row = lambda i: (i, 0)
    fixed = lambda i: (0, 0)
    return pl.pallas_call(
        _in_proj_kernel,
        out_shape=(jax.ShapeDtypeStruct((t, _NA), MXU_DT), jax.ShapeDtypeStruct((t, _NB), F32),
                   jax.ShapeDtypeStruct((t, NSA_KV), F32), jax.ShapeDtypeStruct((t, NSA_KV), F32)),
        grid=(t // tm,),
        in_specs=[pl.BlockSpec((tm, d), row), pl.BlockSpec(wa.shape, fixed),
                  pl.BlockSpec(wb.shape, fixed), pl.BlockSpec(wc.shape, fixed),
                  pl.BlockSpec((1, _NA), fixed)],
        out_specs=(pl.BlockSpec((tm, _NA), row), pl.BlockSpec((tm, _NB), row),
                   pl.BlockSpec((tm, NSA_KV), row), pl.BlockSpec((tm, NSA_KV), row)),
        compiler_params=_params("parallel"),
        name="in_proj",
    )(x2, wa, wb, wc, col_scale)


def _mla_up_kernel(yb_ref, qn_ref, wqa_ref, wqb_ref, kvn_ref, wk_ref, wv_ref, pa_ref, pb_ref,
                   ct_ref, st_ref, q_ref, k_ref, v_ref):
    scale = (MLA_NOPE + MLA_ROPE) ** -0.5 * LOG2E
    ct = ct_ref[...]
    st = st_ref[...]
    cq = _rms_norm(yb_ref[:, _OFF_CQ:_OFF_CQ + MLA_Q_LORA], qn_ref[...]).astype(MXU_DT)
    q = _dot(cq, wqa_ref[...]) * ct + _dot(cq, wqb_ref[...]) * st
    q_ref[...] = (q * scale).astype(q_ref.dtype)
    ckv = _rms_norm(yb_ref[:, _OFF_CKV:_OFF_CKV + MLA_KV_LORA], kvn_ref[...]).astype(MXU_DT)
    v_ref[...] = _dot(ckv, wv_ref[...]).astype(v_ref.dtype)
    kr = yb_ref[:, _OFF_KR:_OFF_KR + MLA_ROPE]
    kr_hi = kr.astype(MXU_DT)
    kr_lo = (kr - kr_hi.astype(F32)).astype(MXU_DT)
    ka = _dot(kr_hi, pa_ref[...]) + _dot(kr_lo, pa_ref[...])
    kb = _dot(kr_hi, pb_ref[...]) + _dot(kr_lo, pb_ref[...])
    k_ref[...] = (_dot(ckv, wk_ref[...]) + ka * ct + kb * st).astype(k_ref.dtype)


def mla_up(yb, w, tabs, seq, tm=512):
    t = yb.shape[0]
    nps = seq // tm
    row = lambda i: (i, 0)
    fixed = lambda i: (0, 0)
    pos = lambda i: (i % nps, 0)
    wide = MLA_HEADS * LANES
    ins = [w["q_norm"], w["wqa"], w["wqb"], w["kv_norm"], w["wk"], w["wv"], tabs["rope_pa"], tabs["rope_pb"]]
    return pl.pallas_call(
        _mla_up_kernel,
        out_shape=(jax.ShapeDtypeStruct((t, wide), MXU_DT), jax.ShapeDtypeStruct((t, wide), MXU_DT),
                   jax.ShapeDtypeStruct((t, MLA_W), MXU_DT)),
        grid=(t // tm,),
        in_specs=[pl.BlockSpec((tm, _NB), row)] + [pl.BlockSpec(a.shape, fixed) for a in ins]
        + [pl.BlockSpec((tm, wide), pos), pl.BlockSpec((tm, wide), pos)],
        out_specs=(pl.BlockSpec((tm, wide), row), pl.BlockSpec((tm, wide), row),
                   pl.BlockSpec((tm, MLA_W), row)),
        compiler_params=_params("parallel"),
        name="mla_up",
    )(yb, *ins, tabs["rope_cos"], tabs["rope_sin"])


def _bf16_head(x):
    bits = lax.bitcast_convert_type(x, jnp.uint32) & jnp.uint32(0xFFFF0000)
    return lax.bitcast_convert_type(bits, F32)


def _fox_cum_kernel(f_ref, b_ref, hi_ref, mid_ref, lo_ref):
    z = f_ref[...] + b_ref[...]
    x = jnp.minimum(z, 0.0) - jnp.log1p(jnp.exp(-jnp.abs(z)))
    n = x.shape[1]
    lane = lax.broadcasted_iota(jnp.int32, x.shape, 1)
    shift = 1
    while shift < n:
        x = x + jnp.where(lane >= shift, pltpu.roll(x, shift, 1), 0.0)
        shift *= 2
    dec = -(x * LOG2E)
    hi = _bf16_head(dec)
    rest = dec - hi
    mid = _bf16_head(rest)
    hi_ref[...] = hi.astype(hi_ref.dtype)
    mid_ref[...] = mid.astype(mid_ref.dtype)
    lo_ref[...] = (rest - mid).astype(lo_ref.dtype)


def fox_cum(f_t, b_col):
    out = jax.ShapeDtypeStruct(f_t.shape, MXU_DT)
    return pl.pallas_call(
        _fox_cum_kernel,
        out_shape=(out, out, out),
        compiler_params=pltpu.CompilerParams(vmem_limit_bytes=VMEM_LIMIT),
        name="fox_cum",
    )(f_t, b_col)


def _keys_softmax_update(m, acc, s, v_aug):
    m_new = jnp.maximum(m, jnp.max(s, axis=0, keepdims=True))
    a = jnp.exp2(m - m_new)
    p = jnp.exp2(s - m_new).astype(MXU_DT)
    pv = lax.dot_general(v_aug, p, (((0,), (0,)), ((), ())), preferred_element_type=F32)
    return m_new, a * acc + pv


def _keys_softmax_init(queries):
    return jnp.full((1, queries), NEG, F32), jnp.zeros((2 * DH, queries), F32)


def _causal_attn_kernel(q_ref, k_ref, v_ref, o_ref, *, n_heads, dq, ones_q):
    qi = pl.program_id(1)
    t = q_ref.shape[1]
    key = lax.broadcasted_iota(jnp.int32, (t, t), 0)
    qry = lax.broadcasted_iota(jnp.int32, (t, t), 1)

    qs = []
    for h in range(n_heads):
        q = q_ref[0, :, h * dq:(h + 1) * dq]
        qs.append(_with_ones(q) if ones_q else q)

    def scores(kt, h):
        rows = pl.ds(pl.multiple_of(kt * t, t), t)
        return _dot_nt(k_ref[0, rows, h * LANES:(h + 1) * LANES], qs[h])

    def values(kt, h):
        rows = pl.ds(pl.multiple_of(kt * t, t), t)
        return _with_ones(v_ref[0, rows, h * DH:(h + 1) * DH])

    def sweep(kt, state, mask):
        ahead = 4
        s = {h: scores(kt, h) for h in range(min(ahead, n_heads))}
        out = []
        for h in range(n_heads):
            sh = s.pop(h)
            if mask is not None:
                sh = jnp.where(mask, sh, NEG)
            out.append(_keys_softmax_update(*state[h], sh, values(kt, h)))
            if h + ahead < n_heads:
                s[h + ahead] = scores(kt, h + ahead)
        return tuple(out)

    state = lax.fori_loop(0, qi, lambda kt, st: sweep(kt, st, None),
                          tuple(_keys_softmax_init(t) for _ in range(n_heads)))
    state = sweep(qi, state, key <= qry)
    for h in range(n_heads):
        o_ref[0, :, h * DH:(h + 1) * DH] = _normalise(state[h][1].T).astype(o_ref.dtype)


def causal_attention(q, q_col, k, v, v_col, n_heads, dq, ones_q, t=ATT_T):
    b, s, _ = q.shape
    return pl.pallas_call(
        functools.partial(_causal_attn_kernel, n_heads=n_heads, dq=dq, ones_q=ones_q),
        out_shape=jax.ShapeDtypeStruct((b, s, n_heads * DH), MXU_DT),
        grid=(b, s // t),
        in_specs=[pl.BlockSpec((1, t, n_heads * dq), lambda bi, qi: (bi, qi, q_col)),
                  pl.BlockSpec((1, s, n_heads * LANES), lambda bi, qi: (bi, 0, 0)),
                  pl.BlockSpec((1, s, n_heads * DH), lambda bi, qi: (bi, 0, v_col))],
        out_specs=pl.BlockSpec((1, t, n_heads * DH), lambda bi, qi: (bi, qi, 0)),
        compiler_params=_params("parallel", "arbitrary"),
        name="causal_attn_decay" if ones_q else "causal_attn",
    )(q, k, v)


def _compress_kernel(x_ref, pe_ref, wlo_ref, whi_ref, w2_ref, o_ref):
    x = x_ref[0]
    nc = x.shape[0]
    lo = _dot((x + pe_ref[0:1, :]).astype(MXU_DT), wlo_ref[...])
    hi = _dot((x + pe_ref[1:2, :]).astype(MXU_DT), whi_ref[...])
    hid = lo + pltpu.roll(hi, nc - 1, 0)
    act = jax.nn.gelu(hid)
    o_ref[0] = _dot(act.astype(MXU_DT), w2_ref[...]).astype(o_ref.dtype)


def compress(x, pe, wlo, whi, w2):
    b, nc, width = x.shape
    fixed = lambda bi: (0, 0)
    return pl.pallas_call(
        _compress_kernel,
        out_shape=jax.ShapeDtypeStruct((b, nc, NSA_KV), MXU_DT),
        grid=(b,),
        in_specs=[pl.BlockSpec((1, nc, width), lambda bi: (bi, 0, 0)),
                  pl.BlockSpec(pe.shape, fixed), pl.BlockSpec(wlo.shape, fixed),
                  pl.BlockSpec(whi.shape, fixed), pl.BlockSpec(w2.shape, fixed)],
        out_specs=pl.BlockSpec((1, nc, NSA_KV), lambda bi: (bi, 0, 0)),
        compiler_params=_params("parallel"),
        name="nsa_compress",
    )(x, pe, wlo, whi, w2)


def _nsa_kernel(q_ref, kc_ref, vc_ref, ks_ref, vs_ref, kw_ref, vw_ref, misc_ref,
                bc_ref, bw_ref, ovt_ref, o_ref, *, n_sel):
    i = pl.program_id(1)
    tq, rq = NSA_TQ, NSA_RQ
    nslc = ovt_ref.shape[0]
    seg = DH + nslc
    nwt = WINDOW // tq + 1
    sub = SEL_TK // tq
    t0 = i * tq
    g_off = _OFF_NG - _COL_MISC * LANES
    gates_t = jax.nn.sigmoid(misc_ref[0]).T

    lane = lax.broadcasted_iota(jnp.int32, (1, rq), 1)
    has_cmp = (t0 + (lane & (tq - 1))) >= (L_CMP - 1)

    blk = lax.broadcasted_iota(jnp.int32, (nslc, tq), 0)
    blk_f = blk.astype(F32)
    tpos = lax.broadcasted_iota(jnp.int32, (nslc, tq), 1) + t0
    cur = lax.shift_right_logical(tpos, int(math.log2(L_SLC)))
    forced = (blk == 0) | (blk == cur) | (blk == cur - 1)
    causal_blk = blk * L_SLC <= tpos
    eye = (lax.broadcasted_iota(jnp.int32, (tq, tq), 0)
           == lax.broadcasted_iota(jnp.int32, (tq, tq), 1)).astype(MXU_DT)

    def window_tile_kind(d):
        static = {0: TILE_DIAG, 1: TILE_PREV, nwt - 1: TILE_EDGE}.get(d, TILE_FAR)
        return jnp.where(i - d < 0, TILE_NONE, static)

    def group_cols(g):
        return slice(g * DH, (g + 1) * DH)

    groups = range(NSA_GROUPS)
    q_groups = [jnp.concatenate([q_ref[0, :, (g * NSA_R + r) * DH:(g * NSA_R + r + 1) * DH]
                                 for r in range(NSA_R)], axis=0) for g in groups]

    s_cmp = [_dot_nt(kc_ref[0, :, group_cols(g)], q_groups[g]) + bc_ref[g, 0] for g in groups]
    win_rows = [pl.ds(pl.multiple_of(jnp.maximum(i - d, 0) * tq, tq), tq) for d in range(nwt - 1, -1, -1)]
    s_win = []
    for g in groups:
        kcat = jnp.concatenate([kw_ref[0, r, group_cols(g)] for r in win_rows], axis=0)
        bias = jnp.concatenate([bw_ref[g, window_tile_kind(d)] for d in range(nwt - 1, -1, -1)], axis=0)
        s_win.append(_dot_nt(kcat, q_groups[g]) + bias)

    def gate_row(g, j):
        return jnp.concatenate(
            [gates_t[g_off + 3 * (g * NSA_R + r) + j:g_off + 3 * (g * NSA_R + r) + j + 1] for r in range(NSA_R)],
            axis=1)

    def weighted_values(v, p):
        return lax.dot_general(_with_ones(v), p, (((0,), (0,)), ((), ())), preferred_element_type=F32)

    o_cmp, imp = [], []
    for g in groups:
        e = jnp.exp2(s_cmp[g] - jnp.max(s_cmp[g], axis=0, keepdims=True)).astype(MXU_DT)
        acc = weighted_values(vc_ref[0, :, group_cols(g)], e)
        inv = jnp.where(has_cmp, 1.0 / acc[DH:DH + 1], 0.0)
        o_cmp.append(acc[:DH] * inv)
        imp4 = _dot(ovt_ref[...], e) * inv
        imp.append(sum(imp4[:, r * tq:(r + 1) * tq] for r in range(1, NSA_R)) + imp4[:, 0:tq])

    o_fixed = []
    for g in groups:
        vcat = jnp.concatenate([vw_ref[0, r, group_cols(g)] for r in win_rows], axis=0)
        p = jnp.exp2(s_win[g] - jnp.max(s_win[g], axis=0, keepdims=True)).astype(MXU_DT)
        acc = weighted_values(vcat, p)
        o_win = acc[:DH] * (1.0 / acc[DH:DH + 1])
        o_fixed.append(gate_row(g, 0) * o_cmp[g] + gate_row(g, 2) * o_win)

    q_aug = []
    for g in groups:
        x = jnp.where(causal_blk, imp[g] + jnp.where(forced, FORCE, 0.0), NEG)
        sel_t = jnp.zeros((nslc, tq), F32)
        for _ in range(n_sel):
            mx = jnp.max(x, axis=0, keepdims=True)
            first = jnp.min(jnp.where(x == mx, blk_f, float(nslc)), axis=0, keepdims=True)
            hit = blk_f == first
            sel_t = jnp.where(hit, 1.0, sel_t)
            x = jnp.where(hit, -3e38, x)
        penalty = ((1.0 - _dot_nt(eye, sel_t.astype(MXU_DT))) * NEG).astype(MXU_DT)
        q_aug.append(jnp.concatenate([q_groups[g], jnp.concatenate([penalty] * NSA_R, axis=0)], axis=1))

    def sel_scores(kt, g):
        rows = pl.ds(pl.multiple_of(kt * SEL_TK, SEL_TK), SEL_TK)
        return _dot_nt(ks_ref[0, rows, g * seg:(g + 1) * seg], q_aug[g])

    def sel_values(kt, g):
        rows = pl.ds(pl.multiple_of(kt * SEL_TK, SEL_TK), SEL_TK)
        return _with_ones(vs_ref[0, rows, group_cols(g)])

    def far_body(kt, state):
        s = [sel_scores(kt, g) for g in groups]
        return tuple(_keys_softmax_update(*state[g], s[g], sel_values(kt, g)) for g in groups)

    last = i // sub
    state = lax.fori_loop(0, jnp.maximum(last - 1, 0), far_body,
                          tuple(_keys_softmax_init(rq) for _ in groups))
    near = []
    for kn in (last - 1, last):
        kt = jnp.maximum(kn, 0)
        kinds = []
        for j in range(sub):
            d = i - (kn * sub + j)
            kind = jnp.where(d < 0, TILE_NONE, jnp.where(d == 0, TILE_SEL_DIAG,
                                                         jnp.where(d == 1, TILE_SEL_PREV, TILE_ZERO)))
            kinds.append(jnp.where(kn < 0, TILE_NONE, kind))
        for g in groups:
            bias = jnp.concatenate([bw_ref[g, kd] for kd in kinds], axis=0)
            near.append((kt, g, sel_scores(kt, g) + bias))
    state = list(state)
    for kt, g, s in near:
        state[g] = _keys_softmax_update(*state[g], s, sel_values(kt, g))

    out_t = []
    for g in groups:
        acc = state[g][1]
        out_t.append(o_fixed[g] + gate_row(g, 1) * (acc[:DH] * (1.0 / acc[DH:DH + 1])))
    out_t = jnp.concatenate(out_t, axis=0)
    for r in range(NSA_R):
        slab = out_t[:, r * tq:(r + 1) * tq].T
        for g in groups:
            h = g * NSA_R + r
            o_ref[0, :, h * DH:(h + 1) * DH] = slab[:, g * DH:(g + 1) * DH].astype(o_ref.dtype)


def nsa_attention(ya, yb, kc, vc, ks_aug, bias_c, bias_w, overlap_t):
    b, s, _ = ya.shape
    tq = NSA_TQ
    ncp = kc.shape[1]
    per_b_kv = lambda col: pl.BlockSpec((1, s, NSA_KV), lambda bi, i: (bi, 0, col))
    per_b_cmp = pl.BlockSpec((1, ncp, NSA_KV), lambda bi, i: (bi, 0, 0))
    return pl.pallas_call(
        functools.partial(_nsa_kernel, n_sel=min(N_SEL, overlap_t.shape[0])),
        out_shape=jax.ShapeDtypeStruct((b, s, NSA_W), MXU_DT),
        grid=(b, s // tq),
        in_specs=[pl.BlockSpec((1, tq, NSA_W), lambda bi, i: (bi, i, 0)),
                  per_b_cmp, per_b_cmp,
                  pl.BlockSpec((1, s, ks_aug.shape[2]), lambda bi, i: (bi, 0, 0)),
                  per_b_kv(_COL_KS + 1), per_b_kv(_COL_KS + 2), per_b_kv(_COL_KS + 3),
                  pl.BlockSpec((1, tq, LANES), lambda bi, i: (bi, i, _COL_MISC)),
                  pl.BlockSpec((NSA_GROUPS, 1, ncp, NSA_RQ), lambda bi, i: (0, i, 0, 0)),
                  pl.BlockSpec(bias_w.shape, lambda bi, i: (0, 0, 0, 0)),
                  pl.BlockSpec(overlap_t.shape, lambda bi, i: (0, 0))],
        out_specs=pl.BlockSpec((1, tq, NSA_W), lambda bi, i: (bi, i, 0)),
        compiler_params=_params("parallel", "arbitrary"),
        name="nsa_attn",
    )(ya, kc, vc, ks_aug, ya, ya, ya, yb, bias_c, bias_w, overlap_t)


def _mix_kernel(x_ref, on_ref, om_ref, of_ref, wg_ref, wn_ref, wm_ref, wf_ref, wo_ref,
                g_ref, b_ref, o_ref, *, alpha):
    x = x_ref[...]
    xb = x.astype(MXU_DT)
    d = x.shape[1]
    merged = (jax.nn.sigmoid(_dot(xb, wg_ref[:, 0:d])) * _dot(on_ref[...], wn_ref[...])
              + jax.nn.sigmoid(_dot(xb, wg_ref[:, d:2 * d])) * _dot(om_ref[...], wm_ref[...])
              + jax.nn.sigmoid(_dot(xb, wg_ref[:, 2 * d:3 * d])) * _dot(of_ref[...], wf_ref[...]))
    y = _dot(merged.astype(MXU_DT), wo_ref[...])
    o_ref[...] = _layer_norm(alpha * x + y, g_ref[...], b_ref[...])


def mixer_out(x2, o_nsa, o_mla, o_fox, wg, wn, wm, wf, wo, ln_g, ln_b, alpha, tm=256):
    t, d = x2.shape
    row = lambda i: (i, 0)
    fixed = lambda i: (0, 0)
    return pl.pallas_call(
        functools.partial(_mix_kernel, alpha=alpha),
        out_shape=jax.ShapeDtypeStruct((t, d), F32),
        grid=(t // tm,),
        in_specs=[pl.BlockSpec((tm, d), row),
                  pl.BlockSpec((tm, NSA_W), row),
                  pl.BlockSpec((tm, MLA_W), row),
                  pl.BlockSpec((tm, FOX_W), row),
                  pl.BlockSpec(wg.shape, fixed), pl.BlockSpec(wn.shape, fixed),
                  pl.BlockSpec(wm.shape, fixed), pl.BlockSpec(wf.shape, fixed),
                  pl.BlockSpec(wo.shape, fixed),
                  pl.BlockSpec((1, d), fixed), pl.BlockSpec((1, d), fixed)],
        out_specs=pl.BlockSpec((tm, d), row),
        compiler_params=_params("parallel"),
        name="mixer_out",
    )(x2, o_nsa, o_mla, o_fox, wg, wn, wm, wf, wo, ln_g, ln_b)


def _matmul_kernel(x_ref, w_ref, o_ref):
    o_ref[...] = _dot(x_ref[...].astype(MXU_DT), w_ref[...]).astype(o_ref.dtype)


def matmul(x2, w, out_dtype, tm):
    t, d = x2.shape
    n = w.shape[1]
    return pl.pallas_call(
        _matmul_kernel,
        out_shape=jax.ShapeDtypeStruct((t, n), out_dtype),
        grid=(t // tm,),
        in_specs=[pl.BlockSpec((tm, d), lambda i: (i, 0)), pl.BlockSpec((d, n), lambda i: (0, 0))],
        out_specs=pl.BlockSpec((tm, n), lambda i: (i, 0)),
        compiler_params=_params("parallel"),
        name="mem_kv_proj",
    )(x2, w)


def _xattn_kernel(x_ref, wq_ref, kv_ref, wo_ref, g_ref, b_ref, o_ref, *, alpha):
    x = x_ref[0]
    q = (_dot(x.astype(MXU_DT), wq_ref[...]) * LOG2E).astype(MXU_DT)
    outs = []
    for h in range(XA_HEADS):
        s = _dot_nt(q[:, h * DH:(h + 1) * DH], kv_ref[0, :, h * DH:(h + 1) * DH])
        v = kv_ref[0, :, XA_W + h * DH:XA_W + (h + 1) * DH]
        outs.append(_normalise(_softmax_once(s, _with_ones(v))))
    o = jnp.concatenate(outs, axis=1).astype(MXU_DT)
    y = _dot(o, wo_ref[...])
    o_ref[0] = _layer_norm(alpha * x + y, g_ref[...], b_ref[...])


def cross_attention(x, kv, wq, wo, ln_g, ln_b, alpha, tm=256):
    b, s, d = x.shape
    m = kv.shape[1]
    fixed = lambda bi, i: (0, 0)
    return pl.pallas_call(
        functools.partial(_xattn_kernel, alpha=alpha),
        out_shape=jax.ShapeDtypeStruct((b, s, d), F32),
        grid=(b, s // tm),
        in_specs=[pl.BlockSpec((1, tm, d), lambda bi, i: (bi, i, 0)),
                  pl.BlockSpec(wq.shape, fixed),
                  pl.BlockSpec((1, m, 2 * XA_W), lambda bi, i: (bi, 0, 0)),
                  pl.BlockSpec(wo.shape, fixed),
                  pl.BlockSpec((1, d), fixed), pl.BlockSpec((1, d), fixed)],
        out_specs=pl.BlockSpec((1, tm, d), lambda bi, i: (bi, i, 0)),
        compiler_params=_params("parallel", "parallel"),
        name="cross_attn",
    )(x, wq, kv, wo, ln_g, ln_b)


def _mlp_kernel(x_ref, wu_ref, wd_ref, g_ref, b_ref, o_ref, acc_ref, *, alpha):
    j = pl.program_id(1)
    x = x_ref[...]
    hdn = jnp.square(jnp.maximum(_dot(x.astype(MXU_DT), wu_ref[...]), 0.0))
    part = _dot(hdn.astype(MXU_DT), wd_ref[...])

    @pl.when(j == 0)
    def _():
        acc_ref[...] = part

    @pl.when(j > 0)
    def _():
        acc_ref[...] += part

    @pl.when(j == pl.num_programs(1) - 1)
    def _():
        o_ref[...] = _layer_norm(alpha * x + acc_ref[...], g_ref[...], b_ref[...])


def mlp(x2, wu, wd, ln_g, ln_b, alpha, tm=512, tf=1024):
    t, d = x2.shape
    f = wu.shape[1]
    return pl.pallas_call(
        functools.partial(_mlp_kernel, alpha=alpha),
        out_shape=jax.ShapeDtypeStruct((t, d), F32),
        grid=(t // tm, f // tf),
        in_specs=[pl.BlockSpec((tm, d), lambda i, j: (i, 0)),
                  pl.BlockSpec((d, tf), lambda i, j: (0, j)),
                  pl.BlockSpec((tf, d), lambda i, j: (j, 0)),
                  pl.BlockSpec((1, d), lambda i, j: (0, 0)),
                  pl.BlockSpec((1, d), lambda i, j: (0, 0))],
        out_specs=pl.BlockSpec((tm, d), lambda i, j: (i, 0)),
        scratch_shapes=[pltpu.VMEM((tm, d), F32)],
        compiler_params=_params("parallel", "arbitrary"),
        name="mlp",
    )(x2, wu, wd, ln_g, ln_b)


def _segment_cols(w, cols_per_head, head_offset):
    out = jnp.zeros(w.shape[:-1] + (len(cols_per_head) * LANES,), w.dtype)
    for h, cols in enumerate(cols_per_head):
        start = h * LANES + head_offset
        out = out.at[..., start:start + len(cols)].set(w[..., np.asarray(cols)])
    return out


def _prep_layer_weights(w_in, cmp_pe, cmp_w1, cmp_w2, mla_q_norm, mla_w_uq, mla_kv_norm, mla_w_ukv,
                        fox_b_f, w_gate, w_br_nsa, w_br_mla, w_br_fox, w_mix_out, xa_w_q, xa_w_kv,
                        xa_w_o, mlp_w_up, mlp_w_down, ln_g, ln_b):
    depth, d, _ = w_in.shape
    sizes = (NSA_W, NSA_KV, NSA_KV, NSA_KV, NSA_KV, NSA_KV, NSA_KV, 3 * NSA_HEADS,
             MLA_Q_LORA, MLA_KV_LORA, MLA_ROPE, FOX_W, FOX_W, FOX_W, FOX_HEADS)
    offs = np.concatenate([[0], np.cumsum(sizes)])
    col = lambda n: w_in[:, :, offs[n]:offs[n + 1]]
    (nq, nkc, nvc, nks, nvs, nkw, nvw, ngate, cq, ckv, kr, fq, fk, fv, ff) = [col(n) for n in range(15)]
    qk_scale = DH ** -0.5
    wa = jnp.concatenate([nq * qk_scale, nks, nvs, nkw, nvw, fq * qk_scale, fk, fv], axis=2)
    pad = jnp.zeros((depth, d, _NB - (_OFF_FF + FOX_HEADS)), F32)
    wb = jnp.concatenate([cq, ckv, kr, ngate, ff, pad], axis=2)
    wc = jnp.concatenate([nkc, nvc], axis=2)

    dq = MLA_NOPE + MLA_ROPE
    half = MLA_ROPE // 2
    heads = range(MLA_HEADS)
    nope = [np.arange(h * dq, h * dq + MLA_NOPE) for h in heads]
    x1 = [np.arange(h * dq + MLA_NOPE, h * dq + MLA_NOPE + half) for h in heads]
    x2 = [c + half for c in x1]
    wqa = (_segment_cols(mla_w_uq, nope, 0) + _segment_cols(mla_w_uq, x1, MLA_NOPE)
           + _segment_cols(mla_w_uq, x2, MLA_NOPE + half))
    wqb = _segment_cols(mla_w_uq, x2, MLA_NOPE) + _segment_cols(mla_w_uq, x1, MLA_NOPE + half)
    dkv = MLA_NOPE + MLA_V
    wk = _segment_cols(mla_w_ukv, [np.arange(h * dkv, h * dkv + MLA_NOPE) for h in heads], 0)
    wv = mla_w_ukv[:, :, np.concatenate([np.arange(h * dkv + MLA_NOPE, (h + 1) * dkv) for h in heads])]

    eye_g = jnp.eye(NSA_GROUPS, dtype=F32)
    w1r = cmp_w1.reshape(depth, 2, L_CMP, DH, CMP_HIDDEN)
    def chunk_weights(w1_half):
        blk = jnp.einsum('ealdj,gh->ealgdhj', w1_half, eye_g)
        return blk.reshape(depth, 2, D_CMP * NSA_KV, NSA_GROUPS * CMP_HIDDEN)
    w1lo = chunk_weights(w1r[:, :, :D_CMP])
    w1hi = chunk_weights(w1r[:, :, D_CMP:])
    w2b = jnp.einsum('eajd,gh->eagjhd', cmp_w2, eye_g).reshape(depth, 2, NSA_GROUPS * CMP_HIDDEN, NSA_KV)
    pe = jnp.broadcast_to(cmp_pe.reshape(depth, 2, 2, D_CMP, 1, DH),
                          (depth, 2, 2, D_CMP, NSA_GROUPS, DH)).reshape(depth, 2, 2, D_CMP * NSA_KV)

    c = lambda a: a.astype(MXU_DT)
    return dict(
        wa=c(wa), wb=c(wb), wc=c(wc),
        pe=pe, w1lo=c(w1lo), w1hi=c(w1hi), w2b=c(w2b),
        q_norm=mla_q_norm[:, None, :], wqa=c(wqa), wqb=c(wqb),
        kv_norm=mla_kv_norm[:, None, :], wk=c(wk), wv=c(wv),
        b_f=fox_b_f, wg=c(w_gate), wn=c(w_br_nsa), wm=c(w_br_mla), wf=c(w_br_fox), wo=c(w_mix_out),
        xq=c(xa_w_q * qk_scale), xkv=c(xa_w_kv), xo=c(xa_w_o),
        wu=c(mlp_w_up), wd=c(mlp_w_down),
        ln_g=ln_g[:, :, None, :], ln_b=ln_b[:, :, None, :],
    )


def _layer(x, mem2, w, tabs, alpha):
    b, s, d = x.shape
    t = b * s
    x2 = x.reshape(t, d)

    ya, yb, ykc, yvc = in_proj(x2, w["wa"], w["wb"], w["wc"], tabs["col_scale"])
    ya3 = ya.reshape(b, s, _NA)
    yb3 = yb.reshape(b, s, _NB)

    nc = s // D_CMP
    kc = compress(ykc.reshape(b, nc, D_CMP * NSA_KV), w["pe"][0], w["w1lo"][0], w["w1hi"][0], w["w2b"][0])
    vc = compress(yvc.reshape(b, nc, D_CMP * NSA_KV), w["pe"][1], w["w1lo"][1], w["w1hi"][1], w["w2b"][1])
    ks = ya3[:, :, _COL_KS * NSA_KV:(_COL_KS + 1) * NSA_KV].reshape(b, s, NSA_GROUPS, DH)
    block_id = jnp.broadcast_to(tabs["block_onehot"][None, :, None, :],
                                (b, s, NSA_GROUPS, tabs["block_onehot"].shape[1]))
    ks_aug = jnp.concatenate([ks, block_id], axis=-1).reshape(b, s, -1)
    o_nsa = nsa_attention(ya3, yb3, kc, vc, ks_aug, tabs["bias_c"], tabs["bias_w"], tabs["overlap_t"])

    q_mla, k_mla, v_mla = mla_up(yb, w, tabs, s)
    wide = MLA_HEADS * LANES
    o_mla = causal_attention(q_mla.reshape(b, s, wide), 0, k_mla.reshape(b, s, wide),
                             v_mla.reshape(b, s, MLA_W), 0, MLA_HEADS, LANES, ones_q=False)

    f_t = yb3[:, :, _OFF_FF:_OFF_FF + FOX_HEADS].transpose(0, 2, 1).reshape(b * FOX_HEADS, s)
    b_col = jnp.tile(w["b_f"], b).reshape(b * FOX_HEADS, 1)
    dec = jnp.stack(fox_cum(f_t, b_col), axis=-1).reshape(b, FOX_HEADS, s, 3).transpose(0, 2, 1, 3)
    fk = ya3[:, :, (_COL_FQ + 1) * FOX_W:(_COL_FQ + 2) * FOX_W].reshape(b, s, FOX_HEADS, DH)
    k_fox = jnp.concatenate([fk, dec, jnp.zeros((b, s, FOX_HEADS, LANES - DH - 3), MXU_DT)], axis=-1)
    o_fox = causal_attention(ya3, _COL_FQ, k_fox.reshape(b, s, FOX_HEADS * LANES), ya3, _COL_FQ + 2,
                             FOX_HEADS, DH, ones_q=True)

    x2 = mixer_out(x2, o_nsa.reshape(t, NSA_W), o_mla.reshape(t, MLA_W), o_fox.reshape(t, FOX_W),
                   w["wg"], w["wn"], w["wm"], w["wf"], w["wo"], w["ln_g"][0], w["ln_b"][0], alpha)

    m = mem2.shape[0] // b
    kv = matmul(mem2, w["xkv"], MXU_DT, tm=min(512, mem2.shape[0])).reshape(b, m, 2 * XA_W)
    x3 = cross_attention(x2.reshape(b, s, d), kv, w["xq"], w["xo"], w["ln_g"][1], w["ln_b"][1], alpha)

    x2 = mlp(x3.reshape(t, d), w["wu"], w["wd"], w["ln_g"][2], w["ln_b"][2], alpha)
    return x2.reshape(b, s, d)


def _group_stack_t(a):
    _, n, tq, c = a.shape
    a = a.reshape(NSA_GROUPS, NSA_R, n, tq, c).transpose(0, 2, 4, 1, 3)
    return a.reshape(NSA_GROUPS, n, c, NSA_R * tq)


def _tables(t5_table, s):
    tq = NSA_TQ
    qi = jnp.arange(tq)[:, None]
    ki = jnp.arange(tq)[None, :]
    last = jnp.full((tq, tq), T5_BUCKETS - 1, jnp.int32)
    masked = jnp.full((tq, tq), -1, jnp.int32)
    kinds = [None] * N_TILE_KINDS
    kinds[TILE_DIAG] = jnp.where(qi >= ki, _t5_bucket(qi - ki), -1)
    kinds[TILE_PREV] = _t5_bucket(tq + qi - ki)
    kinds[TILE_FAR] = last
    kinds[TILE_EDGE] = jnp.where(qi < ki, last, masked)
    kinds[TILE_NONE] = masked
    kinds[TILE_SEL_DIAG] = kinds[TILE_DIAG]
    kinds[TILE_SEL_PREV] = kinds[TILE_PREV]
    kinds[TILE_ZERO] = last
    relative = [int(k in (TILE_SEL_DIAG, TILE_SEL_PREV, TILE_ZERO)) for k in range(N_TILE_KINDS)]
    idx_w = jnp.concatenate(kinds, axis=0).astype(jnp.int32)
    rel_w = jnp.concatenate([jnp.full((tq, tq), r, jnp.int32) for r in relative], axis=0)
    bias_w = bias_table(t5_table, idx_w, rel_w, tq).reshape(NSA_HEADS, N_TILE_KINDS, tq, tq)
    bias_w = _group_stack_t(bias_w)

    nc = s // D_CMP
    cmp_end = jnp.arange(nc) * D_CMP + (L_CMP - 1)
    dist_c = jnp.arange(s)[:, None] - cmp_end[None, :]
    idx_c = jnp.where(dist_c >= 0, _t5_bucket(dist_c), -1).astype(jnp.int32)
    bias_c = bias_table(t5_table, idx_c, jnp.zeros_like(idx_c), min(512, s))
    bias_c = _group_stack_t(bias_c.reshape(NSA_HEADS, s // tq, tq, nc))

    n_slc = s // L_SLC
    block_onehot = (np.arange(s)[:, None] // L_SLC == np.arange(n_slc)[None, :]).astype(np.float32)
    c_lo = np.arange(nc)[:, None] * D_CMP
    s_lo = np.arange(n_slc)[None, :] * L_SLC
    overlap = np.maximum(np.minimum(c_lo + L_CMP, s_lo + L_SLC) - np.maximum(c_lo, s_lo), 0) / D_CMP
    overlap[nc - 1, :] = 0.0

    half = MLA_ROPE // 2
    inv = ROPE_THETA ** (-jnp.arange(half, dtype=F32) / half)
    ang = jnp.arange(s).astype(F32)[:, None] * inv[None, :]
    cos, sin = jnp.cos(ang), jnp.sin(ang)
    tail = jnp.zeros((s, LANES - MLA_NOPE - MLA_ROPE), F32)
    seg_c = jnp.concatenate([jnp.ones((s, MLA_NOPE), F32), cos, cos, tail], axis=1)
    seg_s = jnp.concatenate([jnp.zeros((s, MLA_NOPE), F32), -sin, sin, tail], axis=1)
    pa = np.zeros((MLA_ROPE, MLA_HEADS * LANES), np.float32)
    pb = np.zeros((MLA_ROPE, MLA_HEADS * LANES), np.float32)
    for h in range(MLA_HEADS):
        for j in range(MLA_ROPE):
            pa[j, h * LANES + MLA_NOPE + j] = 1.0
            pb[j, h * LANES + MLA_NOPE + (j + half) % MLA_ROPE] = 1.0

    col_scale = np.ones((1, _NA), np.float32)
    col_scale[0, :NSA_W] = LOG2E
    col_scale[0, _COL_FQ * FOX_W:(_COL_FQ + 1) * FOX_W] = LOG2E
    return dict(bias_w=bias_w, bias_c=bias_c, overlap_t=jnp.asarray(overlap.T, MXU_DT),
                block_onehot=jnp.asarray(block_onehot, MXU_DT),
                rope_cos=jnp.tile(seg_c, (1, MLA_HEADS)), rope_sin=jnp.tile(seg_s, (1, MLA_HEADS)),
                rope_pa=jnp.asarray(pa, MXU_DT), rope_pb=jnp.asarray(pb, MXU_DT),
                col_scale=jnp.asarray(col_scale))


def kernel(x, mem, w_in, cmp_pe, cmp_w1, cmp_w2, t5_table, mla_q_norm, mla_w_uq, mla_kv_norm, mla_w_ukv, fox_b_f, w_gate, w_br_nsa, w_br_mla, w_br_fox, w_mix_out, xa_w_q, xa_w_kv, xa_w_o, mlp_w_up, mlp_w_down, ln_g, ln_b):
    depth = w_in.shape[0]
    b, s, d = x.shape
    alpha = (2 * depth) ** 0.25
    weights = _prep_layer_weights(w_in, cmp_pe, cmp_w1, cmp_w2, mla_q_norm, mla_w_uq, mla_kv_norm,
                                  mla_w_ukv, fox_b_f, w_gate, w_br_nsa, w_br_mla, w_br_fox, w_mix_out,
                                  xa_w_q, xa_w_kv, xa_w_o, mlp_w_up, mlp_w_down, ln_g, ln_b)
    tabs = _tables(t5_table, s)
    mem2 = mem.reshape(-1, d)

    def step(xc, w):
        return _layer(xc, mem2, w, tabs, alpha), None

    out, _ = lax.scan(step, x, weights)
    return out
```

```python
import functools
import math

import numpy as np
import jax
import jax.numpy as jnp
from jax import lax
from jax.experimental import pallas as pl
from jax.experimental.pallas import tpu as pltpu

D_MODEL = 1024
DH = 64
NSA_HEADS = 8
NSA_GROUPS = 2
NSA_R = NSA_HEADS // NSA_GROUPS
L_CMP = 32
D_CMP = 16
CMP_HIDDEN = 128
L_SLC = 64
N_SEL = 8
WINDOW = 512
MLA_HEADS = 4
MLA_NOPE = 64
MLA_ROPE = 32
MLA_V = 64
MLA_Q_LORA = 384
MLA_KV_LORA = 128
ROPE_THETA = 10000.0
FOX_HEADS = 4
XA_HEADS = 4
D_FF = 4 * D_MODEL
T5_BUCKETS = 32
T5_MAX_DIST = 128
NSA_W = NSA_HEADS * DH
NSA_KV = NSA_GROUPS * DH
MLA_W = MLA_HEADS * MLA_V
FOX_W = FOX_HEADS * DH
XA_W = XA_HEADS * DH
LN_EPS = 1e-5
RMS_EPS = 1e-6
NEG = -1e30
FORCE = 1e4
LOG2E = math.log2(math.e)

F32 = jnp.float32
MXU_DT = jnp.bfloat16
LANES = 128
NSA_TQ = 128
NSA_RQ = NSA_R * NSA_TQ
SEL_TK = 512
ATT_T = 512
VMEM_LIMIT = 48 * 1024 * 1024

TILE_DIAG = 0
TILE_PREV = 1
TILE_FAR = 2
TILE_EDGE = 3
TILE_NONE = 4
TILE_SEL_DIAG = 5
TILE_SEL_PREV = 6
TILE_ZERO = 7
N_TILE_KINDS = 8

_NA = NSA_W + 2 * FOX_W + 3 * NSA_KV
_COL_FQ = NSA_W // FOX_W
_COL_VS = (NSA_W + 2 * FOX_W) // NSA_KV
_OFF_CQ = 0
_OFF_CKV = MLA_Q_LORA
_OFF_KR = _OFF_CKV + MLA_KV_LORA
_OFF_NG = _OFF_KR + MLA_ROPE
_OFF_FF = _OFF_NG + 3 * NSA_HEADS
_NB = 640
_COL_MISC = _OFF_KR // LANES


def _dot(a, b):
    return jnp.dot(a, b, preferred_element_type=F32)


def _dot_nt(a, b):
    return lax.dot_general(a, b, (((1,), (1,)), ((), ())), preferred_element_type=F32)


def _params(*sem):
    return pltpu.CompilerParams(dimension_semantics=sem, vmem_limit_bytes=VMEM_LIMIT)


def _layer_norm(z, g, b):
    mu = jnp.mean(z, axis=-1, keepdims=True)
    zc = z - mu
    var = jnp.mean(zc * zc, axis=-1, keepdims=True)
    return zc * lax.rsqrt(var + LN_EPS) * g + b


def _rms_norm(z, g):
    return z * lax.rsqrt(jnp.mean(z * z, axis=-1, keepdims=True) + RMS_EPS) * g


def _t5_bucket(dist):
    n = jnp.maximum(dist, 0)
    max_exact = T5_BUCKETS // 2
    nf = jnp.maximum(n, 1).astype(F32)
    large = max_exact + (jnp.log(nf / max_exact) / math.log(T5_MAX_DIST / max_exact)
                         * (T5_BUCKETS - max_exact)).astype(jnp.int32)
    large = jnp.minimum(large, T5_BUCKETS - 1)
    return jnp.where(n < max_exact, n, large)


def _with_ones(v):
    return jnp.concatenate([v, jnp.ones(v.shape, v.dtype)], axis=1)


def _softmax_update(m, acc, s, v_aug):
    m_new = jnp.maximum(m, jnp.max(s, axis=1, keepdims=True))
    a = jnp.exp2(m - m_new)
    p = jnp.exp2(s - m_new)
    return m_new, a * acc + _dot(p.astype(MXU_DT), v_aug)


def _softmax_once(s, v_aug):
    p = jnp.exp2(s - jnp.max(s, axis=1, keepdims=True))
    return _dot(p.astype(MXU_DT), v_aug)


def _normalise(acc):
    return acc[:, :DH] * (1.0 / acc[:, DH:])


def _bias_table_kernel(t5_ref, idx_ref, rel_ref, o_ref):
    h = pl.program_id(0)
    idx = idx_ref[...]
    acc = jnp.full(idx.shape, NEG, F32)
    for b in range(T5_BUCKETS):
        acc = jnp.where(idx == b, t5_ref[b, h] * LOG2E, acc)
    o_ref[0] = acc - jnp.where(rel_ref[...] > 0, t5_ref[T5_BUCKETS - 1, h] * LOG2E, 0.0)


def bias_table(t5_table, idx, rel, tr):
    rows, cols = idx.shape
    return pl.pallas_call(
        _bias_table_kernel,
        out_shape=jax.ShapeDtypeStruct((NSA_HEADS, rows, cols), F32),
        grid=(NSA_HEADS, rows // tr),
        in_specs=[pl.BlockSpec(memory_space=pltpu.SMEM),
                  pl.BlockSpec((tr, cols), lambda h, i: (i, 0)),
                  pl.BlockSpec((tr, cols), lambda h, i: (i, 0))],
        out_specs=pl.BlockSpec((1, tr, cols), lambda h, i: (h, i, 0)),
        compiler_params=_params("parallel", "parallel"),
        name="bias_table",
    )(t5_table, idx, rel)


def _in_proj_kernel(x_ref, wa_ref, wb_ref, wc_ref, wks_ref, wfk_ref, sc_ref, blk_ref,
                    ya_ref, yb_ref, ykc_ref, yvc_ref, yks_ref, yfk_ref):
    xb = x_ref[...].astype(MXU_DT)
    ya_ref[...] = (_dot(xb, wa_ref[...]) * sc_ref[...]).astype(ya_ref.dtype)
    yb_ref[...] = _dot(xb, wb_ref[...])
    yc = _dot(xb, wc_ref[...])
    ykc_ref[...] = yc[:, :NSA_KV]
    yvc_ref[...] = yc[:, NSA_KV:]
    yks_ref[...] = (_dot(xb, wks_ref[...]) + blk_ref[...]).astype(yks_ref.dtype)
    yfk_ref[...] = _dot(xb, wfk_ref[...]).astype(yfk_ref.dtype)


def in_proj(x2, w, tabs, seq, tm=512):
    t, d = x2.shape
    nps = seq // tm
    row = lambda i: (i, 0)
    fixed = lambda i: (0, 0)
    weights = [w["wa"], w["wb"], w["wc"], w["wks"], w["wfk"]]
    widths = [(_NA, MXU_DT), (_NB, F32), (NSA_KV, F32), (NSA_KV, F32),
              (w["wks"].shape[1], MXU_DT), (w["wfk"].shape[1], MXU_DT)]
    return pl.pallas_call(
        _in_proj_kernel,
        out_shape=tuple(jax.ShapeDtypeStruct((t, n), dt) for n, dt in widths),
        grid=(t // tm,),
        in_specs=[pl.BlockSpec((tm, d), row)] + [pl.BlockSpec(a.shape, fixed) for a in weights]
        + [pl.BlockSpec((1, _NA), fixed),
           pl.BlockSpec((tm, tabs["block_cols"].shape[1]), lambda i: (i % nps, 0))],
        out_specs=tuple(pl.BlockSpec((tm, n), row) for n, _ in widths),
        compiler_params=_params("parallel"),
        name="in_proj",
    )(x2, *weights, tabs["col_scale"], tabs["block_cols"])


def _mla_up_kernel(yb_ref, qn_ref, wqa_ref, wqb_ref, kvn_ref, wk_ref, wv_ref, pa_ref, pb_ref,
                   ct_ref, st_ref, q_ref, k_ref, v_ref):
    scale = (MLA_NOPE + MLA_ROPE) ** -0.5 * LOG2E
    ct = ct_ref[...]
    st = st_ref[...]
    cq = _rms_norm(yb_ref[:, _OFF_CQ:_OFF_CQ + MLA_Q_LORA], qn_ref[...]).astype(MXU_DT)
    q = _dot(cq, wqa_ref[...]) * ct + _dot(cq, wqb_ref[...]) * st
    q_ref[...] = (q * scale).astype(q_ref.dtype)
    ckv = _rms_norm(yb_ref[:, _OFF_CKV:_OFF_CKV + MLA_KV_LORA], kvn_ref[...]).astype(MXU_DT)
    v_ref[...] = _dot(ckv, wv_ref[...]).astype(v_ref.dtype)
    kr = yb_ref[:, _OFF_KR:_OFF_KR + MLA_ROPE]
    kr_hi = kr.astype(MXU_DT)
    kr_lo = (kr - kr_hi.astype(F32)).astype(MXU_DT)
    ka = _dot(kr_hi, pa_ref[...]) + _dot(kr_lo, pa_ref[...])
    kb = _dot(kr_hi, pb_ref[...]) + _dot(kr_lo, pb_ref[...])
    k_ref[...] = (_dot(ckv, wk_ref[...]) + ka * ct + kb * st).astype(k_ref.dtype)


def mla_up(yb, w, tabs, seq, tm=512):
    t = yb.shape[0]
    nps = seq // tm
    row = lambda i: (i, 0)
    fixed = lambda i: (0, 0)
    pos = lambda i: (i % nps, 0)
    wide = MLA_HEADS * LANES
    ins = [w["q_norm"], w["wqa"], w["wqb"], w["kv_norm"], w["wk"], w["wv"], tabs["rope_pa"], tabs["rope_pb"]]
    return pl.pallas_call(
        _mla_up_kernel,
        out_shape=(jax.ShapeDtypeStruct((t, wide), MXU_DT), jax.ShapeDtypeStruct((t, wide), MXU_DT),
                   jax.ShapeDtypeStruct((t, MLA_W), MXU_DT)),
        grid=(t // tm,),
        in_specs=[pl.BlockSpec((tm, _NB), row)] + [pl.BlockSpec(a.shape, fixed) for a in ins]
        + [pl.BlockSpec((tm, wide), pos), pl.BlockSpec((tm, wide), pos)],
        out_specs=(pl.BlockSpec((tm, wide), row), pl.BlockSpec((tm, wide), row),
                   pl.BlockSpec((tm, MLA_W), row)),
        compiler_params=_params("parallel"),
        name="mla_up",
    )(yb, *ins, tabs["rope_cos"], tabs["rope_sin"])


def _bf16_head(x):
    bits = lax.bitcast_convert_type(x, jnp.uint32) & jnp.uint32(0xFFFF0000)
    return lax.bitcast_convert_type(bits, F32)


def _fox_decay_kernel(f_ref, b_ref, place_ref, o_ref):
    z = f_ref[0] + b_ref[...]
    x = jnp.minimum(z, 0.0) - jnp.log1p(jnp.exp(-jnp.abs(z)))
    n = x.shape[0]
    row = lax.broadcasted_iota(jnp.int32, x.shape, 0)
    shift = 1
    while shift < n:
        x = x + jnp.where(row >= shift, pltpu.roll(x, shift, 0), 0.0)
        shift *= 2
    dec = -(x * LOG2E)
    hi = _bf16_head(dec)
    rest = dec - hi
    mid = _bf16_head(rest)
    terms = (hi, mid, rest - mid)
    o_ref[0] = sum(_dot(t.astype(MXU_DT), place_ref[j]) for j, t in enumerate(terms)).astype(o_ref.dtype)


def fox_decay(misc, b_row, place):
    b, s, _ = misc.shape
    wide = place.shape[2]
    return pl.pallas_call(
        _fox_decay_kernel,
        out_shape=jax.ShapeDtypeStruct((b, s, wide), MXU_DT),
        grid=(b,),
        in_specs=[pl.BlockSpec((1, s, LANES), lambda bi: (bi, 0, _COL_MISC)),
                  pl.BlockSpec((1, LANES), lambda bi: (0, 0)),
                  pl.BlockSpec(place.shape, lambda bi: (0, 0, 0))],
        out_specs=pl.BlockSpec((1, s, wide), lambda bi: (bi, 0, 0)),
        compiler_params=_params("parallel"),
        name="fox_decay",
    )(misc, b_row, place)


def _keys_softmax_update(m, acc, s, v_aug):
    m_new = jnp.maximum(m, jnp.max(s, axis=0, keepdims=True))
    a = jnp.exp2(m - m_new)
    p = jnp.exp2(s - m_new).astype(MXU_DT)
    pv = lax.dot_general(v_aug, p, (((0,), (0,)), ((), ())), preferred_element_type=F32)
    return m_new, a * acc + pv


def _keys_softmax_init(queries):
    return jnp.full((1, queries), NEG, F32), jnp.zeros((2 * DH, queries), F32)


def _causal_attn_kernel(*refs, n_heads, dq, decay):
    if decay:
        q_ref, k_ref, kd_ref, v_ref, o_ref = refs
    else:
        q_ref, k_ref, v_ref, o_ref = refs
    qi = pl.program_id(1)
    t = q_ref.shape[1]
    key = lax.broadcasted_iota(jnp.int32, (t, t), 0)
    qry = lax.broadcasted_iota(jnp.int32, (t, t), 1)

    qs = []
    for h in range(n_heads):
        q = q_ref[0, :, h * dq:(h + 1) * dq]
        qs.append(_with_ones(q) if decay else q)

    def scores(rows, h):
        k = k_ref[0, rows, h * LANES:(h + 1) * LANES]
        if decay:
            k = k + kd_ref[0, rows, h * LANES:(h + 1) * LANES]
        return _dot_nt(k, qs[h])

    def values(rows, h):
        return _with_ones(v_ref[0, rows, h * DH:(h + 1) * DH])

    def sweep(rows, state, mask):
        s = [scores(rows, h) for h in range(n_heads)]
        if mask is not None:
            s = [jnp.where(mask, sh, NEG) for sh in s]
        return tuple(_keys_softmax_update(*state[h], s[h], values(rows, h)) for h in range(n_heads))

    n_double = lax.shift_right_logical(qi, 1)
    state = lax.fori_loop(
        0, n_double, lambda p, st: sweep(pl.ds(pl.multiple_of(p * 2 * t, 2 * t), 2 * t), st, None),
        tuple(_keys_softmax_init(t) for _ in range(n_heads)))
    state = lax.fori_loop(
        0, qi & 1, lambda _, st: sweep(pl.ds(pl.multiple_of(n_double * 2 * t, t), t), st, None), state)
    state = sweep(pl.ds(pl.multiple_of(qi * t, t), t), state, key <= qry)
    for h in range(n_heads):
        o_ref[0, :, h * DH:(h + 1) * DH] = _normalise(state[h][1].T).astype(o_ref.dtype)


def causal_attention(q, q_col, k, v, v_col, n_heads, dq, k_decay=None, t=ATT_T):
    b, s, _ = q.shape
    whole = pl.BlockSpec((1, s, n_heads * LANES), lambda bi, qi: (bi, 0, 0))
    keys = [k] if k_decay is None else [k, k_decay]
    return pl.pallas_call(
        functools.partial(_causal_attn_kernel, n_heads=n_heads, dq=dq, decay=k_decay is not None),
        out_shape=jax.ShapeDtypeStruct((b, s, n_heads * DH), MXU_DT),
        grid=(b, s // t),
        in_specs=[pl.BlockSpec((1, t, n_heads * dq), lambda bi, qi: (bi, qi, q_col))]
        + [whole] * len(keys)
        + [pl.BlockSpec((1, s, n_heads * DH), lambda bi, qi: (bi, 0, v_col))],
        out_specs=pl.BlockSpec((1, t, n_heads * DH), lambda bi, qi: (bi, qi, 0)),
        compiler_params=_params("parallel", "arbitrary"),
        name="causal_attn_decay" if k_decay is not None else "causal_attn",
    )(q, *keys, v)


def _compress_kernel(x_ref, pe_ref, wlo_ref, whi_ref, w2_ref, o_ref):
    x = x_ref[0]
    nc = x.shape[0]
    lo = _dot((x + pe_ref[0:1, :]).astype(MXU_DT), wlo_ref[...])
    hi = _dot((x + pe_ref[1:2, :]).astype(MXU_DT), whi_ref[...])
    hid = lo + pltpu.roll(hi, nc - 1, 0)
    act = jax.nn.gelu(hid)
    o_ref[0] = _dot(act.astype(MXU_DT), w2_ref[...]).astype(o_ref.dtype)


def compress(x, pe, wlo, whi, w2):
    b, nc, width = x.shape
    fixed = lambda bi: (0, 0)
    return pl.pallas_call(
        _compress_kernel,
        out_shape=jax.ShapeDtypeStruct((b, nc, NSA_KV), MXU_DT),
        grid=(b,),
        in_specs=[pl.BlockSpec((1, nc, width), lambda bi: (bi, 0, 0)),
                  pl.BlockSpec(pe.shape, fixed), pl.BlockSpec(wlo.shape, fixed),
                  pl.BlockSpec(whi.shape, fixed), pl.BlockSpec(w2.shape, fixed)],
        out_specs=pl.BlockSpec((1, nc, NSA_KV), lambda bi: (bi, 0, 0)),
        compiler_params=_params("parallel"),
        name="nsa_compress",
    )(x, pe, wlo, whi, w2)


def _nsa_kernel(q_ref, kc_ref, vc_ref, ks_ref, vs_ref, kw_ref, vw_ref, misc_ref,
                bc_ref, bw_ref, ovt_ref, o_ref, *, n_sel):
    i = pl.program_id(1)
    tq, rq = NSA_TQ, NSA_RQ
    nslc = ovt_ref.shape[0]
    seg = DH + nslc
    nwt = WINDOW // tq + 1
    sub = SEL_TK // tq
    t0 = i * tq
    g_off = _OFF_NG - _COL_MISC * LANES
    gates_t = jax.nn.sigmoid(misc_ref[0]).T

    lane = lax.broadcasted_iota(jnp.int32, (1, rq), 1)
    has_cmp = (t0 + (lane & (tq - 1))) >= (L_CMP - 1)

    blk = lax.broadcasted_iota(jnp.int32, (nslc, tq), 0)
    blk_f = blk.astype(F32)
    tpos = lax.broadcasted_iota(jnp.int32, (nslc, tq), 1) + t0
    cur = lax.shift_right_logical(tpos, int(math.log2(L_SLC)))
    forced = (blk == 0) | (blk == cur) | (blk == cur - 1)
    causal_blk = blk * L_SLC <= tpos
    eye = (lax.broadcasted_iota(jnp.int32, (tq, tq), 0)
           == lax.broadcasted_iota(jnp.int32, (tq, tq), 1)).astype(MXU_DT)

    def window_tile_kind(d):
        static = {0: TILE_DIAG, 1: TILE_PREV, nwt - 1: TILE_EDGE}.get(d, TILE_FAR)
        return jnp.where(i - d < 0, TILE_NONE, static)

    def group_cols(g):
        return slice(g * DH, (g + 1) * DH)

    groups = range(NSA_GROUPS)
    q_groups = [jnp.concatenate([q_ref[0, :, (g * NSA_R + r) * DH:(g * NSA_R + r + 1) * DH]
                                 for r in range(NSA_R)], axis=0) for g in groups]

    s_cmp = [_dot_nt(kc_ref[0, :, group_cols(g)], q_groups[g]) + bc_ref[g, 0] for g in groups]
    win_rows = [pl.ds(pl.multiple_of(jnp.maximum(i - d, 0) * tq, tq), tq) for d in range(nwt - 1, -1, -1)]
    s_win = []
    for g in groups:
        kcat = jnp.concatenate([kw_ref[0, r, group_cols(g)] for r in win_rows], axis=0)
        bias = jnp.concatenate([bw_ref[g, window_tile_kind(d)] for d in range(nwt - 1, -1, -1)], axis=0)
        s_win.append(_dot_nt(kcat, q_groups[g]) + bias)

    def gate_row(g, j):
        return jnp.concatenate(
            [gates_t[g_off + 3 * (g * NSA_R + r) + j:g_off + 3 * (g * NSA_R + r) + j + 1] for r in range(NSA_R)],
            axis=1)

    def weighted_values(v, p):
        return lax.dot_general(_with_ones(v), p, (((0,), (0,)), ((), ())), preferred_element_type=F32)

    o_cmp, imp = [], []
    for g in groups:
        e = jnp.exp2(s_cmp[g] - jnp.max(s_cmp[g], axis=0, keepdims=True)).astype(MXU_DT)
        acc = weighted_values(vc_ref[0, :, group_cols(g)], e)
        inv = jnp.where(has_cmp, 1.0 / acc[DH:DH + 1], 0.0)
        o_cmp.append(acc[:DH] * inv)
        imp4 = _dot(ovt_ref[...], e) * inv
        imp.append(sum(imp4[:, r * tq:(r + 1) * tq] for r in range(1, NSA_R)) + imp4[:, 0:tq])

    o_fixed = []
    for g in groups:
        vcat = jnp.concatenate([vw_ref[0, r, group_cols(g)] for r in win_rows], axis=0)
        p = jnp.exp2(s_win[g] - jnp.max(s_win[g], axis=0, keepdims=True)).astype(MXU_DT)
        acc = weighted_values(vcat, p)
        o_win = acc[:DH] * (1.0 / acc[DH:DH + 1])
        o_fixed.append(gate_row(g, 0) * o_cmp[g] + gate_row(g, 2) * o_win)

    q_aug = []
    for g in groups:
        x = jnp.where(causal_blk, imp[g] + jnp.where(forced, FORCE, 0.0), NEG)
        sel_t = jnp.zeros((nslc, tq), F32)
        for _ in range(n_sel):
            mx = jnp.max(x, axis=0, keepdims=True)
            first = jnp.min(jnp.where(x == mx, blk_f, float(nslc)), axis=0, keepdims=True)
            hit = blk_f == first
            sel_t = jnp.where(hit, 1.0, sel_t)
            x = jnp.where(hit, -3e38, x)
        penalty = ((1.0 - _dot_nt(eye, sel_t.astype(MXU_DT))) * NEG).astype(MXU_DT)
        q_aug.append(jnp.concatenate([q_groups[g], jnp.concatenate([penalty] * NSA_R, axis=0)], axis=1))

    def sel_scores(rows, g):
        return _dot_nt(ks_ref[0, rows, g * seg:(g + 1) * seg], q_aug[g])

    def sel_values(rows, g):
        return _with_ones(vs_ref[0, rows, group_cols(g)])

    def far_sweep(rows, state):
        s = [sel_scores(rows, g) for g in groups]
        return tuple(_keys_softmax_update(*state[g], s[g], sel_values(rows, g)) for g in groups)

    last = i // sub
    n_far = jnp.maximum(last - 1, 0)
    n_double = lax.shift_right_logical(n_far, 1)
    state = lax.fori_loop(
        0, n_double,
        lambda p, st: far_sweep(pl.ds(pl.multiple_of(p * 2 * SEL_TK, 2 * SEL_TK), 2 * SEL_TK), st),
        tuple(_keys_softmax_init(rq) for _ in groups))
    state = lax.fori_loop(
        0, n_far & 1,
        lambda _, st: far_sweep(pl.ds(pl.multiple_of(n_double * 2 * SEL_TK, SEL_TK), SEL_TK), st), state)
    near = []
    for kn in (last - 1, last):
        kt = pl.ds(pl.multiple_of(jnp.maximum(kn, 0) * SEL_TK, SEL_TK), SEL_TK)
        kinds = []
        for j in range(sub):
            d = i - (kn * sub + j)
            kind = jnp.where(d < 0, TILE_NONE, jnp.where(d == 0, TILE_SEL_DIAG,
                                                         jnp.where(d == 1, TILE_SEL_PREV, TILE_ZERO)))
            kinds.append(jnp.where(kn < 0, TILE_NONE, kind))
        for g in groups:
            bias = jnp.concatenate([bw_ref[g, kd] for kd in kinds], axis=0)
            near.append((kt, g, sel_scores(kt, g) + bias))
    state = list(state)
    for kt, g, s in near:
        state[g] = _keys_softmax_update(*state[g], s, sel_values(kt, g))

    out_t = []
    for g in groups:
        acc = state[g][1]
        out_t.append(o_fixed[g] + gate_row(g, 1) * (acc[:DH] * (1.0 / acc[DH:DH + 1])))
    out_t = jnp.concatenate(out_t, axis=0)
    for r in range(NSA_R):
        slab = out_t[:, r * tq:(r + 1) * tq].T
        for g in groups:
            h = g * NSA_R + r
            o_ref[0, :, h * DH:(h + 1) * DH] = slab[:, g * DH:(g + 1) * DH].astype(o_ref.dtype)


def nsa_attention(ya, yb, kc, vc, ks_aug, bias_c, bias_w, overlap_t):
    b, s, _ = ya.shape
    tq = NSA_TQ
    ncp = kc.shape[1]
    per_b_kv = lambda col: pl.BlockSpec((1, s, NSA_KV), lambda bi, i: (bi, 0, col))
    per_b_cmp = pl.BlockSpec((1, ncp, NSA_KV), lambda bi, i: (bi, 0, 0))
    return pl.pallas_call(
        functools.partial(_nsa_kernel, n_sel=min(N_SEL, overlap_t.shape[0])),
        out_shape=jax.ShapeDtypeStruct((b, s, NSA_W), MXU_DT),
        grid=(b, s // tq),
        in_specs=[pl.BlockSpec((1, tq, NSA_W), lambda bi, i: (bi, i, 0)),
                  per_b_cmp, per_b_cmp,
                  pl.BlockSpec((1, s, ks_aug.shape[2]), lambda bi, i: (bi, 0, 0)),
                  per_b_kv(_COL_VS), per_b_kv(_COL_VS + 1), per_b_kv(_COL_VS + 2),
                  pl.BlockSpec((1, tq, LANES), lambda bi, i: (bi, i, _COL_MISC)),
                  pl.BlockSpec((NSA_GROUPS, 1, ncp, NSA_RQ), lambda bi, i: (0, i, 0, 0)),
                  pl.BlockSpec(bias_w.shape, lambda bi, i: (0, 0, 0, 0)),
                  pl.BlockSpec(overlap_t.shape, lambda bi, i: (0, 0))],
        out_specs=pl.BlockSpec((1, tq, NSA_W), lambda bi, i: (bi, i, 0)),
        compiler_params=_params("parallel", "arbitrary"),
        name="nsa_attn",
    )(ya, kc, vc, ks_aug, ya, ya, ya, yb, bias_c, bias_w, overlap_t)


def _mix_kernel(x_ref, on_ref, om_ref, of_ref, wg_ref, wn_ref, wm_ref, wf_ref, wo_ref,
                g_ref, b_ref, o_ref, *, alpha):
    x = x_ref[...]
    xb = x.astype(MXU_DT)
    d = x.shape[1]
    merged = (jax.nn.sigmoid(_dot(xb, wg_ref[:, 0:d])) * _dot(on_ref[...], wn_ref[...])
              + jax.nn.sigmoid(_dot(xb, wg_ref[:, d:2 * d])) * _dot(om_ref[...], wm_ref[...])
              + jax.nn.sigmoid(_dot(xb, wg_ref[:, 2 * d:3 * d])) * _dot(of_ref[...], wf_ref[...]))
    y = _dot(merged.astype(MXU_DT), wo_ref[...])
    o_ref[...] = _layer_norm(alpha * x + y, g_ref[...], b_ref[...])


def mixer_out(x2, o_nsa, o_mla, o_fox, wg, wn, wm, wf, wo, ln_g, ln_b, alpha, tm=512):
    t, d = x2.shape
    row = lambda i: (i, 0)
    fixed = lambda i: (0, 0)
    resident = lambda a: pl.BlockSpec(a.shape, fixed, pipeline_mode=pl.Buffered(1))
    return pl.pallas_call(
        functools.partial(_mix_kernel, alpha=alpha),
        out_shape=jax.ShapeDtypeStruct((t, d), F32),
        grid=(t // tm,),
        in_specs=[pl.BlockSpec((tm, d), row),
                  pl.BlockSpec((tm, NSA_W), row),
                  pl.BlockSpec((tm, MLA_W), row),
                  pl.BlockSpec((tm, FOX_W), row),
                  resident(wg), resident(wn), resident(wm), resident(wf), resident(wo),
                  pl.BlockSpec((1, d), fixed), pl.BlockSpec((1, d), fixed)],
        out_specs=pl.BlockSpec((tm, d), row),
        compiler_params=_params("parallel"),
        name="mixer_out",
    )(x2, o_nsa, o_mla, o_fox, wg, wn, wm, wf, wo, ln_g, ln_b)


def _matmul_kernel(x_ref, w_ref, o_ref):
    o_ref[...] = _dot(x_ref[...].astype(MXU_DT), w_ref[...]).astype(o_ref.dtype)


def matmul(x2, w, out_dtype, tm):
    t, d = x2.shape
    n = w.shape[1]
    return pl.pallas_call(
        _matmul_kernel,
        out_shape=jax.ShapeDtypeStruct((t, n), out_dtype),
        grid=(t // tm,),
        in_specs=[pl.BlockSpec((tm, d), lambda i: (i, 0)), pl.BlockSpec((d, n), lambda i: (0, 0))],
        out_specs=pl.BlockSpec((tm, n), lambda i: (i, 0)),
        compiler_params=_params("parallel"),
        name="mem_kv_proj",
    )(x2, w)


def _xattn_kernel(x_ref, wq_ref, kv_ref, wo_ref, g_ref, b_ref, o_ref, *, alpha):
    x = x_ref[0]
    q = (_dot(x.astype(MXU_DT), wq_ref[...]) * LOG2E).astype(MXU_DT)
    s = [_dot_nt(kv_ref[0, :, h * DH:(h + 1) * DH], q[:, h * DH:(h + 1) * DH]) for h in range(XA_HEADS)]
    outs = []
    for h in range(XA_HEADS):
        v = kv_ref[0, :, XA_W + h * DH:XA_W + (h + 1) * DH]
        p = jnp.exp2(s[h] - jnp.max(s[h], axis=0, keepdims=True)).astype(MXU_DT)
        acc = lax.dot_general(_with_ones(v), p, (((0,), (0,)), ((), ())), preferred_element_type=F32)
        outs.append(acc[:DH] * (1.0 / acc[DH:DH + 1]))
    o = jnp.concatenate(outs, axis=0).T.astype(MXU_DT)
    y = _dot(o, wo_ref[...])
    o_ref[0] = _layer_norm(alpha * x + y, g_ref[...], b_ref[...])


def cross_attention(x, kv, wq, wo, ln_g, ln_b, alpha, tm=512):
    b, s, d = x.shape
    m = kv.shape[1]
    fixed = lambda bi, i: (0, 0)
    return pl.pallas_call(
        functools.partial(_xattn_kernel, alpha=alpha),
        out_shape=jax.ShapeDtypeStruct((b, s, d), F32),
        grid=(b, s // tm),
        in_specs=[pl.BlockSpec((1, tm, d), lambda bi, i: (bi, i, 0)),
                  pl.BlockSpec(wq.shape, fixed),
                  pl.BlockSpec((1, m, 2 * XA_W), lambda bi, i: (bi, 0, 0)),
                  pl.BlockSpec(wo.shape, fixed),
                  pl.BlockSpec((1, d), fixed), pl.BlockSpec((1, d), fixed)],
        out_specs=pl.BlockSpec((1, tm, d), lambda bi, i: (bi, i, 0)),
        compiler_params=_params("parallel", "parallel"),
        name="cross_attn",
    )(x, wq, kv, wo, ln_g, ln_b)


def _mlp_kernel(x_ref, wu_ref, wd_ref, g_ref, b_ref, o_ref, *, alpha):
    x = x_ref[...]
    hdn = jnp.square(jnp.maximum(_dot(x.astype(MXU_DT), wu_ref[...]), 0.0))
    y = _dot(hdn.astype(MXU_DT), wd_ref[...])
    o_ref[...] = _layer_norm(alpha * x + y, g_ref[...], b_ref[...])


def mlp(x2, wu, wd, ln_g, ln_b, alpha, tm=512):
    t, d = x2.shape
    row = lambda i: (i, 0)
    fixed = lambda i: (0, 0)
    resident = lambda a: pl.BlockSpec(a.shape, fixed, pipeline_mode=pl.Buffered(1))
    return pl.pallas_call(
        functools.partial(_mlp_kernel, alpha=alpha),
        out_shape=jax.ShapeDtypeStruct((t, d), F32),
        grid=(t // tm,),
        in_specs=[pl.BlockSpec((tm, d), row), resident(wu), resident(wd),
                  pl.BlockSpec((1, d), fixed), pl.BlockSpec((1, d), fixed)],
        out_specs=pl.BlockSpec((tm, d), row),
        compiler_params=_params("parallel"),
        name="mlp",
    )(x2, wu, wd, ln_g, ln_b)


def _segment_cols(w, cols_per_head, head_offset, width=LANES):
    out = jnp.zeros(w.shape[:-1] + (len(cols_per_head) * width,), w.dtype)
    for h, cols in enumerate(cols_per_head):
        start = h * width + head_offset
        out = out.at[..., start:start + len(cols)].set(w[..., np.asarray(cols)])
    return out


def _prep_layer_weights(seq, w_in, cmp_pe, cmp_w1, cmp_w2, mla_q_norm, mla_w_uq, mla_kv_norm, mla_w_ukv,
                        fox_b_f, w_gate, w_br_nsa, w_br_mla, w_br_fox, w_mix_out, xa_w_q, xa_w_kv,
                        xa_w_o, mlp_w_up, mlp_w_down, ln_g, ln_b):
    depth, d, _ = w_in.shape
    sizes = (NSA_W, NSA_KV, NSA_KV, NSA_KV, NSA_KV, NSA_KV, NSA_KV, 3 * NSA_HEADS,
             MLA_Q_LORA, MLA_KV_LORA, MLA_ROPE, FOX_W, FOX_W, FOX_W, FOX_HEADS)
    offs = np.concatenate([[0], np.cumsum(sizes)])
    col = lambda n: w_in[:, :, offs[n]:offs[n + 1]]
    (nq, nkc, nvc, nks, nvs, nkw, nvw, ngate, cq, ckv, kr, fq, fk, fv, ff) = [col(n) for n in range(15)]
    qk_scale = DH ** -0.5
    wa = jnp.concatenate([nq * qk_scale, fq * qk_scale, fv, nvs, nkw, nvw], axis=2)
    pad = jnp.zeros((depth, d, _NB - (_OFF_FF + FOX_HEADS)), F32)
    wb = jnp.concatenate([cq, ckv, kr, ngate, ff, pad], axis=2)
    wc = jnp.concatenate([nkc, nvc], axis=2)
    wks = _segment_cols(nks, [np.arange(g * DH, (g + 1) * DH) for g in range(NSA_GROUPS)], 0,
                        DH + seq // L_SLC)
    wfk = _segment_cols(fk, [np.arange(h * DH, (h + 1) * DH) for h in range(FOX_HEADS)], 0)
    ff_lane = _OFF_FF - _COL_MISC * LANES
    b_row = jnp.zeros((depth, 1, LANES), F32).at[:, 0, ff_lane:ff_lane + FOX_HEADS].set(fox_b_f)

    dq = MLA_NOPE + MLA_ROPE
    half = MLA_ROPE // 2
    heads = range(MLA_HEADS)
    nope = [np.arange(h * dq, h * dq + MLA_NOPE) for h in heads]
    x1 = [np.arange(h * dq + MLA_NOPE, h * dq + MLA_NOPE + half) for h in heads]
    x2 = [c + half for c in x1]
    wqa = (_segment_cols(mla_w_uq, nope, 0) + _segment_cols(mla_w_uq, x1, MLA_NOPE)
           + _segment_cols(mla_w_uq, x2, MLA_NOPE + half))
    wqb = _segment_cols(mla_w_uq, x2, MLA_NOPE) + _segment_cols(mla_w_uq, x1, MLA_NOPE + half)
    dkv = MLA_NOPE + MLA_V
    wk = _segment_cols(mla_w_ukv, [np.arange(h * dkv, h * dkv + MLA_NOPE) for h in heads], 0)
    wv = mla_w_ukv[:, :, np.concatenate([np.arange(h * dkv + MLA_NOPE, (h + 1) * dkv) for h in heads])]

    eye_g = jnp.eye(NSA_GROUPS, dtype=F32)
    w1r = cmp_w1.reshape(depth, 2, L_CMP, DH, CMP_HIDDEN)
    def chunk_weights(w1_half):
        blk = jnp.einsum('ealdj,gh->ealgdhj', w1_half, eye_g)
        return blk.reshape(depth, 2, D_CMP * NSA_KV, NSA_GROUPS * CMP_HIDDEN)
    w1lo = chunk_weights(w1r[:, :, :D_CMP])
    w1hi = chunk_weights(w1r[:, :, D_CMP:])
    w2b = jnp.einsum('eajd,gh->eagjhd', cmp_w2, eye_g).reshape(depth, 2, NSA_GROUPS * CMP_HIDDEN, NSA_KV)
    pe = jnp.broadcast_to(cmp_pe.reshape(depth, 2, 2, D_CMP, 1, DH),
                          (depth, 2, 2, D_CMP, NSA_GROUPS, DH)).reshape(depth, 2, 2, D_CMP * NSA_KV)

    c = lambda a: a.astype(MXU_DT)
    return dict(
        wa=c(wa), wb=c(wb), wc=c(wc), wks=c(wks), wfk=c(wfk),
        pe=pe, w1lo=c(w1lo), w1hi=c(w1hi), w2b=c(w2b),
        q_norm=mla_q_norm[:, None, :], wqa=c(wqa), wqb=c(wqb),
        kv_norm=mla_kv_norm[:, None, :], wk=c(wk), wv=c(wv),
        b_row=b_row, wg=c(w_gate), wn=c(w_br_nsa), wm=c(w_br_mla), wf=c(w_br_fox), wo=c(w_mix_out),
        xq=c(xa_w_q * qk_scale), xkv=c(xa_w_kv), xo=c(xa_w_o),
        wu=c(mlp_w_up), wd=c(mlp_w_down),
        ln_g=ln_g[:, :, None, :], ln_b=ln_b[:, :, None, :],
    )


def _layer(x, mem2, w, tabs, alpha):
    b, s, d = x.shape
    t = b * s
    x2 = x.reshape(t, d)

    ya, yb, ykc, yvc, yks, yfk = in_proj(x2, w, tabs, s)
    ya3 = ya.reshape(b, s, _NA)
    yb3 = yb.reshape(b, s, _NB)

    nc = s // D_CMP
    kc = compress(ykc.reshape(b, nc, D_CMP * NSA_KV), w["pe"][0], w["w1lo"][0], w["w1hi"][0], w["w2b"][0])
    vc = compress(yvc.reshape(b, nc, D_CMP * NSA_KV), w["pe"][1], w["w1lo"][1], w["w1hi"][1], w["w2b"][1])
    o_nsa = nsa_attention(ya3, yb3, kc, vc, yks.reshape(b, s, -1),
                          tabs["bias_c"], tabs["bias_w"], tabs["overlap_t"])

    q_mla, k_mla, v_mla = mla_up(yb, w, tabs, s)
    wide = MLA_HEADS * LANES
    o_mla = causal_attention(q_mla.reshape(b, s, wide), 0, k_mla.reshape(b, s, wide),
                             v_mla.reshape(b, s, MLA_W), 0, MLA_HEADS, LANES)

    k_decay = fox_decay(yb3, w["b_row"], tabs["decay_place"])
    o_fox = causal_attention(ya3, _COL_FQ, yfk.reshape(b, s, FOX_HEADS * LANES), ya3, _COL_FQ + 1,
                             FOX_HEADS, DH, k_decay=k_decay)

    x2 = mixer_out(x2, o_nsa.reshape(t, NSA_W), o_mla.reshape(t, MLA_W), o_fox.reshape(t, FOX_W),
                   w["wg"], w["wn"], w["wm"], w["wf"], w["wo"], w["ln_g"][0], w["ln_b"][0], alpha)

    m = mem2.shape[0] // b
    kv = matmul(mem2, w["xkv"], MXU_DT, tm=min(512, mem2.shape[0])).reshape(b, m, 2 * XA_W)
    x3 = cross_attention(x2.reshape(b, s, d), kv, w["xq"], w["xo"], w["ln_g"][1], w["ln_b"][1], alpha)

    x2 = mlp(x3.reshape(t, d), w["wu"], w["wd"], w["ln_g"][2], w["ln_b"][2], alpha)
    return x2.reshape(b, s, d)


def _group_stack_t(a):
    _, n, tq, c = a.shape
    a = a.reshape(NSA_GROUPS, NSA_R, n, tq, c).transpose(0, 2, 4, 1, 3)
    return a.reshape(NSA_GROUPS, n, c, NSA_R * tq)


def _tables(t5_table, s):
    tq = NSA_TQ
    qi = jnp.arange(tq)[:, None]
    ki = jnp.arange(tq)[None, :]
    last = jnp.full((tq, tq), T5_BUCKETS - 1, jnp.int32)
    masked = jnp.full((tq, tq), -1, jnp.int32)
    kinds = [None] * N_TILE_KINDS
    kinds[TILE_DIAG] = jnp.where(qi >= ki, _t5_bucket(qi - ki), -1)
    kinds[TILE_PREV] = _t5_bucket(tq + qi - ki)
    kinds[TILE_FAR] = last
    kinds[TILE_EDGE] = jnp.where(qi < ki, last, masked)
    kinds[TILE_NONE] = masked
    kinds[TILE_SEL_DIAG] = kinds[TILE_DIAG]
    kinds[TILE_SEL_PREV] = kinds[TILE_PREV]
    kinds[TILE_ZERO] = last
    relative = [int(k in (TILE_SEL_DIAG, TILE_SEL_PREV, TILE_ZERO)) for k in range(N_TILE_KINDS)]
    idx_w = jnp.concatenate(kinds, axis=0).astype(jnp.int32)
    rel_w = jnp.concatenate([jnp.full((tq, tq), r, jnp.int32) for r in relative], axis=0)
    bias_w = bias_table(t5_table, idx_w, rel_w, tq).reshape(NSA_HEADS, N_TILE_KINDS, tq, tq)
    bias_w = _group_stack_t(bias_w)

    nc = s // D_CMP
    cmp_end = jnp.arange(nc) * D_CMP + (L_CMP - 1)
    dist_c = jnp.arange(s)[:, None] - cmp_end[None, :]
    idx_c = jnp.where(dist_c >= 0, _t5_bucket(dist_c), -1).astype(jnp.int32)
    bias_c = bias_table(t5_table, idx_c, jnp.zeros_like(idx_c), min(512, s))
    bias_c = _group_stack_t(bias_c.reshape(NSA_HEADS, s // tq, tq, nc))

    n_slc = s // L_SLC
    block_onehot = (np.arange(s)[:, None] // L_SLC == np.arange(n_slc)[None, :]).astype(np.float32)
    block_cols = np.concatenate([np.zeros((s, DH), np.float32), block_onehot] * NSA_GROUPS, axis=1)
    ff_lane = _OFF_FF - _COL_MISC * LANES
    decay_place = np.zeros((3, LANES, FOX_HEADS * LANES), np.float32)
    for j in range(3):
        for h in range(FOX_HEADS):
            decay_place[j, ff_lane + h, h * LANES + DH + j] = 1.0
    c_lo = np.arange(nc)[:, None] * D_CMP
    s_lo = np.arange(n_slc)[None, :] * L_SLC
    overlap = np.maximum(np.minimum(c_lo + L_CMP, s_lo + L_SLC) - np.maximum(c_lo, s_lo), 0) / D_CMP
    overlap[nc - 1, :] = 0.0

    half = MLA_ROPE // 2
    inv = ROPE_THETA ** (-jnp.arange(half, dtype=F32) / half)
    ang = jnp.arange(s).astype(F32)[:, None] * inv[None, :]
    cos, sin = jnp.cos(ang), jnp.sin(ang)
    tail = jnp.zeros((s, LANES - MLA_NOPE - MLA_ROPE), F32)
    seg_c = jnp.concatenate([jnp.ones((s, MLA_NOPE), F32), cos, cos, tail], axis=1)
    seg_s = jnp.concatenate([jnp.zeros((s, MLA_NOPE), F32), -sin, sin, tail], axis=1)
    pa = np.zeros((MLA_ROPE, MLA_HEADS * LANES), np.float32)
    pb = np.zeros((MLA_ROPE, MLA_HEADS * LANES), np.float32)
    for h in range(MLA_HEADS):
        for j in range(MLA_ROPE):
            pa[j, h * LANES + MLA_NOPE + j] = 1.0
            pb[j, h * LANES + MLA_NOPE + (j + half) % MLA_ROPE] = 1.0

    col_scale = np.ones((1, _NA), np.float32)
    col_scale[0, :NSA_W] = LOG2E
    col_scale[0, _COL_FQ * FOX_W:(_COL_FQ + 1) * FOX_W] = LOG2E
    return dict(bias_w=bias_w, bias_c=bias_c, overlap_t=jnp.asarray(overlap.T, MXU_DT),
                block_cols=jnp.asarray(block_cols), decay_place=jnp.asarray(decay_place, MXU_DT),
                rope_cos=jnp.tile(seg_c, (1, MLA_HEADS)), rope_sin=jnp.tile(seg_s, (1, MLA_HEADS)),
                rope_pa=jnp.asarray(pa, MXU_DT), rope_pb=jnp.asarray(pb, MXU_DT),
                col_scale=jnp.asarray(col_scale))


def kernel(x, mem, w_in, cmp_pe, cmp_w1, cmp_w2, t5_table, mla_q_norm, mla_w_uq, mla_kv_norm, mla_w_ukv, fox_b_f, w_gate, w_br_nsa, w_br_mla, w_br_fox, w_mix_out, xa_w_q, xa_w_kv, xa_w_o, mlp_w_up, mlp_w_down, ln_g, ln_b):
    depth = w_in.shape[0]
    b, s, d = x.shape
    alpha = (2 * depth) ** 0.25
    weights = _prep_layer_weights(s, w_in, cmp_pe, cmp_w1, cmp_w2, mla_q_norm, mla_w_uq, mla_kv_norm,
                                  mla_w_ukv, fox_b_f, w_gate, w_br_nsa, w_br_mla, w_br_fox, w_mix_out,
                                  xa_w_q, xa_w_kv, xa_w_o, mlp_w_up, mlp_w_down, ln_g, ln_b)
    tabs = _tables(t5_table, s)
    mem2 = mem.reshape(-1, d)

    def step(xc, w):
        return _layer(xc, mem2, w, tabs, alpha), None

    out, _ = lax.scan(step, x, weights)
    return out
```

```python
import functools
import math

import numpy as np
import jax
import jax.numpy as jnp
from jax import lax
from jax.experimental import pallas as pl
from jax.experimental.pallas import tpu as pltpu

D_MODEL = 1024
DH = 64
NSA_HEADS = 8
NSA_GROUPS = 2
NSA_R = NSA_HEADS // NSA_GROUPS
L_CMP = 32
D_CMP = 16
CMP_HIDDEN = 128
L_SLC = 64
N_SEL = 8
WINDOW = 512
MLA_HEADS = 4
MLA_NOPE = 64
MLA_ROPE = 32
MLA_V = 64
MLA_Q_LORA = 384
MLA_KV_LORA = 128
ROPE_THETA = 10000.0
FOX_HEADS = 4
XA_HEADS = 4
D_FF = 4 * D_MODEL
T5_BUCKETS = 32
T5_MAX_DIST = 128
NSA_W = NSA_HEADS * DH
NSA_KV = NSA_GROUPS * DH
MLA_W = MLA_HEADS * MLA_V
FOX_W = FOX_HEADS * DH
XA_W = XA_HEADS * DH
LN_EPS = 1e-5
RMS_EPS = 1e-6
NEG = -1e30
FORCE = 1e4
LOG2E = math.log2(math.e)

F32 = jnp.float32
MXU_DT = jnp.bfloat16
LANES = 128
NSA_TQ = 128
NSA_RQ = NSA_R * NSA_TQ
SEL_TK = 512
ATT_T = 512
VMEM_LIMIT = 48 * 1024 * 1024

TILE_DIAG = 0
TILE_PREV = 1
TILE_FAR = 2
TILE_EDGE = 3
TILE_NONE = 4
TILE_SEL_DIAG = 5
TILE_SEL_PREV = 6
TILE_ZERO = 7
N_TILE_KINDS = 8

_NA = NSA_W + 2 * FOX_W + 3 * NSA_KV
_COL_FQ = NSA_W // FOX_W
_COL_VS = (NSA_W + 2 * FOX_W) // NSA_KV
_OFF_CQ = 0
_OFF_CKV = MLA_Q_LORA
_OFF_KR = _OFF_CKV + MLA_KV_LORA
_OFF_NG = _OFF_KR + MLA_ROPE
_OFF_FF = _OFF_NG + 3 * NSA_HEADS
_NB = 640
_COL_MISC = _OFF_KR // LANES


def _dot(a, b):
    return jnp.dot(a, b, preferred_element_type=F32)


def _dot_nt(a, b):
    return lax.dot_general(a, b, (((1,), (1,)), ((), ())), preferred_element_type=F32)


def _params(*sem):
    return pltpu.CompilerParams(dimension_semantics=sem, vmem_limit_bytes=VMEM_LIMIT)


def _layer_norm(z, g, b):
    mu = jnp.mean(z, axis=-1, keepdims=True)
    zc = z - mu
    var = jnp.mean(zc * zc, axis=-1, keepdims=True)
    return zc * lax.rsqrt(var + LN_EPS) * g + b


def _rms_norm(z, g):
    return z * lax.rsqrt(jnp.mean(z * z, axis=-1, keepdims=True) + RMS_EPS) * g


def _t5_bucket(dist):
    n = jnp.maximum(dist, 0)
    max_exact = T5_BUCKETS // 2
    nf = jnp.maximum(n, 1).astype(F32)
    large = max_exact + (jnp.log(nf / max_exact) / math.log(T5_MAX_DIST / max_exact)
                         * (T5_BUCKETS - max_exact)).astype(jnp.int32)
    large = jnp.minimum(large, T5_BUCKETS - 1)
    return jnp.where(n < max_exact, n, large)


def _with_ones(v):
    return jnp.concatenate([v, jnp.ones(v.shape, v.dtype)], axis=1)


def _softmax_update(m, acc, s, v_aug):
    m_new = jnp.maximum(m, jnp.max(s, axis=1, keepdims=True))
    a = jnp.exp2(m - m_new)
    p = jnp.exp2(s - m_new)
    return m_new, a * acc + _dot(p.astype(MXU_DT), v_aug)


def _softmax_once(s, v_aug):
    p = jnp.exp2(s - jnp.max(s, axis=1, keepdims=True))
    return _dot(p.astype(MXU_DT), v_aug)


def _normalise(acc):
    return acc[:, :DH] * (1.0 / acc[:, DH:])


def _bias_table_kernel(t5_ref, idx_ref, rel_ref, o_ref):
    h = pl.program_id(0)
    idx = idx_ref[...]
    acc = jnp.full(idx.shape, NEG, F32)
    for b in range(T5_BUCKETS):
        acc = jnp.where(idx == b, t5_ref[b, h] * LOG2E, acc)
    o_ref[0] = acc - jnp.where(rel_ref[...] > 0, t5_ref[T5_BUCKETS - 1, h] * LOG2E, 0.0)


def bias_table(t5_table, idx, rel, tr):
    rows, cols = idx.shape
    return pl.pallas_call(
        _bias_table_kernel,
        out_shape=jax.ShapeDtypeStruct((NSA_HEADS, rows, cols), F32),
        grid=(NSA_HEADS, rows // tr),
        in_specs=[pl.BlockSpec(memory_space=pltpu.SMEM),
                  pl.BlockSpec((tr, cols), lambda h, i: (i, 0)),
                  pl.BlockSpec((tr, cols), lambda h, i: (i, 0))],
        out_specs=pl.BlockSpec((1, tr, cols), lambda h, i: (h, i, 0)),
        compiler_params=_params("parallel", "parallel"),
        name="bias_table",
    )(t5_table, idx, rel)


def _in_proj_kernel(x_ref, wa_ref, wb_ref, wc_ref, wks_ref, wfk_ref, sc_ref, blk_ref,
                    ya_ref, yb_ref, ykc_ref, yvc_ref, yks_ref, yfk_ref):
    xb = x_ref[...].astype(MXU_DT)
    ya_ref[...] = (_dot(xb, wa_ref[...]) * sc_ref[...]).astype(ya_ref.dtype)
    yb_ref[...] = _dot(xb, wb_ref[...])
    yc = _dot(xb, wc_ref[...])
    ykc_ref[...] = yc[:, :NSA_KV]
    yvc_ref[...] = yc[:, NSA_KV:]
    yks_ref[...] = (_dot(xb, wks_ref[...]) + blk_ref[...]).astype(yks_ref.dtype)
    yfk_ref[...] = _dot(xb, wfk_ref[...]).astype(yfk_ref.dtype)


def in_proj(x2, w, tabs, seq, tm=512):
    t, d = x2.shape
    nps = seq // tm
    row = lambda i: (i, 0)
    fixed = lambda i: (0, 0)
    weights = [w["wa"], w["wb"], w["wc"], w["wks"], w["wfk"]]
    widths = [(_NA, MXU_DT), (_NB, F32), (NSA_KV, F32), (NSA_KV, F32),
              (w["wks"].shape[1], MXU_DT), (w["wfk"].shape[1], MXU_DT)]
    return pl.pallas_call(
        _in_proj_kernel,
        out_shape=tuple(jax.ShapeDtypeStruct((t, n), dt) for n, dt in widths),
        grid=(t // tm,),
        in_specs=[pl.BlockSpec((tm, d), row)] + [pl.BlockSpec(a.shape, fixed) for a in weights]
        + [pl.BlockSpec((1, _NA), fixed),
           pl.BlockSpec((tm, tabs["block_cols"].shape[1]), lambda i: (i % nps, 0))],
        out_specs=tuple(pl.BlockSpec((tm, n), row) for n, _ in widths),
        compiler_params=_params("parallel"),
        name="in_proj",
    )(x2, *weights, tabs["col_scale"], tabs["block_cols"])


def _mla_up_kernel(yb_ref, qn_ref, wqa_ref, wqb_ref, kvn_ref, wk_ref, wv_ref, pa_ref, pb_ref,
                   ct_ref, st_ref, q_ref, k_ref, v_ref):
    scale = (MLA_NOPE + MLA_ROPE) ** -0.5 * LOG2E
    ct = ct_ref[...]
    st = st_ref[...]
    cq = _rms_norm(yb_ref[:, _OFF_CQ:_OFF_CQ + MLA_Q_LORA], qn_ref[...]).astype(MXU_DT)
    q = _dot(cq, wqa_ref[...]) * ct + _dot(cq, wqb_ref[...]) * st
    q_ref[...] = (q * scale).astype(q_ref.dtype)
    ckv = _rms_norm(yb_ref[:, _OFF_CKV:_OFF_CKV + MLA_KV_LORA], kvn_ref[...]).astype(MXU_DT)
    v_ref[...] = _dot(ckv, wv_ref[...]).astype(v_ref.dtype)
    kr = yb_ref[:, _OFF_KR:_OFF_KR + MLA_ROPE]
    kr_hi = kr.astype(MXU_DT)
    kr_lo = (kr - kr_hi.astype(F32)).astype(MXU_DT)
    ka = _dot(kr_hi, pa_ref[...]) + _dot(kr_lo, pa_ref[...])
    kb = _dot(kr_hi, pb_ref[...]) + _dot(kr_lo, pb_ref[...])
    k_ref[...] = (_dot(ckv, wk_ref[...]) + ka * ct + kb * st).astype(k_ref.dtype)


def mla_up(yb, w, tabs, seq, tm=512):
    t = yb.shape[0]
    nps = seq // tm
    row = lambda i: (i, 0)
    fixed = lambda i: (0, 0)
    pos = lambda i: (i % nps, 0)
    wide = MLA_HEADS * LANES
    ins = [w["q_norm"], w["wqa"], w["wqb"], w["kv_norm"], w["wk"], w["wv"], tabs["rope_pa"], tabs["rope_pb"]]
    return pl.pallas_call(
        _mla_up_kernel,
        out_shape=(jax.ShapeDtypeStruct((t, wide), MXU_DT), jax.ShapeDtypeStruct((t, wide), MXU_DT),
                   jax.ShapeDtypeStruct((t, MLA_W), MXU_DT)),
        grid=(t // tm,),
        in_specs=[pl.BlockSpec((tm, _NB), row)] + [pl.BlockSpec(a.shape, fixed) for a in ins]
        + [pl.BlockSpec((tm, wide), pos), pl.BlockSpec((tm, wide), pos)],
        out_specs=(pl.BlockSpec((tm, wide), row), pl.BlockSpec((tm, wide), row),
                   pl.BlockSpec((tm, MLA_W), row)),
        compiler_params=_params("parallel"),
        name="mla_up",
    )(yb, *ins, tabs["rope_cos"], tabs["rope_sin"])


def _bf16_head(x):
    bits = lax.bitcast_convert_type(x, jnp.uint32) & jnp.uint32(0xFFFF0000)
    return lax.bitcast_convert_type(bits, F32)


def _fox_decay_kernel(f_ref, b_ref, place_ref, o_ref):
    z = f_ref[0] + b_ref[...]
    x = jnp.minimum(z, 0.0) - jnp.log1p(jnp.exp(-jnp.abs(z)))
    n = x.shape[0]
    row = lax.broadcasted_iota(jnp.int32, x.shape, 0)
    shift = 1
    while shift < n:
        x = x + jnp.where(row >= shift, pltpu.roll(x, shift, 0), 0.0)
        shift *= 2
    dec = -(x * LOG2E)
    hi = _bf16_head(dec)
    rest = dec - hi
    mid = _bf16_head(rest)
    terms = (hi, mid, rest - mid)
    o_ref[0] = sum(_dot(t.astype(MXU_DT), place_ref[j]) for j, t in enumerate(terms)).astype(o_ref.dtype)


def fox_decay(misc, b_row, place):
    b, s, _ = misc.shape
    wide = place.shape[2]
    return pl.pallas_call(
        _fox_decay_kernel,
        out_shape=jax.ShapeDtypeStruct((b, s, wide), MXU_DT),
        grid=(b,),
        in_specs=[pl.BlockSpec((1, s, LANES), lambda bi: (bi, 0, _COL_MISC)),
                  pl.BlockSpec((1, LANES), lambda bi: (0, 0)),
                  pl.BlockSpec(place.shape, lambda bi: (0, 0, 0))],
        out_specs=pl.BlockSpec((1, s, wide), lambda bi: (bi, 0, 0)),
        compiler_params=_params("parallel"),
        name="fox_decay",
    )(misc, b_row, place)


def _keys_softmax_update(m, acc, s, v_aug):
    m_new = jnp.maximum(m, jnp.max(s, axis=0, keepdims=True))
    a = jnp.exp2(m - m_new)
    p = jnp.exp2(s - m_new).astype(MXU_DT)
    pv = lax.dot_general(v_aug, p, (((0,), (0,)), ((), ())), preferred_element_type=F32)
    return m_new, a * acc + pv


def _keys_softmax_init(queries):
    return jnp.full((1, queries), NEG, F32), jnp.zeros((2 * DH, queries), F32)


def _softmax_piece(s, v_aug):
    m = jnp.max(s, axis=0, keepdims=True)
    p = jnp.exp2(s - m).astype(MXU_DT)
    return m, lax.dot_general(v_aug, p, (((0,), (0,)), ((), ())), preferred_element_type=F32)


def _softmax_merge(state, pieces):
    m_old, acc = state
    m = m_old
    for mp, _ in pieces:
        m = jnp.maximum(m, mp)
    acc = acc * jnp.exp2(m_old - m)
    for mp, ap in pieces:
        acc = acc + ap * jnp.exp2(mp - m)
    return m, acc


def _causal_attn_kernel(*refs, n_heads, dq, decay):
    if decay:
        q_ref, k_ref, kd_ref, v_ref, o_ref = refs
    else:
        q_ref, k_ref, v_ref, o_ref = refs
    qi = pl.program_id(1)
    t = q_ref.shape[1]
    key = lax.broadcasted_iota(jnp.int32, (t, t), 0)
    qry = lax.broadcasted_iota(jnp.int32, (t, t), 1)

    qs = []
    for h in range(n_heads):
        q = q_ref[0, :, h * dq:(h + 1) * dq]
        qs.append(_with_ones(q) if decay else q)

    def scores(rows, h):
        k = k_ref[0, rows, h * LANES:(h + 1) * LANES]
        if decay:
            k = k + kd_ref[0, rows, h * LANES:(h + 1) * LANES]
        return _dot_nt(k, qs[h])

    def values(rows, h):
        return _with_ones(v_ref[0, rows, h * DH:(h + 1) * DH])

    def sweep(rows, state, mask):
        s = [scores(rows, h) for h in range(n_heads)]
        if mask is not None:
            s = [jnp.where(mask, sh, NEG) for sh in s]
        return tuple(_keys_softmax_update(*state[h], s[h], values(rows, h)) for h in range(n_heads))

    n_double = lax.shift_right_logical(qi, 1)
    state = lax.fori_loop(
        0, n_double, lambda p, st: sweep(pl.ds(pl.multiple_of(p * 2 * t, 2 * t), 2 * t), st, None),
        tuple(_keys_softmax_init(t) for _ in range(n_heads)))
    state = lax.fori_loop(
        0, qi & 1, lambda _, st: sweep(pl.ds(pl.multiple_of(n_double * 2 * t, t), t), st, None), state)
    state = sweep(pl.ds(pl.multiple_of(qi * t, t), t), state, key <= qry)
    outs = [acc[:DH] * (1.0 / acc[DH:DH + 1]) for _, acc in state]
    for j in range(n_heads // 2):
        slab = jnp.concatenate(outs[2 * j:2 * j + 2], axis=0).T
        o_ref[0, :, 2 * j * DH:(2 * j + 2) * DH] = slab.astype(o_ref.dtype)


def causal_attention(q, q_col, k, v, v_col, n_heads, dq, k_decay=None, t=ATT_T):
    b, s, _ = q.shape
    whole = pl.BlockSpec((1, s, n_heads * LANES), lambda bi, qi: (bi, 0, 0))
    keys = [k] if k_decay is None else [k, k_decay]
    return pl.pallas_call(
        functools.partial(_causal_attn_kernel, n_heads=n_heads, dq=dq, decay=k_decay is not None),
        out_shape=jax.ShapeDtypeStruct((b, s, n_heads * DH), MXU_DT),
        grid=(b, s // t),
        in_specs=[pl.BlockSpec((1, t, n_heads * dq), lambda bi, qi: (bi, qi, q_col))]
        + [whole] * len(keys)
        + [pl.BlockSpec((1, s, n_heads * DH), lambda bi, qi: (bi, 0, v_col))],
        out_specs=pl.BlockSpec((1, t, n_heads * DH), lambda bi, qi: (bi, qi, 0)),
        compiler_params=_params("parallel", "arbitrary"),
        name="causal_attn_decay" if k_decay is not None else "causal_attn",
    )(q, *keys, v)


def _compress_kernel(x_ref, pe_ref, wlo_ref, whi_ref, w2_ref, o_ref):
    x = x_ref[0]
    nc = x.shape[0]
    lo = _dot((x + pe_ref[0:1, :]).astype(MXU_DT), wlo_ref[...])
    hi = _dot((x + pe_ref[1:2, :]).astype(MXU_DT), whi_ref[...])
    hid = lo + pltpu.roll(hi, nc - 1, 0)
    act = jax.nn.gelu(hid)
    o_ref[0] = _dot(act.astype(MXU_DT), w2_ref[...]).astype(o_ref.dtype)


def compress(x, pe, wlo, whi, w2):
    b, nc, width = x.shape
    fixed = lambda bi: (0, 0)
    return pl.pallas_call(
        _compress_kernel,
        out_shape=jax.ShapeDtypeStruct((b, nc, NSA_KV), MXU_DT),
        grid=(b,),
        in_specs=[pl.BlockSpec((1, nc, width), lambda bi: (bi, 0, 0)),
                  pl.BlockSpec(pe.shape, fixed), pl.BlockSpec(wlo.shape, fixed),
                  pl.BlockSpec(whi.shape, fixed), pl.BlockSpec(w2.shape, fixed)],
        out_specs=pl.BlockSpec((1, nc, NSA_KV), lambda bi: (bi, 0, 0)),
        compiler_params=_params("parallel"),
        name="nsa_compress",
    )(x, pe, wlo, whi, w2)


def _nsa_kernel(q_ref, kc_ref, vc_ref, ks_ref, vs_ref, kw_ref, vw_ref, misc_ref,
                bc_ref, bw_ref, ovt_ref, o_ref, *, n_sel):
    i = pl.program_id(1)
    tq, rq = NSA_TQ, NSA_RQ
    nslc = ovt_ref.shape[0]
    seg = DH + nslc
    nwt = WINDOW // tq + 1
    sub = SEL_TK // tq
    t0 = i * tq
    g_off = _OFF_NG - _COL_MISC * LANES
    gates_t = jax.nn.sigmoid(misc_ref[0]).T

    lane = lax.broadcasted_iota(jnp.int32, (1, rq), 1)
    has_cmp = (t0 + (lane & (tq - 1))) >= (L_CMP - 1)

    blk = lax.broadcasted_iota(jnp.int32, (nslc, tq), 0)
    blk_f = blk.astype(F32)
    tpos = lax.broadcasted_iota(jnp.int32, (nslc, tq), 1) + t0
    cur = lax.shift_right_logical(tpos, int(math.log2(L_SLC)))
    forced = (blk == 0) | (blk == cur) | (blk == cur - 1)
    causal_blk = blk * L_SLC <= tpos
    eye = (lax.broadcasted_iota(jnp.int32, (tq, tq), 0)
           == lax.broadcasted_iota(jnp.int32, (tq, tq), 1)).astype(MXU_DT)

    def window_tile_kind(d):
        static = {0: TILE_DIAG, 1: TILE_PREV, nwt - 1: TILE_EDGE}.get(d, TILE_FAR)
        return jnp.where(i - d < 0, TILE_NONE, static)

    def group_cols(g):
        return slice(g * DH, (g + 1) * DH)

    groups = range(NSA_GROUPS)
    q_groups = [jnp.concatenate([q_ref[0, :, (g * NSA_R + r) * DH:(g * NSA_R + r + 1) * DH]
                                 for r in range(NSA_R)], axis=0) for g in groups]

    def group_lanes(tile_of_head, g):
        return jnp.concatenate([tile_of_head(g * NSA_R + r) for r in range(NSA_R)], axis=1)

    def bias_tile(g, kind):
        return group_lanes(lambda h: bw_ref[h, kind], g)

    s_cmp = [_dot_nt(kc_ref[0, :, group_cols(g)], q_groups[g]) + group_lanes(lambda h: bc_ref[h], g)
             for g in groups]
    win_rows = [pl.ds(pl.multiple_of(jnp.maximum(i - d, 0) * tq, tq), tq) for d in range(nwt - 1, -1, -1)]
    s_win = []
    for g in groups:
        kcat = jnp.concatenate([kw_ref[0, r, group_cols(g)] for r in win_rows], axis=0)
        bias = jnp.concatenate([bias_tile(g, window_tile_kind(d)) for d in range(nwt - 1, -1, -1)], axis=0)
        s_win.append(_dot_nt(kcat, q_groups[g]) + bias)

    def gate_row(g, j):
        return jnp.concatenate(
            [gates_t[g_off + 3 * (g * NSA_R + r) + j:g_off + 3 * (g * NSA_R + r) + j + 1] for r in range(NSA_R)],
            axis=1)

    def weighted_values(v, p):
        return lax.dot_general(_with_ones(v), p, (((0,), (0,)), ((), ())), preferred_element_type=F32)

    o_cmp, imp = [], []
    for g in groups:
        e = jnp.exp2(s_cmp[g] - jnp.max(s_cmp[g], axis=0, keepdims=True)).astype(MXU_DT)
        acc = weighted_values(vc_ref[0, :, group_cols(g)], e)
        inv = jnp.where(has_cmp, 1.0 / acc[DH:DH + 1], 0.0)
        o_cmp.append(acc[:DH] * inv)
        imp4 = _dot(ovt_ref[...], e) * inv
        imp.append(sum(imp4[:, r * tq:(r + 1) * tq] for r in range(1, NSA_R)) + imp4[:, 0:tq])

    o_fixed = []
    for g in groups:
        vcat = jnp.concatenate([vw_ref[0, r, group_cols(g)] for r in win_rows], axis=0)
        p = jnp.exp2(s_win[g] - jnp.max(s_win[g], axis=0, keepdims=True)).astype(MXU_DT)
        acc = weighted_values(vcat, p)
        o_win = acc[:DH] * (1.0 / acc[DH:DH + 1])
        o_fixed.append(gate_row(g, 0) * o_cmp[g] + gate_row(g, 2) * o_win)

    q_aug = []
    for g in groups:
        x = jnp.where(causal_blk, imp[g] + jnp.where(forced, FORCE, 0.0), NEG)
        sel_t = jnp.zeros((nslc, tq), F32)
        for _ in range(n_sel):
            mx = jnp.max(x, axis=0, keepdims=True)
            first = jnp.min(jnp.where(x == mx, blk_f, float(nslc)), axis=0, keepdims=True)
            hit = blk_f == first
            sel_t = jnp.where(hit, 1.0, sel_t)
            x = jnp.where(hit, -3e38, x)
        penalty = ((1.0 - _dot_nt(eye, sel_t.astype(MXU_DT))) * NEG).astype(MXU_DT)
        q_aug.append(jnp.concatenate([q_groups[g], jnp.concatenate([penalty] * NSA_R, axis=0)], axis=1))

    def sel_scores(rows, g):
        return _dot_nt(ks_ref[0, rows, g * seg:(g + 1) * seg], q_aug[g])

    def sel_values(rows, g):
        return _with_ones(vs_ref[0, rows, group_cols(g)])

    def far_sweep(start, n_tiles, state):
        tiles = [pl.ds(pl.multiple_of(start + j * SEL_TK, SEL_TK), SEL_TK) for j in range(n_tiles)]
        s = [[sel_scores(rows, g) for rows in tiles] for g in groups]
        pieces = [[_softmax_piece(s[g][j], sel_values(rows, g)) for j, rows in enumerate(tiles)]
                  for g in groups]
        return tuple(_softmax_merge(state[g], pieces[g]) for g in groups)

    last = i // sub
    n_far = jnp.maximum(last - 1, 0)
    n_double = lax.shift_right_logical(n_far, 1)
    state = lax.fori_loop(0, n_double, lambda p, st: far_sweep(p * 2 * SEL_TK, 2, st),
                          tuple(_keys_softmax_init(rq) for _ in groups))
    state = lax.fori_loop(0, n_far & 1, lambda _, st: far_sweep(n_double * 2 * SEL_TK, 1, st), state)
    near = []
    for kn in (last - 1, last):
        kt = pl.ds(pl.multiple_of(jnp.maximum(kn, 0) * SEL_TK, SEL_TK), SEL_TK)
        kinds = []
        for j in range(sub):
            d = i - (kn * sub + j)
            kind = jnp.where(d < 0, TILE_NONE, jnp.where(d == 0, TILE_SEL_DIAG,
                                                         jnp.where(d == 1, TILE_SEL_PREV, TILE_ZERO)))
            kinds.append(jnp.where(kn < 0, TILE_NONE, kind))
        for g in groups:
            bias = jnp.concatenate([bias_tile(g, kd) for kd in kinds], axis=0)
            near.append((kt, g, sel_scores(kt, g) + bias))
    pieces = [[] for _ in groups]
    for kt, g, s in near:
        pieces[g].append(_softmax_piece(s, sel_values(kt, g)))
    state = [_softmax_merge(state[g], pieces[g]) for g in groups]

    out_t = []
    for g in groups:
        acc = state[g][1]
        out_t.append(o_fixed[g] + gate_row(g, 1) * (acc[:DH] * (1.0 / acc[DH:DH + 1])))
    out_t = jnp.concatenate(out_t, axis=0)
    for r in range(NSA_R):
        slab = out_t[:, r * tq:(r + 1) * tq].T
        for g in groups:
            h = g * NSA_R + r
            o_ref[0, :, h * DH:(h + 1) * DH] = slab[:, g * DH:(g + 1) * DH].astype(o_ref.dtype)


def nsa_attention(ya, yb, kc, vc, ks_aug, bias_c, bias_w, overlap_t):
    b, s, _ = ya.shape
    tq = NSA_TQ
    ncp = kc.shape[1]
    per_b_kv = lambda col: pl.BlockSpec((1, s, NSA_KV), lambda bi, i: (bi, 0, col))
    per_b_cmp = pl.BlockSpec((1, ncp, NSA_KV), lambda bi, i: (bi, 0, 0))
    return pl.pallas_call(
        functools.partial(_nsa_kernel, n_sel=min(N_SEL, overlap_t.shape[0])),
        out_shape=jax.ShapeDtypeStruct((b, s, NSA_W), MXU_DT),
        grid=(b, s // tq),
        in_specs=[pl.BlockSpec((1, tq, NSA_W), lambda bi, i: (bi, i, 0)),
                  per_b_cmp, per_b_cmp,
                  pl.BlockSpec((1, s, ks_aug.shape[2]), lambda bi, i: (bi, 0, 0)),
                  per_b_kv(_COL_VS), per_b_kv(_COL_VS + 1), per_b_kv(_COL_VS + 2),
                  pl.BlockSpec((1, tq, LANES), lambda bi, i: (bi, i, _COL_MISC)),
                  pl.BlockSpec((NSA_HEADS, ncp, tq), lambda bi, i: (0, 0, i)),
                  pl.BlockSpec(bias_w.shape, lambda bi, i: (0, 0, 0, 0)),
                  pl.BlockSpec(overlap_t.shape, lambda bi, i: (0, 0))],
        out_specs=pl.BlockSpec((1, tq, NSA_W), lambda bi, i: (bi, i, 0)),
        compiler_params=_params("parallel", "arbitrary"),
        name="nsa_attn",
    )(ya, kc, vc, ks_aug, ya, ya, ya, yb, bias_c, bias_w, overlap_t)


def _mix_kernel(x_ref, on_ref, om_ref, of_ref, wg_ref, wn_ref, wm_ref, wf_ref, wo_ref,
                g_ref, b_ref, o_ref, *, alpha):
    x = x_ref[...]
    xb = x.astype(MXU_DT)
    d = x.shape[1]
    merged = (jax.nn.sigmoid(_dot(xb, wg_ref[:, 0:d])) * _dot(on_ref[...], wn_ref[...])
              + jax.nn.sigmoid(_dot(xb, wg_ref[:, d:2 * d])) * _dot(om_ref[...], wm_ref[...])
              + jax.nn.sigmoid(_dot(xb, wg_ref[:, 2 * d:3 * d])) * _dot(of_ref[...], wf_ref[...]))
    y = _dot(merged.astype(MXU_DT), wo_ref[...])
    o_ref[...] = _layer_norm(alpha * x + y, g_ref[...], b_ref[...])


def mixer_out(x2, o_nsa, o_mla, o_fox, wg, wn, wm, wf, wo, ln_g, ln_b, alpha, tm=512):
    t, d = x2.shape
    row = lambda i: (i, 0)
    fixed = lambda i: (0, 0)
    resident = lambda a: pl.BlockSpec(a.shape, fixed, pipeline_mode=pl.Buffered(1))
    return pl.pallas_call(
        functools.partial(_mix_kernel, alpha=alpha),
        out_shape=jax.ShapeDtypeStruct((t, d), F32),
        grid=(t // tm,),
        in_specs=[pl.BlockSpec((tm, d), row),
                  pl.BlockSpec((tm, NSA_W), row),
                  pl.BlockSpec((tm, MLA_W), row),
                  pl.BlockSpec((tm, FOX_W), row),
                  resident(wg), resident(wn), resident(wm), resident(wf), resident(wo),
                  pl.BlockSpec((1, d), fixed), pl.BlockSpec((1, d), fixed)],
        out_specs=pl.BlockSpec((tm, d), row),
        compiler_params=_params("parallel"),
        name="mixer_out",
    )(x2, o_nsa, o_mla, o_fox, wg, wn, wm, wf, wo, ln_g, ln_b)


def _matmul_kernel(x_ref, w_ref, o_ref):
    o_ref[...] = _dot(x_ref[...].astype(MXU_DT), w_ref[...]).astype(o_ref.dtype)


def matmul(x2, w, out_dtype, tm):
    t, d = x2.shape
    n = w.shape[1]
    return pl.pallas_call(
        _matmul_kernel,
        out_shape=jax.ShapeDtypeStruct((t, n), out_dtype),
        grid=(t // tm,),
        in_specs=[pl.BlockSpec((tm, d), lambda i: (i, 0)), pl.BlockSpec((d, n), lambda i: (0, 0))],
        out_specs=pl.BlockSpec((tm, n), lambda i: (i, 0)),
        compiler_params=_params("parallel"),
        name="mem_kv_proj",
    )(x2, w)


def _xattn_kernel(x_ref, wq_ref, kv_ref, wo_ref, g_ref, b_ref, o_ref, *, alpha):
    x = x_ref[0]
    q = (_dot(x.astype(MXU_DT), wq_ref[...]) * LOG2E).astype(MXU_DT)
    s = [_dot_nt(kv_ref[0, :, h * DH:(h + 1) * DH], q[:, h * DH:(h + 1) * DH]) for h in range(XA_HEADS)]
    outs = []
    for h in range(XA_HEADS):
        v = kv_ref[0, :, XA_W + h * DH:XA_W + (h + 1) * DH]
        p = jnp.exp2(s[h] - jnp.max(s[h], axis=0, keepdims=True)).astype(MXU_DT)
        acc = lax.dot_general(_with_ones(v), p, (((0,), (0,)), ((), ())), preferred_element_type=F32)
        outs.append(acc[:DH] * (1.0 / acc[DH:DH + 1]))
    o = jnp.concatenate(outs, axis=0).T.astype(MXU_DT)
    y = _dot(o, wo_ref[...])
    o_ref[0] = _layer_norm(alpha * x + y, g_ref[...], b_ref[...])


def cross_attention(x, kv, wq, wo, ln_g, ln_b, alpha, tm=512):
    b, s, d = x.shape
    m = kv.shape[1]
    fixed = lambda bi, i: (0, 0)
    return pl.pallas_call(
        functools.partial(_xattn_kernel, alpha=alpha),
        out_shape=jax.ShapeDtypeStruct((b, s, d), F32),
        grid=(b, s // tm),
        in_specs=[pl.BlockSpec((1, tm, d), lambda bi, i: (bi, i, 0)),
                  pl.BlockSpec(wq.shape, fixed),
                  pl.BlockSpec((1, m, 2 * XA_W), lambda bi, i: (bi, 0, 0)),
                  pl.BlockSpec(wo.shape, fixed),
                  pl.BlockSpec((1, d), fixed), pl.BlockSpec((1, d), fixed)],
        out_specs=pl.BlockSpec((1, tm, d), lambda bi, i: (bi, i, 0)),
        compiler_params=_params("parallel", "parallel"),
        name="cross_attn",
    )(x, wq, kv, wo, ln_g, ln_b)


def _mlp_kernel(x_ref, wu_ref, wd_ref, g_ref, b_ref, o_ref, *, alpha):
    x = x_ref[...]
    hdn = jnp.square(jnp.maximum(_dot(x.astype(MXU_DT), wu_ref[...]), 0.0))
    y = _dot(hdn.astype(MXU_DT), wd_ref[...])
    o_ref[...] = _layer_norm(alpha * x + y, g_ref[...], b_ref[...])


def mlp(x2, wu, wd, ln_g, ln_b, alpha, tm=512):
    t, d = x2.shape
    row = lambda i: (i, 0)
    fixed = lambda i: (0, 0)
    resident = lambda a: pl.BlockSpec(a.shape, fixed, pipeline_mode=pl.Buffered(1))
    return pl.pallas_call(
        functools.partial(_mlp_kernel, alpha=alpha),
        out_shape=jax.ShapeDtypeStruct((t, d), F32),
        grid=(t // tm,),
        in_specs=[pl.BlockSpec((tm, d), row), resident(wu), resident(wd),
                  pl.BlockSpec((1, d), fixed), pl.BlockSpec((1, d), fixed)],
        out_specs=pl.BlockSpec((tm, d), row),
        compiler_params=_params("parallel"),
        name="mlp",
    )(x2, wu, wd, ln_g, ln_b)


def _segment_cols(w, cols_per_head, head_offset, width=LANES):
    out = jnp.zeros(w.shape[:-1] + (len(cols_per_head) * width,), w.dtype)
    for h, cols in enumerate(cols_per_head):
        start = h * width + head_offset
        out = out.at[..., start:start + len(cols)].set(w[..., np.asarray(cols)])
    return out


def _prep_layer_weights(seq, w_in, cmp_pe, cmp_w1, cmp_w2, mla_q_norm, mla_w_uq, mla_kv_norm, mla_w_ukv,
                        fox_b_f, w_gate, w_br_nsa, w_br_mla, w_br_fox, w_mix_out, xa_w_q, xa_w_kv,
                        xa_w_o, mlp_w_up, mlp_w_down, ln_g, ln_b):
    depth, d, _ = w_in.shape
    sizes = (NSA_W, NSA_KV, NSA_KV, NSA_KV, NSA_KV, NSA_KV, NSA_KV, 3 * NSA_HEADS,
             MLA_Q_LORA, MLA_KV_LORA, MLA_ROPE, FOX_W, FOX_W, FOX_W, FOX_HEADS)
    offs = np.concatenate([[0], np.cumsum(sizes)])
    col = lambda n: w_in[:, :, offs[n]:offs[n + 1]]
    (nq, nkc, nvc, nks, nvs, nkw, nvw, ngate, cq, ckv, kr, fq, fk, fv, ff) = [col(n) for n in range(15)]
    qk_scale = DH ** -0.5
    wa = jnp.concatenate([nq * qk_scale, fq * qk_scale, fv, nvs, nkw, nvw], axis=2)
    pad = jnp.zeros((depth, d, _NB - (_OFF_FF + FOX_HEADS)), F32)
    wb = jnp.concatenate([cq, ckv, kr, ngate, ff, pad], axis=2)
    wc = jnp.concatenate([nkc, nvc], axis=2)
    wks = _segment_cols(nks, [np.arange(g * DH, (g + 1) * DH) for g in range(NSA_GROUPS)], 0,
                        DH + seq // L_SLC)
    wfk = _segment_cols(fk, [np.arange(h * DH, (h + 1) * DH) for h in range(FOX_HEADS)], 0)
    ff_lane = _OFF_FF - _COL_MISC * LANES
    b_row = jnp.zeros((depth, 1, LANES), F32).at[:, 0, ff_lane:ff_lane + FOX_HEADS].set(fox_b_f)

    dq = MLA_NOPE + MLA_ROPE
    half = MLA_ROPE // 2
    heads = range(MLA_HEADS)
    nope = [np.arange(h * dq, h * dq + MLA_NOPE) for h in heads]
    x1 = [np.arange(h * dq + MLA_NOPE, h * dq + MLA_NOPE + half) for h in heads]
    x2 = [c + half for c in x1]
    wqa = (_segment_cols(mla_w_uq, nope, 0) + _segment_cols(mla_w_uq, x1, MLA_NOPE)
           + _segment_cols(mla_w_uq, x2, MLA_NOPE + half))
    wqb = _segment_cols(mla_w_uq, x2, MLA_NOPE) + _segment_cols(mla_w_uq, x1, MLA_NOPE + half)
    dkv = MLA_NOPE + MLA_V
    wk = _segment_cols(mla_w_ukv, [np.arange(h * dkv, h * dkv + MLA_NOPE) for h in heads], 0)
    wv = mla_w_ukv[:, :, np.concatenate([np.arange(h * dkv + MLA_NOPE, (h + 1) * dkv) for h in heads])]

    eye_g = jnp.eye(NSA_GROUPS, dtype=F32)
    w1r = cmp_w1.reshape(depth, 2, L_CMP, DH, CMP_HIDDEN)
    def chunk_weights(w1_half):
        blk = jnp.einsum('ealdj,gh->ealgdhj', w1_half, eye_g)
        return blk.reshape(depth, 2, D_CMP * NSA_KV, NSA_GROUPS * CMP_HIDDEN)
    w1lo = chunk_weights(w1r[:, :, :D_CMP])
    w1hi = chunk_weights(w1r[:, :, D_CMP:])
    w2b = jnp.einsum('eajd,gh->eagjhd', cmp_w2, eye_g).reshape(depth, 2, NSA_GROUPS * CMP_HIDDEN, NSA_KV)
    pe = jnp.broadcast_to(cmp_pe.reshape(depth, 2, 2, D_CMP, 1, DH),
                          (depth, 2, 2, D_CMP, NSA_GROUPS, DH)).reshape(depth, 2, 2, D_CMP * NSA_KV)

    c = lambda a: a.astype(MXU_DT)
    return dict(
        wa=c(wa), wb=c(wb), wc=c(wc), wks=c(wks), wfk=c(wfk),
        pe=pe, w1lo=c(w1lo), w1hi=c(w1hi), w2b=c(w2b),
        q_norm=mla_q_norm[:, None, :], wqa=c(wqa), wqb=c(wqb),
        kv_norm=mla_kv_norm[:, None, :], wk=c(wk), wv=c(wv),
        b_row=b_row, wg=c(w_gate), wn=c(w_br_nsa), wm=c(w_br_mla), wf=c(w_br_fox), wo=c(w_mix_out),
        xq=c(xa_w_q * qk_scale), xkv=c(xa_w_kv), xo=c(xa_w_o),
        wu=c(mlp_w_up), wd=c(mlp_w_down),
        ln_g=ln_g[:, :, None, :], ln_b=ln_b[:, :, None, :],
    )


def _layer(x, mem2, w, tabs, alpha):
    b, s, d = x.shape
    t = b * s
    x2 = x.reshape(t, d)

    ya, yb, ykc, yvc, yks, yfk = in_proj(x2, w, tabs, s)
    ya3 = ya.reshape(b, s, _NA)
    yb3 = yb.reshape(b, s, _NB)

    nc = s // D_CMP
    kc = compress(ykc.reshape(b, nc, D_CMP * NSA_KV), w["pe"][0], w["w1lo"][0], w["w1hi"][0], w["w2b"][0])
    vc = compress(yvc.reshape(b, nc, D_CMP * NSA_KV), w["pe"][1], w["w1lo"][1], w["w1hi"][1], w["w2b"][1])
    o_nsa = nsa_attention(ya3, yb3, kc, vc, yks.reshape(b, s, -1),
                          tabs["bias_c"], tabs["bias_w"], tabs["overlap_t"])

    q_mla, k_mla, v_mla = mla_up(yb, w, tabs, s)
    wide = MLA_HEADS * LANES
    o_mla = causal_attention(q_mla.reshape(b, s, wide), 0, k_mla.reshape(b, s, wide),
                             v_mla.reshape(b, s, MLA_W), 0, MLA_HEADS, LANES)

    k_decay = fox_decay(yb3, w["b_row"], tabs["decay_place"])
    o_fox = causal_attention(ya3, _COL_FQ, yfk.reshape(b, s, FOX_HEADS * LANES), ya3, _COL_FQ + 1,
                             FOX_HEADS, DH, k_decay=k_decay)

    x2 = mixer_out(x2, o_nsa.reshape(t, NSA_W), o_mla.reshape(t, MLA_W), o_fox.reshape(t, FOX_W),
                   w["wg"], w["wn"], w["wm"], w["wf"], w["wo"], w["ln_g"][0], w["ln_b"][0], alpha)

    m = mem2.shape[0] // b
    kv = matmul(mem2, w["xkv"], MXU_DT, tm=min(512, mem2.shape[0])).reshape(b, m, 2 * XA_W)
    x3 = cross_attention(x2.reshape(b, s, d), kv, w["xq"], w["xo"], w["ln_g"][1], w["ln_b"][1], alpha)

    x2 = mlp(x3.reshape(t, d), w["wu"], w["wd"], w["ln_g"][2], w["ln_b"][2], alpha)
    return x2.reshape(b, s, d)


def _tables(t5_table, s):
    tq = NSA_TQ
    ki = jnp.arange(tq)[:, None]
    qi = jnp.arange(tq)[None, :]
    last = jnp.full((tq, tq), T5_BUCKETS - 1, jnp.int32)
    masked = jnp.full((tq, tq), -1, jnp.int32)
    kinds = [None] * N_TILE_KINDS
    kinds[TILE_DIAG] = jnp.where(qi >= ki, _t5_bucket(qi - ki), -1)
    kinds[TILE_PREV] = _t5_bucket(tq + qi - ki)
    kinds[TILE_FAR] = last
    kinds[TILE_EDGE] = jnp.where(qi < ki, last, masked)
    kinds[TILE_NONE] = masked
    kinds[TILE_SEL_DIAG] = kinds[TILE_DIAG]
    kinds[TILE_SEL_PREV] = kinds[TILE_PREV]
    kinds[TILE_ZERO] = last
    relative = [int(k in (TILE_SEL_DIAG, TILE_SEL_PREV, TILE_ZERO)) for k in range(N_TILE_KINDS)]
    idx_w = jnp.concatenate(kinds, axis=0).astype(jnp.int32)
    rel_w = jnp.concatenate([jnp.full((tq, tq), r, jnp.int32) for r in relative], axis=0)
    bias_w = bias_table(t5_table, idx_w, rel_w, tq).reshape(NSA_HEADS, N_TILE_KINDS, tq, tq)

    nc = s // D_CMP
    cmp_end = jnp.arange(nc) * D_CMP + (L_CMP - 1)
    dist_c = jnp.arange(s)[None, :] - cmp_end[:, None]
    idx_c = jnp.where(dist_c >= 0, _t5_bucket(dist_c), -1).astype(jnp.int32)
    bias_c = bias_table(t5_table, idx_c, jnp.zeros_like(idx_c), min(64, nc))

    n_slc = s // L_SLC
    block_onehot = (np.arange(s)[:, None] // L_SLC == np.arange(n_slc)[None, :]).astype(np.float32)
    block_cols = np.concatenate([np.zeros((s, DH), np.float32), block_onehot] * NSA_GROUPS, axis=1)
    ff_lane = _OFF_FF - _COL_MISC * LANES
    decay_place = np.zeros((3, LANES, FOX_HEADS * LANES), np.float32)
    for j in range(3):
        for h in range(FOX_HEADS):
            decay_place[j, ff_lane + h, h * LANES + DH + j] = 1.0
    c_lo = np.arange(nc)[:, None] * D_CMP
    s_lo = np.arange(n_slc)[None, :] * L_SLC
    overlap = np.maximum(np.minimum(c_lo + L_CMP, s_lo + L_SLC) - np.maximum(c_lo, s_lo), 0) / D_CMP
    overlap[nc - 1, :] = 0.0

    half = MLA_ROPE // 2
    inv = ROPE_THETA ** (-jnp.arange(half, dtype=F32) / half)
    ang = jnp.arange(s).astype(F32)[:, None] * inv[None, :]
    cos, sin = jnp.cos(ang), jnp.sin(ang)
    tail = jnp.zeros((s, LANES - MLA_NOPE - MLA_ROPE), F32)
    seg_c = jnp.concatenate([jnp.ones((s, MLA_NOPE), F32), cos, cos, tail], axis=1)
    seg_s = jnp.concatenate([jnp.zeros((s, MLA_NOPE), F32), -sin, sin, tail], axis=1)
    pa = np.zeros((MLA_ROPE, MLA_HEADS * LANES), np.float32)
    pb = np.zeros((MLA_ROPE, MLA_HEADS * LANES), np.float32)
    for h in range(MLA_HEADS):
        for j in range(MLA_ROPE):
            pa[j, h * LANES + MLA_NOPE + j] = 1.0
            pb[j, h * LANES + MLA_NOPE + (j + half) % MLA_ROPE] = 1.0

    col_scale = np.ones((1, _NA), np.float32)
    col_scale[0, :NSA_W] = LOG2E
    col_scale[0, _COL_FQ * FOX_W:(_COL_FQ + 1) * FOX_W] = LOG2E
    return dict(bias_w=bias_w, bias_c=bias_c, overlap_t=jnp.asarray(overlap.T, MXU_DT),
                block_cols=jnp.asarray(block_cols), decay_place=jnp.asarray(decay_place, MXU_DT),
                rope_cos=jnp.tile(seg_c, (1, MLA_HEADS)), rope_sin=jnp.tile(seg_s, (1, MLA_HEADS)),
                rope_pa=jnp.asarray(pa, MXU_DT), rope_pb=jnp.asarray(pb, MXU_DT),
                col_scale=jnp.asarray(col_scale))


def kernel(x, mem, w_in, cmp_pe, cmp_w1, cmp_w2, t5_table, mla_q_norm, mla_w_uq, mla_kv_norm, mla_w_ukv, fox_b_f, w_gate, w_br_nsa, w_br_mla, w_br_fox, w_mix_out, xa_w_q, xa_w_kv, xa_w_o, mlp_w_up, mlp_w_down, ln_g, ln_b):
    depth = w_in.shape[0]
    b, s, d = x.shape
    alpha = (2 * depth) ** 0.25
    weights = _prep_layer_weights(s, w_in, cmp_pe, cmp_w1, cmp_w2, mla_q_norm, mla_w_uq, mla_kv_norm,
                                  mla_w_ukv, fox_b_f, w_gate, w_br_nsa, w_br_mla, w_br_fox, w_mix_out,
                                  xa_w_q, xa_w_kv, xa_w_o, mlp_w_up, mlp_w_down, ln_g, ln_b)
    tabs = _tables(t5_table, s)
    mem2 = mem.reshape(-1, d)

    def step(xc, w):
        return _layer(xc, mem2, w, tabs, alpha), None

    out, _ = lax.scan(step, x, weights)
    return out
```

```python
import functools
import math

import numpy as np
import jax
import jax.numpy as jnp
from jax import lax
from jax.experimental import pallas as pl
from jax.experimental.pallas import tpu as pltpu

D_MODEL = 1024
DH = 64
NSA_HEADS = 8
NSA_GROUPS = 2
NSA_R = NSA_HEADS // NSA_GROUPS
L_CMP = 32
D_CMP = 16
CMP_HIDDEN = 128
L_SLC = 64
N_SEL = 8
WINDOW = 512
MLA_HEADS = 4
MLA_NOPE = 64
MLA_ROPE = 32
MLA_V = 64
MLA_Q_LORA = 384
MLA_KV_LORA = 128
ROPE_THETA = 10000.0
FOX_HEADS = 4
XA_HEADS = 4
D_FF = 4 * D_MODEL
T5_BUCKETS = 32
T5_MAX_DIST = 128
NSA_W = NSA_HEADS * DH
NSA_KV = NSA_GROUPS * DH
MLA_W = MLA_HEADS * MLA_V
FOX_W = FOX_HEADS * DH
XA_W = XA_HEADS * DH
LN_EPS = 1e-5
RMS_EPS = 1e-6
NEG = -1e30
FORCE = 1e4
LOG2E = math.log2(math.e)

F32 = jnp.float32
MXU_DT = jnp.bfloat16
LANES = 128
NSA_TQ = 128
NSA_RQ = NSA_R * NSA_TQ
SEL_TK = 512
ATT_T = 512
VMEM_LIMIT = 48 * 1024 * 1024

TILE_DIAG = 0
TILE_PREV = 1
TILE_FAR = 2
TILE_EDGE = 3
TILE_NONE = 4
TILE_SEL_DIAG = 5
TILE_SEL_PREV = 6
TILE_ZERO = 7
N_TILE_KINDS = 8

_NA = NSA_W + 2 * FOX_W + 3 * NSA_KV
_COL_FQ = NSA_W // FOX_W
_COL_VS = (NSA_W + 2 * FOX_W) // NSA_KV
_OFF_CQ = 0
_OFF_CKV = MLA_Q_LORA
_OFF_KR = _OFF_CKV + MLA_KV_LORA
_OFF_NG = _OFF_KR + MLA_ROPE
_OFF_FF = _OFF_NG + 3 * NSA_HEADS
_NB = 640
_COL_MISC = _OFF_KR // LANES


def _dot(a, b):
    return jnp.dot(a, b, preferred_element_type=F32)


def _dot_nt(a, b):
    return lax.dot_general(a, b, (((1,), (1,)), ((), ())), preferred_element_type=F32)


def _params(*sem):
    return pltpu.CompilerParams(dimension_semantics=sem, vmem_limit_bytes=VMEM_LIMIT)


def _layer_norm(z, g, b):
    mu = jnp.mean(z, axis=-1, keepdims=True)
    zc = z - mu
    var = jnp.mean(zc * zc, axis=-1, keepdims=True)
    return zc * lax.rsqrt(var + LN_EPS) * g + b


def _rms_norm(z, g):
    return z * lax.rsqrt(jnp.mean(z * z, axis=-1, keepdims=True) + RMS_EPS) * g


def _t5_bucket(dist):
    n = jnp.maximum(dist, 0)
    max_exact = T5_BUCKETS // 2
    nf = jnp.maximum(n, 1).astype(F32)
    large = max_exact + (jnp.log(nf / max_exact) / math.log(T5_MAX_DIST / max_exact)
                         * (T5_BUCKETS - max_exact)).astype(jnp.int32)
    large = jnp.minimum(large, T5_BUCKETS - 1)
    return jnp.where(n < max_exact, n, large)


SUM_ROWS = 16
ACC_ROWS = DH + SUM_ROWS


def _with_ones(v):
    return jnp.concatenate([v, jnp.ones(v.shape, v.dtype)], axis=1)


def _with_sums(v):
    return jnp.concatenate([v, jnp.ones((v.shape[0], SUM_ROWS), v.dtype)], axis=1)


def _bias_table_kernel(t5_ref, idx_ref, rel_ref, o_ref):
    h = pl.program_id(0)
    idx = idx_ref[...]
    acc = jnp.full(idx.shape, NEG, F32)
    for b in range(T5_BUCKETS):
        acc = jnp.where(idx == b, t5_ref[b, h] * LOG2E, acc)
    o_ref[0] = acc - jnp.where(rel_ref[...] > 0, t5_ref[T5_BUCKETS - 1, h] * LOG2E, 0.0)


def bias_table(t5_table, idx, rel, tr):
    rows, cols = idx.shape
    return pl.pallas_call(
        _bias_table_kernel,
        out_shape=jax.ShapeDtypeStruct((NSA_HEADS, rows, cols), F32),
        grid=(NSA_HEADS, rows // tr),
        in_specs=[pl.BlockSpec(memory_space=pltpu.SMEM),
                  pl.BlockSpec((tr, cols), lambda h, i: (i, 0)),
                  pl.BlockSpec((tr, cols), lambda h, i: (i, 0))],
        out_specs=pl.BlockSpec((1, tr, cols), lambda h, i: (h, i, 0)),
        compiler_params=_params("parallel", "parallel"),
        name="bias_table",
    )(t5_table, idx, rel)


def _in_proj_kernel(x_ref, wa_ref, wb_ref, wc_ref, wks_ref, wfk_ref, sc_ref, blk_ref,
                    ya_ref, yb_ref, ykc_ref, yvc_ref, yks_ref, yfk_ref):
    xb = x_ref[...].astype(MXU_DT)
    ya_ref[...] = (_dot(xb, wa_ref[...]) * sc_ref[...]).astype(ya_ref.dtype)
    yb_ref[...] = _dot(xb, wb_ref[...])
    yc = _dot(xb, wc_ref[...])
    ykc_ref[...] = yc[:, :NSA_KV]
    yvc_ref[...] = yc[:, NSA_KV:]
    yks_ref[...] = (_dot(xb, wks_ref[...]) + blk_ref[...]).astype(yks_ref.dtype)
    yfk_ref[...] = _dot(xb, wfk_ref[...]).astype(yfk_ref.dtype)


def in_proj(x2, w, tabs, seq, tm=512):
    t, d = x2.shape
    nps = seq // tm
    row = lambda i: (i, 0)
    fixed = lambda i: (0, 0)
    weights = [w["wa"], w["wb"], w["wc"], w["wks"], w["wfk"]]
    widths = [(_NA, MXU_DT), (_NB, F32), (NSA_KV, F32), (NSA_KV, F32),
              (w["wks"].shape[1], MXU_DT), (w["wfk"].shape[1], MXU_DT)]
    return pl.pallas_call(
        _in_proj_kernel,
        out_shape=tuple(jax.ShapeDtypeStruct((t, n), dt) for n, dt in widths),
        grid=(t // tm,),
        in_specs=[pl.BlockSpec((tm, d), row)] + [pl.BlockSpec(a.shape, fixed) for a in weights]
        + [pl.BlockSpec((1, _NA), fixed),
           pl.BlockSpec((tm, tabs["block_cols"].shape[1]), lambda i: (i % nps, 0))],
        out_specs=tuple(pl.BlockSpec((tm, n), row) for n, _ in widths),
        compiler_params=_params("parallel"),
        name="in_proj",
    )(x2, *weights, tabs["col_scale"], tabs["block_cols"])


def _mla_up_kernel(yb_ref, qn_ref, wqa_ref, wqb_ref, kvn_ref, wka_ref, wkb_ref, wv_ref,
                   ct_ref, st_ref, q_ref, k_ref, v_ref):
    scale = (MLA_NOPE + MLA_ROPE) ** -0.5 * LOG2E
    ct = ct_ref[...]
    st = st_ref[...]
    cq = _rms_norm(yb_ref[:, _OFF_CQ:_OFF_CQ + MLA_Q_LORA], qn_ref[...]).astype(MXU_DT)
    q = _dot(cq, wqa_ref[...]) * ct + _dot(cq, wqb_ref[...]) * st
    q_ref[...] = (q * scale).astype(q_ref.dtype)
    ckv = _rms_norm(yb_ref[:, _OFF_CKV:_OFF_CKV + MLA_KV_LORA], kvn_ref[...]).astype(MXU_DT)
    v_ref[...] = _dot(ckv, wv_ref[...]).astype(v_ref.dtype)
    kr = yb_ref[:, _OFF_KR:_OFF_KR + MLA_ROPE]
    kr_hi = kr.astype(MXU_DT)
    kr_lo = (kr - kr_hi.astype(F32)).astype(MXU_DT)
    lhs = jnp.concatenate([ckv, kr_hi, kr_lo], axis=1)
    k_ref[...] = (_dot(lhs, wka_ref[...]) * ct + _dot(lhs, wkb_ref[...]) * st).astype(k_ref.dtype)


def mla_up(yb, w, tabs, seq, tm=512):
    t = yb.shape[0]
    nps = seq // tm
    row = lambda i: (i, 0)
    fixed = lambda i: (0, 0)
    pos = lambda i: (i % nps, 0)
    wide = MLA_HEADS * LANES
    ins = [w["q_norm"], w["wqa"], w["wqb"], w["kv_norm"], w["wka"], w["wkb"], w["wv"]]
    return pl.pallas_call(
        _mla_up_kernel,
        out_shape=(jax.ShapeDtypeStruct((t, wide), MXU_DT), jax.ShapeDtypeStruct((t, wide), MXU_DT),
                   jax.ShapeDtypeStruct((t, MLA_W), MXU_DT)),
        grid=(t // tm,),
        in_specs=[pl.BlockSpec((tm, _NB), row)] + [pl.BlockSpec(a.shape, fixed) for a in ins]
        + [pl.BlockSpec((tm, wide), pos), pl.BlockSpec((tm, wide), pos)],
        out_specs=(pl.BlockSpec((tm, wide), row), pl.BlockSpec((tm, wide), row),
                   pl.BlockSpec((tm, MLA_W), row)),
        compiler_params=_params("parallel"),
        name="mla_up",
    )(yb, *ins, tabs["rope_cos"], tabs["rope_sin"])


def _bf16_head(x):
    bits = lax.bitcast_convert_type(x, jnp.uint32) & jnp.uint32(0xFFFF0000)
    return lax.bitcast_convert_type(bits, F32)


def _fox_decay_kernel(f_ref, b_ref, place_ref, o_ref):
    z = f_ref[0] + b_ref[...]
    x = jnp.minimum(z, 0.0) - jnp.log1p(jnp.exp(-jnp.abs(z)))
    n = x.shape[0]
    row = lax.broadcasted_iota(jnp.int32, x.shape, 0)
    shift = 1
    while shift < n:
        x = x + jnp.where(row >= shift, pltpu.roll(x, shift, 0), 0.0)
        shift *= 2
    dec = -(x * LOG2E)
    hi = _bf16_head(dec)
    rest = dec - hi
    mid = _bf16_head(rest)
    terms = (hi, mid, rest - mid)
    o_ref[0] = sum(_dot(t.astype(MXU_DT), place_ref[j]) for j, t in enumerate(terms)).astype(o_ref.dtype)


def fox_decay(misc, b_row, place):
    b, s, _ = misc.shape
    wide = place.shape[2]
    return pl.pallas_call(
        _fox_decay_kernel,
        out_shape=jax.ShapeDtypeStruct((b, s, wide), MXU_DT),
        grid=(b,),
        in_specs=[pl.BlockSpec((1, s, LANES), lambda bi: (bi, 0, _COL_MISC)),
                  pl.BlockSpec((1, LANES), lambda bi: (0, 0)),
                  pl.BlockSpec(place.shape, lambda bi: (0, 0, 0))],
        out_specs=pl.BlockSpec((1, s, wide), lambda bi: (bi, 0, 0)),
        compiler_params=_params("parallel"),
        name="fox_decay",
    )(misc, b_row, place)


def _keys_softmax_update(m, acc, s, v_aug):
    m_new = jnp.maximum(m, jnp.max(s, axis=0, keepdims=True))
    a = jnp.exp2(m - m_new)
    p = jnp.exp2(s - m_new).astype(MXU_DT)
    pv = lax.dot_general(v_aug, p, (((0,), (0,)), ((), ())), preferred_element_type=F32)
    return m_new, a * acc + pv


def _keys_softmax_init(queries):
    return jnp.full((1, queries), NEG, F32), jnp.zeros((ACC_ROWS, queries), F32)


def _softmax_piece(s, v_aug):
    m = jnp.max(s, axis=0, keepdims=True)
    p = jnp.exp2(s - m).astype(MXU_DT)
    return m, lax.dot_general(v_aug, p, (((0,), (0,)), ((), ())), preferred_element_type=F32)


def _softmax_merge(state, pieces):
    m_old, acc = state
    m = m_old
    for mp, _ in pieces:
        m = jnp.maximum(m, mp)
    acc = acc * jnp.exp2(m_old - m)
    for mp, ap in pieces:
        acc = acc + ap * jnp.exp2(mp - m)
    return m, acc


def _causal_attn_kernel(*refs, n_heads, dq, decay):
    if decay:
        q_ref, k_ref, kd_ref, v_ref, o_ref = refs
    else:
        q_ref, k_ref, v_ref, o_ref = refs
    qi = pl.program_id(1)
    t = q_ref.shape[1]
    key = lax.broadcasted_iota(jnp.int32, (t, t), 0)
    qry = lax.broadcasted_iota(jnp.int32, (t, t), 1)

    qs = []
    for h in range(n_heads):
        q = q_ref[0, :, h * dq:(h + 1) * dq]
        qs.append(_with_ones(q) if decay else q)

    def scores(rows, h):
        k = k_ref[0, rows, h * LANES:(h + 1) * LANES]
        if decay:
            k = k + kd_ref[0, rows, h * LANES:(h + 1) * LANES]
        return _dot_nt(k, qs[h])

    def values(rows, h):
        return _with_sums(v_ref[0, rows, h * DH:(h + 1) * DH])

    def sweep(rows, state, mask):
        s = [scores(rows, h) for h in range(n_heads)]
        if mask is not None:
            s = [jnp.where(mask, sh, NEG) for sh in s]
        return tuple(_keys_softmax_update(*state[h], s[h], values(rows, h)) for h in range(n_heads))

    n_double = lax.shift_right_logical(qi, 1)
    state = lax.fori_loop(
        0, n_double, lambda p, st: sweep(pl.ds(pl.multiple_of(p * 2 * t, 2 * t), 2 * t), st, None),
        tuple(_keys_softmax_init(t) for _ in range(n_heads)))
    state = lax.fori_loop(
        0, qi & 1, lambda _, st: sweep(pl.ds(pl.multiple_of(n_double * 2 * t, t), t), st, None), state)
    state = sweep(pl.ds(pl.multiple_of(qi * t, t), t), state, key <= qry)
    outs = [acc[:DH] * (1.0 / acc[DH:DH + 1]) for _, acc in state]
    for j in range(n_heads // 2):
        slab = jnp.concatenate(outs[2 * j:2 * j + 2], axis=0).T
        o_ref[0, :, 2 * j * DH:(2 * j + 2) * DH] = slab.astype(o_ref.dtype)


def causal_attention(q, q_col, k, v, v_col, n_heads, dq, k_decay=None, t=ATT_T):
    b, s, _ = q.shape
    whole = pl.BlockSpec((1, s, n_heads * LANES), lambda bi, qi: (bi, 0, 0))
    keys = [k] if k_decay is None else [k, k_decay]
    return pl.pallas_call(
        functools.partial(_causal_attn_kernel, n_heads=n_heads, dq=dq, decay=k_decay is not None),
        out_shape=jax.ShapeDtypeStruct((b, s, n_heads * DH), MXU_DT),
        grid=(b, s // t),
        in_specs=[pl.BlockSpec((1, t, n_heads * dq), lambda bi, qi: (bi, qi, q_col))]
        + [whole] * len(keys)
        + [pl.BlockSpec((1, s, n_heads * DH), lambda bi, qi: (bi, 0, v_col))],
        out_specs=pl.BlockSpec((1, t, n_heads * DH), lambda bi, qi: (bi, qi, 0)),
        compiler_params=_params("parallel", "arbitrary"),
        name="causal_attn_decay" if k_decay is not None else "causal_attn",
    )(q, *keys, v)


def _compress_kernel(x_ref, pe_ref, wlo_ref, whi_ref, w2_ref, o_ref):
    x = x_ref[0]
    nc = x.shape[0]
    lo = _dot((x + pe_ref[0:1, :]).astype(MXU_DT), wlo_ref[...])
    hi = _dot((x + pe_ref[1:2, :]).astype(MXU_DT), whi_ref[...])
    hid = lo + pltpu.roll(hi, nc - 1, 0)
    act = jax.nn.gelu(hid)
    o_ref[0] = _dot(act.astype(MXU_DT), w2_ref[...]).astype(o_ref.dtype)


def compress(x, pe, wlo, whi, w2):
    b, nc, width = x.shape
    fixed = lambda bi: (0, 0)
    return pl.pallas_call(
        _compress_kernel,
        out_shape=jax.ShapeDtypeStruct((b, nc, NSA_KV), MXU_DT),
        grid=(b,),
        in_specs=[pl.BlockSpec((1, nc, width), lambda bi: (bi, 0, 0)),
                  pl.BlockSpec(pe.shape, fixed), pl.BlockSpec(wlo.shape, fixed),
                  pl.BlockSpec(whi.shape, fixed), pl.BlockSpec(w2.shape, fixed)],
        out_specs=pl.BlockSpec((1, nc, NSA_KV), lambda bi: (bi, 0, 0)),
        compiler_params=_params("parallel"),
        name="nsa_compress",
    )(x, pe, wlo, whi, w2)


def _nsa_kernel(q_ref, kc_ref, vc_ref, ks_ref, vs_ref, kw_ref, vw_ref, misc_ref,
                bc_ref, bw_ref, ovt_ref, o_ref, *, n_sel):
    i = pl.program_id(1)
    tq, rq = NSA_TQ, NSA_RQ
    nslc = ovt_ref.shape[0]
    seg = DH + nslc
    nwt = WINDOW // tq + 1
    sub = SEL_TK // tq
    t0 = i * tq
    g_off = _OFF_NG - _COL_MISC * LANES
    gates_t = jax.nn.sigmoid(misc_ref[0]).T

    lane = lax.broadcasted_iota(jnp.int32, (1, rq), 1)
    has_cmp = (t0 + (lane & (tq - 1))) >= (L_CMP - 1)

    blk = lax.broadcasted_iota(jnp.int32, (nslc, tq), 0)
    blk_f = blk.astype(F32)
    tpos = lax.broadcasted_iota(jnp.int32, (nslc, tq), 1) + t0
    cur = lax.shift_right_logical(tpos, int(math.log2(L_SLC)))
    forced = (blk == 0) | (blk == cur) | (blk == cur - 1)
    causal_blk = blk * L_SLC <= tpos
    eye = (lax.broadcasted_iota(jnp.int32, (tq, tq), 0)
           == lax.broadcasted_iota(jnp.int32, (tq, tq), 1)).astype(MXU_DT)

    def window_tile_kind(d):
        static = {0: TILE_DIAG, 1: TILE_PREV, nwt - 1: TILE_EDGE}.get(d, TILE_FAR)
        return jnp.where(i - d < 0, TILE_NONE, static)

    def group_cols(g):
        return slice(g * DH, (g + 1) * DH)

    groups = range(NSA_GROUPS)
    q_groups = [jnp.concatenate([q_ref[0, :, (g * NSA_R + r) * DH:(g * NSA_R + r + 1) * DH]
                                 for r in range(NSA_R)], axis=0) for g in groups]

    def group_lanes(tile_of_head, g):
        return jnp.concatenate([tile_of_head(g * NSA_R + r) for r in range(NSA_R)], axis=1)

    def bias_tile(g, kind):
        return group_lanes(lambda h: bw_ref[h, kind], g)

    s_cmp = [_dot_nt(kc_ref[0, :, group_cols(g)], q_groups[g]) + group_lanes(lambda h: bc_ref[h], g)
             for g in groups]
    win_rows = [pl.ds(pl.multiple_of(jnp.maximum(i - d, 0) * tq, tq), tq) for d in range(nwt - 1, -1, -1)]
    s_win = []
    for g in groups:
        kcat = jnp.concatenate([kw_ref[0, r, group_cols(g)] for r in win_rows], axis=0)
        bias = jnp.concatenate([bias_tile(g, window_tile_kind(d)) for d in range(nwt - 1, -1, -1)], axis=0)
        s_win.append(_dot_nt(kcat, q_groups[g]) + bias)

    def gate_row(g, j):
        return jnp.concatenate(
            [gates_t[g_off + 3 * (g * NSA_R + r) + j:g_off + 3 * (g * NSA_R + r) + j + 1] for r in range(NSA_R)],
            axis=1)

    def weighted_values(v, p):
        return lax.dot_general(_with_sums(v), p, (((0,), (0,)), ((), ())), preferred_element_type=F32)

    o_cmp, imp = [], []
    for g in groups:
        e = jnp.exp2(s_cmp[g] - jnp.max(s_cmp[g], axis=0, keepdims=True)).astype(MXU_DT)
        acc = weighted_values(vc_ref[0, :, group_cols(g)], e)
        inv = jnp.where(has_cmp, 1.0 / acc[DH:DH + 1], 0.0)
        o_cmp.append(acc[:DH] * inv)
        imp4 = _dot(ovt_ref[...], e) * inv
        imp.append(sum(imp4[:, r * tq:(r + 1) * tq] for r in range(1, NSA_R)) + imp4[:, 0:tq])

    o_fixed = []
    for g in groups:
        vcat = jnp.concatenate([vw_ref[0, r, group_cols(g)] for r in win_rows], axis=0)
        p = jnp.exp2(s_win[g] - jnp.max(s_win[g], axis=0, keepdims=True)).astype(MXU_DT)
        acc = weighted_values(vcat, p)
        o_win = acc[:DH] * (1.0 / acc[DH:DH + 1])
        o_fixed.append(gate_row(g, 0) * o_cmp[g] + gate_row(g, 2) * o_win)

    q_aug = []
    for g in groups:
        x = jnp.where(causal_blk, imp[g] + jnp.where(forced, FORCE, 0.0), NEG)
        sel_t = jnp.zeros((nslc, tq), F32)
        for _ in range(n_sel):
            mx = jnp.max(x, axis=0, keepdims=True)
            first = jnp.min(jnp.where(x == mx, blk_f, float(nslc)), axis=0, keepdims=True)
            hit = blk_f == first
            sel_t = jnp.where(hit, 1.0, sel_t)
            x = jnp.where(hit, -3e38, x)
        penalty = ((1.0 - _dot_nt(eye, sel_t.astype(MXU_DT))) * NEG).astype(MXU_DT)
        q_aug.append(jnp.concatenate([q_groups[g], jnp.concatenate([penalty] * NSA_R, axis=0)], axis=1))

    def sel_scores(rows, g):
        return _dot_nt(ks_ref[0, rows, g * seg:(g + 1) * seg], q_aug[g])

    def sel_values(rows, g):
        return _with_sums(vs_ref[0, rows, group_cols(g)])

    def far_sweep(start, n_tiles, state):
        tiles = [pl.ds(pl.multiple_of(start + j * SEL_TK, SEL_TK), SEL_TK) for j in range(n_tiles)]
        s = [[sel_scores(rows, g) for rows in tiles] for g in groups]
        pieces = [[_softmax_piece(s[g][j], sel_values(rows, g)) for j, rows in enumerate(tiles)]
                  for g in groups]
        return tuple(_softmax_merge(state[g], pieces[g]) for g in groups)

    last = i // sub
    n_far = jnp.maximum(last - 1, 0)
    n_double = lax.shift_right_logical(n_far, 1)
    state = lax.fori_loop(0, n_double, lambda p, st: far_sweep(p * 2 * SEL_TK, 2, st),
                          tuple(_keys_softmax_init(rq) for _ in groups))
    state = lax.fori_loop(0, n_far & 1, lambda _, st: far_sweep(n_double * 2 * SEL_TK, 1, st), state)
    near = []
    for kn in (last - 1, last):
        kt = pl.ds(pl.multiple_of(jnp.maximum(kn, 0) * SEL_TK, SEL_TK), SEL_TK)
        kinds = []
        for j in range(sub):
            d = i - (kn * sub + j)
            kind = jnp.where(d < 0, TILE_NONE, jnp.where(d == 0, TILE_SEL_DIAG,
                                                         jnp.where(d == 1, TILE_SEL_PREV, TILE_ZERO)))
            kinds.append(jnp.where(kn < 0, TILE_NONE, kind))
        for g in groups:
            bias = jnp.concatenate([bias_tile(g, kd) for kd in kinds], axis=0)
            near.append((kt, g, sel_scores(kt, g) + bias))
    pieces = [[] for _ in groups]
    for kt, g, s in near:
        pieces[g].append(_softmax_piece(s, sel_values(kt, g)))
    state = [_softmax_merge(state[g], pieces[g]) for g in groups]

    out_t = []
    for g in groups:
        acc = state[g][1]
        out_t.append(o_fixed[g] + gate_row(g, 1) * (acc[:DH] * (1.0 / acc[DH:DH + 1])))
    out_t = jnp.concatenate(out_t, axis=0)
    for r in range(NSA_R):
        slab = out_t[:, r * tq:(r + 1) * tq].T
        for g in groups:
            h = g * NSA_R + r
            o_ref[0, :, h * DH:(h + 1) * DH] = slab[:, g * DH:(g + 1) * DH].astype(o_ref.dtype)


def nsa_attention(ya, yb, kc, vc, ks_aug, bias_c, bias_w, overlap_t):
    b, s, _ = ya.shape
    tq = NSA_TQ
    ncp = kc.shape[1]
    per_b_kv = lambda col: pl.BlockSpec((1, s, NSA_KV), lambda bi, i: (bi, 0, col))
    per_b_cmp = pl.BlockSpec((1, ncp, NSA_KV), lambda bi, i: (bi, 0, 0))
    return pl.pallas_call(
        functools.partial(_nsa_kernel, n_sel=min(N_SEL, overlap_t.shape[0])),
        out_shape=jax.ShapeDtypeStruct((b, s, NSA_W), MXU_DT),
        grid=(b, s // tq),
        in_specs=[pl.BlockSpec((1, tq, NSA_W), lambda bi, i: (bi, i, 0)),
                  per_b_cmp, per_b_cmp,
                  pl.BlockSpec((1, s, ks_aug.shape[2]), lambda bi, i: (bi, 0, 0)),
                  per_b_kv(_COL_VS), per_b_kv(_COL_VS + 1), per_b_kv(_COL_VS + 2),
                  pl.BlockSpec((1, tq, LANES), lambda bi, i: (bi, i, _COL_MISC)),
                  pl.BlockSpec((NSA_HEADS, ncp, tq), lambda bi, i: (0, 0, i)),
                  pl.BlockSpec(bias_w.shape, lambda bi, i: (0, 0, 0, 0)),
                  pl.BlockSpec(overlap_t.shape, lambda bi, i: (0, 0))],
        out_specs=pl.BlockSpec((1, tq, NSA_W), lambda bi, i: (bi, i, 0)),
        compiler_params=_params("parallel", "arbitrary"),
        name="nsa_attn",
    )(ya, kc, vc, ks_aug, ya, ya, ya, yb, bias_c, bias_w, overlap_t)


def _mix_kernel(x_ref, on_ref, om_ref, of_ref, wg_ref, wn_ref, wm_ref, wf_ref, wo_ref,
                g_ref, b_ref, o_ref, *, alpha):
    x = x_ref[...]
    xb = x.astype(MXU_DT)
    d = x.shape[1]
    merged = (jax.nn.sigmoid(_dot(xb, wg_ref[:, 0:d])) * _dot(on_ref[...], wn_ref[...])
              + jax.nn.sigmoid(_dot(xb, wg_ref[:, d:2 * d])) * _dot(om_ref[...], wm_ref[...])
              + jax.nn.sigmoid(_dot(xb, wg_ref[:, 2 * d:3 * d])) * _dot(of_ref[...], wf_ref[...]))
    y = _dot(merged.astype(MXU_DT), wo_ref[...])
    o_ref[...] = _layer_norm(alpha * x + y, g_ref[...], b_ref[...])


def mixer_out(x2, o_nsa, o_mla, o_fox, wg, wn, wm, wf, wo, ln_g, ln_b, alpha, tm=512):
    t, d = x2.shape
    row = lambda i: (i, 0)
    fixed = lambda i: (0, 0)
    resident = lambda a: pl.BlockSpec(a.shape, fixed, pipeline_mode=pl.Buffered(1))
    return pl.pallas_call(
        functools.partial(_mix_kernel, alpha=alpha),
        out_shape=jax.ShapeDtypeStruct((t, d), F32),
        grid=(t // tm,),
        in_specs=[pl.BlockSpec((tm, d), row),
                  pl.BlockSpec((tm, NSA_W), row),
                  pl.BlockSpec((tm, MLA_W), row),
                  pl.BlockSpec((tm, FOX_W), row),
                  resident(wg), resident(wn), resident(wm), resident(wf), resident(wo),
                  pl.BlockSpec((1, d), fixed), pl.BlockSpec((1, d), fixed)],
        out_specs=pl.BlockSpec((tm, d), row),
        compiler_params=_params("parallel"),
        name="mixer_out",
    )(x2, o_nsa, o_mla, o_fox, wg, wn, wm, wf, wo, ln_g, ln_b)


def _matmul_kernel(x_ref, w_ref, o_ref):
    o_ref[...] = _dot(x_ref[...].astype(MXU_DT), w_ref[...]).astype(o_ref.dtype)


def matmul(x2, w, out_dtype, tm):
    t, d = x2.shape
    n = w.shape[1]
    return pl.pallas_call(
        _matmul_kernel,
        out_shape=jax.ShapeDtypeStruct((t, n), out_dtype),
        grid=(t // tm,),
        in_specs=[pl.BlockSpec((tm, d), lambda i: (i, 0)), pl.BlockSpec((d, n), lambda i: (0, 0))],
        out_specs=pl.BlockSpec((tm, n), lambda i: (i, 0)),
        compiler_params=_params("parallel"),
        name="mem_kv_proj",
    )(x2, w)


def _xattn_kernel(x_ref, wq_ref, kv_ref, wo_ref, g_ref, b_ref, o_ref, *, alpha):
    x = x_ref[0]
    q = (_dot(x.astype(MXU_DT), wq_ref[...]) * LOG2E).astype(MXU_DT)
    s = [_dot_nt(kv_ref[0, :, h * DH:(h + 1) * DH], q[:, h * DH:(h + 1) * DH]) for h in range(XA_HEADS)]
    outs = []
    for h in range(XA_HEADS):
        v = kv_ref[0, :, XA_W + h * DH:XA_W + (h + 1) * DH]
        p = jnp.exp2(s[h] - jnp.max(s[h], axis=0, keepdims=True)).astype(MXU_DT)
        acc = lax.dot_general(_with_sums(v), p, (((0,), (0,)), ((), ())), preferred_element_type=F32)
        outs.append(acc[:DH] * (1.0 / acc[DH:DH + 1]))
    o = jnp.concatenate(outs, axis=0).T.astype(MXU_DT)
    y = _dot(o, wo_ref[...])
    o_ref[0] = _layer_norm(alpha * x + y, g_ref[...], b_ref[...])


def cross_attention(x, kv, wq, wo, ln_g, ln_b, alpha, tm=512):
    b, s, d = x.shape
    m = kv.shape[1]
    fixed = lambda bi, i: (0, 0)
    return pl.pallas_call(
        functools.partial(_xattn_kernel, alpha=alpha),
        out_shape=jax.ShapeDtypeStruct((b, s, d), F32),
        grid=(b, s // tm),
        in_specs=[pl.BlockSpec((1, tm, d), lambda bi, i: (bi, i, 0)),
                  pl.BlockSpec(wq.shape, fixed),
                  pl.BlockSpec((1, m, 2 * XA_W), lambda bi, i: (bi, 0, 0)),
                  pl.BlockSpec(wo.shape, fixed),
                  pl.BlockSpec((1, d), fixed), pl.BlockSpec((1, d), fixed)],
        out_specs=pl.BlockSpec((1, tm, d), lambda bi, i: (bi, i, 0)),
        compiler_params=_params("parallel", "parallel"),
        name="cross_attn",
    )(x, wq, kv, wo, ln_g, ln_b)


def _mlp_kernel(x_ref, wu_ref, wd_ref, g_ref, b_ref, o_ref, *, alpha):
    x = x_ref[...]
    hdn = jnp.square(jnp.maximum(_dot(x.astype(MXU_DT), wu_ref[...]), 0.0))
    y = _dot(hdn.astype(MXU_DT), wd_ref[...])
    o_ref[...] = _layer_norm(alpha * x + y, g_ref[...], b_ref[...])


def mlp(x2, wu, wd, ln_g, ln_b, alpha, tm=512):
    t, d = x2.shape
    row = lambda i: (i, 0)
    fixed = lambda i: (0, 0)
    resident = lambda a: pl.BlockSpec(a.shape, fixed, pipeline_mode=pl.Buffered(1))
    return pl.pallas_call(
        functools.partial(_mlp_kernel, alpha=alpha),
        out_shape=jax.ShapeDtypeStruct((t, d), F32),
        grid=(t // tm,),
        in_specs=[pl.BlockSpec((tm, d), row), resident(wu), resident(wd),
                  pl.BlockSpec((1, d), fixed), pl.BlockSpec((1, d), fixed)],
        out_specs=pl.BlockSpec((tm, d), row),
        compiler_params=_params("parallel"),
        name="mlp",
    )(x2, wu, wd, ln_g, ln_b)


def _segment_cols(w, cols_per_head, head_offset, width=LANES):
    out = jnp.zeros(w.shape[:-1] + (len(cols_per_head) * width,), w.dtype)
    for h, cols in enumerate(cols_per_head):
        start = h * width + head_offset
        out = out.at[..., start:start + len(cols)].set(w[..., np.asarray(cols)])
    return out


def _prep_layer_weights(seq, w_in, cmp_pe, cmp_w1, cmp_w2, mla_q_norm, mla_w_uq, mla_kv_norm, mla_w_ukv,
                        fox_b_f, w_gate, w_br_nsa, w_br_mla, w_br_fox, w_mix_out, xa_w_q, xa_w_kv,
                        xa_w_o, mlp_w_up, mlp_w_down, ln_g, ln_b):
    depth, d, _ = w_in.shape
    sizes = (NSA_W, NSA_KV, NSA_KV, NSA_KV, NSA_KV, NSA_KV, NSA_KV, 3 * NSA_HEADS,
             MLA_Q_LORA, MLA_KV_LORA, MLA_ROPE, FOX_W, FOX_W, FOX_W, FOX_HEADS)
    offs = np.concatenate([[0], np.cumsum(sizes)])
    col = lambda n: w_in[:, :, offs[n]:offs[n + 1]]
    (nq, nkc, nvc, nks, nvs, nkw, nvw, ngate, cq, ckv, kr, fq, fk, fv, ff) = [col(n) for n in range(15)]
    qk_scale = DH ** -0.5
    wa = jnp.concatenate([nq * qk_scale, fq * qk_scale, fv, nvs, nkw, nvw], axis=2)
    pad = jnp.zeros((depth, d, _NB - (_OFF_FF + FOX_HEADS)), F32)
    wb = jnp.concatenate([cq, ckv, kr, ngate, ff, pad], axis=2)
    wc = jnp.concatenate([nkc, nvc], axis=2)
    wks = _segment_cols(nks, [np.arange(g * DH, (g + 1) * DH) for g in range(NSA_GROUPS)], 0,
                        DH + seq // L_SLC)
    wfk = _segment_cols(fk, [np.arange(h * DH, (h + 1) * DH) for h in range(FOX_HEADS)], 0)
    ff_lane = _OFF_FF - _COL_MISC * LANES
    b_row = jnp.zeros((depth, 1, LANES), F32).at[:, 0, ff_lane:ff_lane + FOX_HEADS].set(fox_b_f)

    dq = MLA_NOPE + MLA_ROPE
    half = MLA_ROPE // 2
    heads = range(MLA_HEADS)
    nope = [np.arange(h * dq, h * dq + MLA_NOPE) for h in heads]
    x1 = [np.arange(h * dq + MLA_NOPE, h * dq + MLA_NOPE + half) for h in heads]
    x2 = [c + half for c in x1]
    wqa = (_segment_cols(mla_w_uq, nope, 0) + _segment_cols(mla_w_uq, x1, MLA_NOPE)
           + _segment_cols(mla_w_uq, x2, MLA_NOPE + half))
    wqb = _segment_cols(mla_w_uq, x2, MLA_NOPE) + _segment_cols(mla_w_uq, x1, MLA_NOPE + half)
    dkv = MLA_NOPE + MLA_V
    wk = _segment_cols(mla_w_ukv, [np.arange(h * dkv, h * dkv + MLA_NOPE) for h in heads], 0)
    wv = mla_w_ukv[:, :, np.concatenate([np.arange(h * dkv + MLA_NOPE, (h + 1) * dkv) for h in heads])]
    pa = np.zeros((MLA_ROPE, MLA_HEADS * LANES), np.float32)
    pb = np.zeros((MLA_ROPE, MLA_HEADS * LANES), np.float32)
    for h in heads:
        for j in range(MLA_ROPE):
            pa[j, h * LANES + MLA_NOPE + j] = 1.0
            pb[j, h * LANES + MLA_NOPE + (j + half) % MLA_ROPE] = 1.0
    stack = lambda top, place: jnp.concatenate(
        [top, jnp.broadcast_to(jnp.asarray(np.concatenate([place, place])), (depth,) + (2 * MLA_ROPE, place.shape[1]))],
        axis=1)
    wka = stack(wk, pa)
    wkb = stack(jnp.zeros_like(wk), pb)

    eye_g = jnp.eye(NSA_GROUPS, dtype=F32)
    w1r = cmp_w1.reshape(depth, 2, L_CMP, DH, CMP_HIDDEN)
    def chunk_weights(w1_half):
        blk = jnp.einsum('ealdj,gh->ealgdhj', w1_half, eye_g)
        return blk.reshape(depth, 2, D_CMP * NSA_KV, NSA_GROUPS * CMP_HIDDEN)
    w1lo = chunk_weights(w1r[:, :, :D_CMP])
    w1hi = chunk_weights(w1r[:, :, D_CMP:])
    w2b = jnp.einsum('eajd,gh->eagjhd', cmp_w2, eye_g).reshape(depth, 2, NSA_GROUPS * CMP_HIDDEN, NSA_KV)
    pe = jnp.broadcast_to(cmp_pe.reshape(depth, 2, 2, D_CMP, 1, DH),
                          (depth, 2, 2, D_CMP, NSA_GROUPS, DH)).reshape(depth, 2, 2, D_CMP * NSA_KV)

    c = lambda a: a.astype(MXU_DT)
    return dict(
        wa=c(wa), wb=c(wb), wc=c(wc), wks=c(wks), wfk=c(wfk),
        pe=pe, w1lo=c(w1lo), w1hi=c(w1hi), w2b=c(w2b),
        q_norm=mla_q_norm[:, None, :], wqa=c(wqa), wqb=c(wqb),
        kv_norm=mla_kv_norm[:, None, :], wka=c(wka), wkb=c(wkb), wv=c(wv),
        b_row=b_row, wg=c(w_gate), wn=c(w_br_nsa), wm=c(w_br_mla), wf=c(w_br_fox), wo=c(w_mix_out),
        xq=c(xa_w_q * qk_scale), xkv=c(xa_w_kv), xo=c(xa_w_o),
        wu=c(mlp_w_up), wd=c(mlp_w_down),
        ln_g=ln_g[:, :, None, :], ln_b=ln_b[:, :, None, :],
    )


def _layer(x, mem2, w, tabs, alpha):
    b, s, d = x.shape
    t = b * s
    x2 = x.reshape(t, d)

    ya, yb, ykc, yvc, yks, yfk = in_proj(x2, w, tabs, s)
    ya3 = ya.reshape(b, s, _NA)
    yb3 = yb.reshape(b, s, _NB)

    nc = s // D_CMP
    kc = compress(ykc.reshape(b, nc, D_CMP * NSA_KV), w["pe"][0], w["w1lo"][0], w["w1hi"][0], w["w2b"][0])
    vc = compress(yvc.reshape(b, nc, D_CMP * NSA_KV), w["pe"][1], w["w1lo"][1], w["w1hi"][1], w["w2b"][1])
    o_nsa = nsa_attention(ya3, yb3, kc, vc, yks.reshape(b, s, -1),
                          tabs["bias_c"], tabs["bias_w"], tabs["overlap_t"])

    q_mla, k_mla, v_mla = mla_up(yb, w, tabs, s)
    wide = MLA_HEADS * LANES
    o_mla = causal_attention(q_mla.reshape(b, s, wide), 0, k_mla.reshape(b, s, wide),
                             v_mla.reshape(b, s, MLA_W), 0, MLA_HEADS, LANES)

    k_decay = fox_decay(yb3, w["b_row"], tabs["decay_place"])
    o_fox = causal_attention(ya3, _COL_FQ, yfk.reshape(b, s, FOX_HEADS * LANES), ya3, _COL_FQ + 1,
                             FOX_HEADS, DH, k_decay=k_decay)

    x2 = mixer_out(x2, o_nsa.reshape(t, NSA_W), o_mla.reshape(t, MLA_W), o_fox.reshape(t, FOX_W),
                   w["wg"], w["wn"], w["wm"], w["wf"], w["wo"], w["ln_g"][0], w["ln_b"][0], alpha)

    m = mem2.shape[0] // b
    kv = matmul(mem2, w["xkv"], MXU_DT, tm=min(512, mem2.shape[0])).reshape(b, m, 2 * XA_W)
    x3 = cross_attention(x2.reshape(b, s, d), kv, w["xq"], w["xo"], w["ln_g"][1], w["ln_b"][1], alpha)

    x2 = mlp(x3.reshape(t, d), w["wu"], w["wd"], w["ln_g"][2], w["ln_b"][2], alpha)
    return x2.reshape(b, s, d)


def _tables(t5_table, s):
    tq = NSA_TQ
    ki = jnp.arange(tq)[:, None]
    qi = jnp.arange(tq)[None, :]
    last = jnp.full((tq, tq), T5_BUCKETS - 1, jnp.int32)
    masked = jnp.full((tq, tq), -1, jnp.int32)
    kinds = [None] * N_TILE_KINDS
    kinds[TILE_DIAG] = jnp.where(qi >= ki, _t5_bucket(qi - ki), -1)
    kinds[TILE_PREV] = _t5_bucket(tq + qi - ki)
    kinds[TILE_FAR] = last
    kinds[TILE_EDGE] = jnp.where(qi < ki, last, masked)
    kinds[TILE_NONE] = masked
    kinds[TILE_SEL_DIAG] = kinds[TILE_DIAG]
    kinds[TILE_SEL_PREV] = kinds[TILE_PREV]
    kinds[TILE_ZERO] = last
    relative = [int(k in (TILE_SEL_DIAG, TILE_SEL_PREV, TILE_ZERO)) for k in range(N_TILE_KINDS)]
    idx_w = jnp.concatenate(kinds, axis=0).astype(jnp.int32)
    rel_w = jnp.concatenate([jnp.full((tq, tq), r, jnp.int32) for r in relative], axis=0)
    bias_w = bias_table(t5_table, idx_w, rel_w, tq).reshape(NSA_HEADS, N_TILE_KINDS, tq, tq)

    nc = s // D_CMP
    cmp_end = jnp.arange(nc) * D_CMP + (L_CMP - 1)
    dist_c = jnp.arange(s)[None, :] - cmp_end[:, None]
    idx_c = jnp.where(dist_c >= 0, _t5_bucket(dist_c), -1).astype(jnp.int32)
    bias_c = bias_table(t5_table, idx_c, jnp.zeros_like(idx_c), min(64, nc))

    n_slc = s // L_SLC
    block_onehot = (np.arange(s)[:, None] // L_SLC == np.arange(n_slc)[None, :]).astype(np.float32)
    block_cols = np.concatenate([np.zeros((s, DH), np.float32), block_onehot] * NSA_GROUPS, axis=1)
    ff_lane = _OFF_FF - _COL_MISC * LANES
    decay_place = np.zeros((3, LANES, FOX_HEADS * LANES), np.float32)
    for j in range(3):
        for h in range(FOX_HEADS):
            decay_place[j, ff_lane + h, h * LANES + DH + j] = 1.0
    c_lo = np.arange(nc)[:, None] * D_CMP
    s_lo = np.arange(n_slc)[None, :] * L_SLC
    overlap = np.maximum(np.minimum(c_lo + L_CMP, s_lo + L_SLC) - np.maximum(c_lo, s_lo), 0) / D_CMP
    overlap[nc - 1, :] = 0.0

    half = MLA_ROPE // 2
    inv = ROPE_THETA ** (-jnp.arange(half, dtype=F32) / half)
    ang = jnp.arange(s).astype(F32)[:, None] * inv[None, :]
    cos, sin = jnp.cos(ang), jnp.sin(ang)
    tail = jnp.zeros((s, LANES - MLA_NOPE - MLA_ROPE), F32)
    seg_c = jnp.concatenate([jnp.ones((s, MLA_NOPE), F32), cos, cos, tail], axis=1)
    seg_s = jnp.concatenate([jnp.zeros((s, MLA_NOPE), F32), -sin, sin, tail], axis=1)
    col_scale = np.ones((1, _NA), np.float32)
    col_scale[0, :NSA_W] = LOG2E
    col_scale[0, _COL_FQ * FOX_W:(_COL_FQ + 1) * FOX_W] = LOG2E
    return dict(bias_w=bias_w, bias_c=bias_c, overlap_t=jnp.asarray(overlap.T, MXU_DT),
                block_cols=jnp.asarray(block_cols), decay_place=jnp.asarray(decay_place, MXU_DT),
                rope_cos=jnp.tile(seg_c, (1, MLA_HEADS)), rope_sin=jnp.tile(seg_s, (1, MLA_HEADS)),
                col_scale=jnp.asarray(col_scale))


def kernel(x, mem, w_in, cmp_pe, cmp_w1, cmp_w2, t5_table, mla_q_norm, mla_w_uq, mla_kv_norm, mla_w_ukv, fox_b_f, w_gate, w_br_nsa, w_br_mla, w_br_fox, w_mix_out, xa_w_q, xa_w_kv, xa_w_o, mlp_w_up, mlp_w_down, ln_g, ln_b):
    depth = w_in.shape[0]
    b, s, d = x.shape
    alpha = (2 * depth) ** 0.25
    weights = _prep_layer_weights(s, w_in, cmp_pe, cmp_w1, cmp_w2, mla_q_norm, mla_w_uq, mla_kv_norm,
                                  mla_w_ukv, fox_b_f, w_gate, w_br_nsa, w_br_mla, w_br_fox, w_mix_out,
                                  xa_w_q, xa_w_kv, xa_w_o, mlp_w_up, mlp_w_down, ln_g, ln_b)
    tabs = _tables(t5_table, s)
    mem2 = mem.reshape(-1, d)

    def step(xc, w):
        return _layer(xc, mem2, w, tabs, alpha), None

    out, _ = lax.scan(step, x, weights)
    return out
```

```python
import functools
import math

import numpy as np
import jax
import jax.numpy as jnp
from jax import lax
from jax.experimental import pallas as pl
from jax.experimental.pallas import tpu as pltpu

D_MODEL = 1024
DH = 64
NSA_HEADS = 8
NSA_GROUPS = 2
NSA_R = NSA_HEADS // NSA_GROUPS
L_CMP = 32
D_CMP = 16
CMP_HIDDEN = 128
L_SLC = 64
N_SEL = 8
WINDOW = 512
MLA_HEADS = 4
MLA_NOPE = 64
MLA_ROPE = 32
MLA_V = 64
MLA_Q_LORA = 384
MLA_KV_LORA = 128
ROPE_THETA = 10000.0
FOX_HEADS = 4
XA_HEADS = 4
D_FF = 4 * D_MODEL
T5_BUCKETS = 32
T5_MAX_DIST = 128
NSA_W = NSA_HEADS * DH
NSA_KV = NSA_GROUPS * DH
MLA_W = MLA_HEADS * MLA_V
FOX_W = FOX_HEADS * DH
XA_W = XA_HEADS * DH
LN_EPS = 1e-5
RMS_EPS = 1e-6
NEG = -1e30
FORCE = 1e4
LOG2E = math.log2(math.e)

F32 = jnp.float32
MXU_DT = jnp.bfloat16
LANES = 128
NSA_TQ = 128
NSA_RQ = NSA_R * NSA_TQ
SEL_TK = 512
ATT_T = 512
VMEM_LIMIT = 48 * 1024 * 1024

TILE_DIAG = 0
TILE_PREV = 1
TILE_FAR = 2
TILE_EDGE = 3
TILE_NONE = 4
TILE_SEL_DIAG = 5
TILE_SEL_PREV = 6
TILE_ZERO = 7
N_TILE_KINDS = 8

_NA = NSA_W + 2 * FOX_W + 3 * NSA_KV
_COL_FQ = NSA_W // FOX_W
_COL_VS = (NSA_W + 2 * FOX_W) // NSA_KV
_OFF_CQ = 0
_OFF_CKV = MLA_Q_LORA
_OFF_KR = _OFF_CKV + MLA_KV_LORA
_OFF_NG = _OFF_KR + MLA_ROPE
_OFF_FF = _OFF_NG + 3 * NSA_HEADS
_NB = 640
_COL_MISC = _OFF_KR // LANES


def _dot(a, b):
    return jnp.dot(a, b, preferred_element_type=F32)


def _dot_nt(a, b):
    return lax.dot_general(a, b, (((1,), (1,)), ((), ())), preferred_element_type=F32)


def _params(*sem):
    return pltpu.CompilerParams(dimension_semantics=sem, vmem_limit_bytes=VMEM_LIMIT)


def _layer_norm(z, g, b):
    mu = jnp.mean(z, axis=-1, keepdims=True)
    zc = z - mu
    var = jnp.mean(zc * zc, axis=-1, keepdims=True)
    return zc * lax.rsqrt(var + LN_EPS) * g + b


def _rms_norm(z, g):
    return z * lax.rsqrt(jnp.mean(z * z, axis=-1, keepdims=True) + RMS_EPS) * g


def _t5_bucket(dist):
    n = jnp.maximum(dist, 0)
    max_exact = T5_BUCKETS // 2
    nf = jnp.maximum(n, 1).astype(F32)
    large = max_exact + (jnp.log(nf / max_exact) / math.log(T5_MAX_DIST / max_exact)
                         * (T5_BUCKETS - max_exact)).astype(jnp.int32)
    large = jnp.minimum(large, T5_BUCKETS - 1)
    return jnp.where(n < max_exact, n, large)


SUM_ROWS = 16
ACC_ROWS = DH + SUM_ROWS


def _with_ones(v):
    return jnp.concatenate([v, jnp.ones(v.shape, v.dtype)], axis=1)


def _with_sums(v):
    return jnp.concatenate([v, jnp.ones((v.shape[0], SUM_ROWS), v.dtype)], axis=1)


def _bias_table_kernel(t5_ref, idx_ref, rel_ref, o_ref):
    h = pl.program_id(0)
    idx = idx_ref[...]
    acc = jnp.full(idx.shape, NEG, F32)
    for b in range(T5_BUCKETS):
        acc = jnp.where(idx == b, t5_ref[b, h] * LOG2E, acc)
    o_ref[0] = acc - jnp.where(rel_ref[...] > 0, t5_ref[T5_BUCKETS - 1, h] * LOG2E, 0.0)


def bias_table(t5_table, idx, rel, tr):
    rows, cols = idx.shape
    return pl.pallas_call(
        _bias_table_kernel,
        out_shape=jax.ShapeDtypeStruct((NSA_HEADS, rows, cols), F32),
        grid=(NSA_HEADS, rows // tr),
        in_specs=[pl.BlockSpec(memory_space=pltpu.SMEM),
                  pl.BlockSpec((tr, cols), lambda h, i: (i, 0)),
                  pl.BlockSpec((tr, cols), lambda h, i: (i, 0))],
        out_specs=pl.BlockSpec((1, tr, cols), lambda h, i: (h, i, 0)),
        compiler_params=_params("parallel", "parallel"),
        name="bias_table",
    )(t5_table, idx, rel)


def _in_proj_kernel(x_ref, wa_ref, wb_ref, wc_ref, wks_ref, wfk_ref, sc_ref, blk_ref,
                    ya_ref, yb_ref, ykc_ref, yvc_ref, yks_ref, yfk_ref):
    xb = x_ref[...].astype(MXU_DT)
    ya_ref[...] = (_dot(xb, wa_ref[...]) * sc_ref[...]).astype(ya_ref.dtype)
    yb_ref[...] = _dot(xb, wb_ref[...])
    yc = _dot(xb, wc_ref[...])
    ykc_ref[...] = yc[:, :NSA_KV]
    yvc_ref[...] = yc[:, NSA_KV:]
    yks_ref[...] = (_dot(xb, wks_ref[...]) + blk_ref[...]).astype(yks_ref.dtype)
    yfk_ref[...] = _dot(xb, wfk_ref[...]).astype(yfk_ref.dtype)


def in_proj(x2, w, tabs, seq, tm=512):
    t, d = x2.shape
    nps = seq // tm
    row = lambda i: (i, 0)
    fixed = lambda i: (0, 0)
    weights = [w["wa"], w["wb"], w["wc"], w["wks"], w["wfk"]]
    widths = [(_NA, MXU_DT), (_NB, F32), (NSA_KV, F32), (NSA_KV, F32),
              (w["wks"].shape[1], MXU_DT), (w["wfk"].shape[1], MXU_DT)]
    return pl.pallas_call(
        _in_proj_kernel,
        out_shape=tuple(jax.ShapeDtypeStruct((t, n), dt) for n, dt in widths),
        grid=(t // tm,),
        in_specs=[pl.BlockSpec((tm, d), row)] + [pl.BlockSpec(a.shape, fixed) for a in weights]
        + [pl.BlockSpec((1, _NA), fixed),
           pl.BlockSpec((tm, tabs["block_cols"].shape[1]), lambda i: (i % nps, 0))],
        out_specs=tuple(pl.BlockSpec((tm, n), row) for n, _ in widths),
        compiler_params=_params("parallel"),
        name="in_proj",
    )(x2, *weights, tabs["col_scale"], tabs["block_cols"])


def _mla_up_kernel(yb_ref, qn_ref, wqa_ref, wqb_ref, kvn_ref, wka_ref, wkb_ref, wv_ref,
                   ct_ref, st_ref, q_ref, k_ref, v_ref):
    scale = (MLA_NOPE + MLA_ROPE) ** -0.5 * LOG2E
    ct = ct_ref[...]
    st = st_ref[...]
    cq = _rms_norm(yb_ref[:, _OFF_CQ:_OFF_CQ + MLA_Q_LORA], qn_ref[...]).astype(MXU_DT)
    q = _dot(cq, wqa_ref[...]) * ct + _dot(cq, wqb_ref[...]) * st
    q_ref[...] = (q * scale).astype(q_ref.dtype)
    ckv = _rms_norm(yb_ref[:, _OFF_CKV:_OFF_CKV + MLA_KV_LORA], kvn_ref[...]).astype(MXU_DT)
    v_ref[...] = _dot(ckv, wv_ref[...]).astype(v_ref.dtype)
    kr = yb_ref[:, _OFF_KR:_OFF_KR + MLA_ROPE]
    kr_hi = kr.astype(MXU_DT)
    kr_lo = (kr - kr_hi.astype(F32)).astype(MXU_DT)
    lhs = jnp.concatenate([ckv, kr_hi, kr_lo], axis=1)
    k_ref[...] = (_dot(lhs, wka_ref[...]) * ct + _dot(lhs, wkb_ref[...]) * st).astype(k_ref.dtype)


def mla_up(yb, w, tabs, seq, tm=512):
    t = yb.shape[0]
    nps = seq // tm
    row = lambda i: (i, 0)
    fixed = lambda i: (0, 0)
    pos = lambda i: (i % nps, 0)
    wide = MLA_HEADS * LANES
    ins = [w["q_norm"], w["wqa"], w["wqb"], w["kv_norm"], w["wka"], w["wkb"], w["wv"]]
    return pl.pallas_call(
        _mla_up_kernel,
        out_shape=(jax.ShapeDtypeStruct((t, wide), MXU_DT), jax.ShapeDtypeStruct((t, wide), MXU_DT),
                   jax.ShapeDtypeStruct((t, MLA_W), MXU_DT)),
        grid=(t // tm,),
        in_specs=[pl.BlockSpec((tm, _NB), row)] + [pl.BlockSpec(a.shape, fixed) for a in ins]
        + [pl.BlockSpec((tm, wide), pos), pl.BlockSpec((tm, wide), pos)],
        out_specs=(pl.BlockSpec((tm, wide), row), pl.BlockSpec((tm, wide), row),
                   pl.BlockSpec((tm, MLA_W), row)),
        compiler_params=_params("parallel"),
        name="mla_up",
    )(yb, *ins, tabs["rope_cos"], tabs["rope_sin"])


def _bf16_head(x):
    bits = lax.bitcast_convert_type(x, jnp.uint32) & jnp.uint32(0xFFFF0000)
    return lax.bitcast_convert_type(bits, F32)


def _fox_decay_kernel(f_ref, b_ref, place_ref, o_ref):
    z = f_ref[0] + b_ref[...]
    x = jnp.minimum(z, 0.0) - jnp.log1p(jnp.exp(-jnp.abs(z)))
    n = x.shape[0]
    row = lax.broadcasted_iota(jnp.int32, x.shape, 0)
    shift = 1
    while shift < n:
        x = x + jnp.where(row >= shift, pltpu.roll(x, shift, 0), 0.0)
        shift *= 2
    dec = -(x * LOG2E)
    hi = _bf16_head(dec)
    rest = dec - hi
    mid = _bf16_head(rest)
    terms = (hi, mid, rest - mid)
    o_ref[0] = sum(_dot(t.astype(MXU_DT), place_ref[j]) for j, t in enumerate(terms)).astype(o_ref.dtype)


def fox_decay(misc, b_row, place):
    b, s, _ = misc.shape
    wide = place.shape[2]
    return pl.pallas_call(
        _fox_decay_kernel,
        out_shape=jax.ShapeDtypeStruct((b, s, wide), MXU_DT),
        grid=(b,),
        in_specs=[pl.BlockSpec((1, s, LANES), lambda bi: (bi, 0, _COL_MISC)),
                  pl.BlockSpec((1, LANES), lambda bi: (0, 0)),
                  pl.BlockSpec(place.shape, lambda bi: (0, 0, 0))],
        out_specs=pl.BlockSpec((1, s, wide), lambda bi: (bi, 0, 0)),
        compiler_params=_params("parallel"),
        name="fox_decay",
    )(misc, b_row, place)


def _keys_softmax_update(m, acc, s, v_aug):
    m_new = jnp.maximum(m, jnp.max(s, axis=0, keepdims=True))
    a = jnp.exp2(m - m_new)
    p = jnp.exp2(s - m_new).astype(MXU_DT)
    pv = lax.dot_general(v_aug, p, (((0,), (0,)), ((), ())), preferred_element_type=F32)
    return m_new, a * acc + pv


def _keys_softmax_init(queries):
    return jnp.full((1, queries), NEG, F32), jnp.zeros((ACC_ROWS, queries), F32)


def _softmax_piece(s, v_aug):
    m = jnp.max(s, axis=0, keepdims=True)
    p = jnp.exp2(s - m).astype(MXU_DT)
    return m, lax.dot_general(v_aug, p, (((0,), (0,)), ((), ())), preferred_element_type=F32)


def _softmax_merge(state, pieces):
    m_old, acc = state
    m = m_old
    for mp, _ in pieces:
        m = jnp.maximum(m, mp)
    acc = acc * jnp.exp2(m_old - m)
    for mp, ap in pieces:
        acc = acc + ap * jnp.exp2(mp - m)
    return m, acc


def _causal_attn_kernel(*refs, n_heads, dq, decay):
    if decay:
        q_ref, k_ref, kd_ref, v_ref, o_ref = refs
    else:
        q_ref, k_ref, v_ref, o_ref = refs
    qi = pl.program_id(1)
    t = q_ref.shape[1]
    key = lax.broadcasted_iota(jnp.int32, (t, t), 0)
    qry = lax.broadcasted_iota(jnp.int32, (t, t), 1)

    qs = []
    for h in range(n_heads):
        q = q_ref[0, :, h * dq:(h + 1) * dq]
        qs.append(_with_ones(q) if decay else q)

    def scores(rows, h):
        k = k_ref[0, rows, h * LANES:(h + 1) * LANES]
        if decay:
            k = k + kd_ref[0, rows, h * LANES:(h + 1) * LANES]
        return _dot_nt(k, qs[h])

    def values(rows, h):
        return _with_sums(v_ref[0, rows, h * DH:(h + 1) * DH])

    def sweep(rows, state, mask):
        s = [scores(rows, h) for h in range(n_heads)]
        if mask is not None:
            s = [jnp.where(mask, sh, NEG) for sh in s]
        return tuple(_keys_softmax_update(*state[h], s[h], values(rows, h)) for h in range(n_heads))

    n_double = lax.shift_right_logical(qi, 1)
    state = lax.fori_loop(
        0, n_double, lambda p, st: sweep(pl.ds(pl.multiple_of(p * 2 * t, 2 * t), 2 * t), st, None),
        tuple(_keys_softmax_init(t) for _ in range(n_heads)))
    state = lax.fori_loop(
        0, qi & 1, lambda _, st: sweep(pl.ds(pl.multiple_of(n_double * 2 * t, t), t), st, None), state)
    state = sweep(pl.ds(pl.multiple_of(qi * t, t), t), state, key <= qry)
    outs = [acc[:DH] * (1.0 / acc[DH:DH + 1]) for _, acc in state]
    for j in range(n_heads // 2):
        slab = jnp.concatenate(outs[2 * j:2 * j + 2], axis=0).T
        o_ref[0, :, 2 * j * DH:(2 * j + 2) * DH] = slab.astype(o_ref.dtype)


def causal_attention(q, q_col, k, v, v_col, n_heads, dq, k_decay=None, t=ATT_T):
    b, s, _ = q.shape
    whole = pl.BlockSpec((1, s, n_heads * LANES), lambda bi, qi: (bi, 0, 0))
    keys = [k] if k_decay is None else [k, k_decay]
    return pl.pallas_call(
        functools.partial(_causal_attn_kernel, n_heads=n_heads, dq=dq, decay=k_decay is not None),
        out_shape=jax.ShapeDtypeStruct((b, s, n_heads * DH), MXU_DT),
        grid=(b, s // t),
        in_specs=[pl.BlockSpec((1, t, n_heads * dq), lambda bi, qi: (bi, qi, q_col))]
        + [whole] * len(keys)
        + [pl.BlockSpec((1, s, n_heads * DH), lambda bi, qi: (bi, 0, v_col))],
        out_specs=pl.BlockSpec((1, t, n_heads * DH), lambda bi, qi: (bi, qi, 0)),
        compiler_params=_params("parallel", "arbitrary"),
        name="causal_attn_decay" if k_decay is not None else "causal_attn",
    )(q, *keys, v)


def _compress_kernel(x_ref, pe_ref, wlo_ref, whi_ref, w2_ref, o_ref):
    x = x_ref[0]
    nc = x.shape[0]
    lo = _dot((x + pe_ref[0:1, :]).astype(MXU_DT), wlo_ref[...])
    hi = _dot((x + pe_ref[1:2, :]).astype(MXU_DT), whi_ref[...])
    hid = lo + pltpu.roll(hi, nc - 1, 0)
    act = jax.nn.gelu(hid)
    o_ref[0] = _dot(act.astype(MXU_DT), w2_ref[...]).astype(o_ref.dtype)


def compress(x, pe, wlo, whi, w2):
    b, nc, width = x.shape
    fixed = lambda bi: (0, 0)
    return pl.pallas_call(
        _compress_kernel,
        out_shape=jax.ShapeDtypeStruct((b, nc, NSA_KV), MXU_DT),
        grid=(b,),
        in_specs=[pl.BlockSpec((1, nc, width), lambda bi: (bi, 0, 0)),
                  pl.BlockSpec(pe.shape, fixed), pl.BlockSpec(wlo.shape, fixed),
                  pl.BlockSpec(whi.shape, fixed), pl.BlockSpec(w2.shape, fixed)],
        out_specs=pl.BlockSpec((1, nc, NSA_KV), lambda bi: (bi, 0, 0)),
        compiler_params=_params("parallel"),
        name="nsa_compress",
    )(x, pe, wlo, whi, w2)


def _nsa_kernel(q_ref, kc_ref, vc_ref, ks_ref, vs_ref, kw_ref, vw_ref, misc_ref,
                bc_ref, bw_ref, ovt_ref, o_ref, *, n_sel):
    i = pl.program_id(1)
    tq, rq = NSA_TQ, NSA_RQ
    nslc = ovt_ref.shape[0]
    seg = DH + nslc
    nwt = WINDOW // tq + 1
    sub = SEL_TK // tq
    t0 = i * tq
    g_off = _OFF_NG - _COL_MISC * LANES
    gates_t = jax.nn.sigmoid(misc_ref[0]).T

    lane = lax.broadcasted_iota(jnp.int32, (1, rq), 1)
    has_cmp = (t0 + (lane & (tq - 1))) >= (L_CMP - 1)

    blk = lax.broadcasted_iota(jnp.int32, (nslc, tq), 0)
    blk_f = blk.astype(F32)
    tpos = lax.broadcasted_iota(jnp.int32, (nslc, tq), 1) + t0
    cur = lax.shift_right_logical(tpos, int(math.log2(L_SLC)))
    forced = (blk == 0) | (blk == cur) | (blk == cur - 1)
    causal_blk = blk * L_SLC <= tpos
    eye = (lax.broadcasted_iota(jnp.int32, (tq, tq), 0)
           == lax.broadcasted_iota(jnp.int32, (tq, tq), 1)).astype(MXU_DT)

    def window_tile_kind(d):
        static = {0: TILE_DIAG, 1: TILE_PREV, nwt - 1: TILE_EDGE}.get(d, TILE_FAR)
        return jnp.where(i - d < 0, TILE_NONE, static)

    def group_cols(g):
        return slice(g * DH, (g + 1) * DH)

    groups = range(NSA_GROUPS)
    q_groups = [jnp.concatenate([q_ref[0, :, (g * NSA_R + r) * DH:(g * NSA_R + r + 1) * DH]
                                 for r in range(NSA_R)], axis=0) for g in groups]

    def group_lanes(tile_of_head, g):
        return jnp.concatenate([tile_of_head(g * NSA_R + r) for r in range(NSA_R)], axis=1)

    def bias_tile(g, kind):
        return group_lanes(lambda h: bw_ref[h, kind], g)

    s_cmp = [_dot_nt(kc_ref[0, :, group_cols(g)], q_groups[g]) + group_lanes(lambda h: bc_ref[h], g)
             for g in groups]
    win_rows = [pl.ds(pl.multiple_of(jnp.maximum(i - d, 0) * tq, tq), tq) for d in range(nwt - 1, -1, -1)]
    s_win = []
    for g in groups:
        kcat = jnp.concatenate([kw_ref[0, r, group_cols(g)] for r in win_rows], axis=0)
        bias = jnp.concatenate([bias_tile(g, window_tile_kind(d)) for d in range(nwt - 1, -1, -1)], axis=0)
        s_win.append(_dot_nt(kcat, q_groups[g]) + bias)

    def gate_row(g, j):
        return jnp.concatenate(
            [gates_t[g_off + 3 * (g * NSA_R + r) + j:g_off + 3 * (g * NSA_R + r) + j + 1] for r in range(NSA_R)],
            axis=1)

    def weighted_values(v, p):
        return lax.dot_general(_with_sums(v), p, (((0,), (0,)), ((), ())), preferred_element_type=F32)

    o_cmp, imp = [], []
    for g in groups:
        e = jnp.exp2(s_cmp[g] - jnp.max(s_cmp[g], axis=0, keepdims=True)).astype(MXU_DT)
        acc = weighted_values(vc_ref[0, :, group_cols(g)], e)
        inv = jnp.where(has_cmp, 1.0 / acc[DH:DH + 1], 0.0)
        o_cmp.append(acc[:DH] * inv)
        imp4 = _dot(ovt_ref[...], e) * inv
        imp.append(sum(imp4[:, r * tq:(r + 1) * tq] for r in range(1, NSA_R)) + imp4[:, 0:tq])

    o_fixed = []
    for g in groups:
        vcat = jnp.concatenate([vw_ref[0, r, group_cols(g)] for r in win_rows], axis=0)
        p = jnp.exp2(s_win[g] - jnp.max(s_win[g], axis=0, keepdims=True)).astype(MXU_DT)
        acc = weighted_values(vcat, p)
        o_win = acc[:DH] * (1.0 / acc[DH:DH + 1])
        o_fixed.append(gate_row(g, 0) * o_cmp[g] + gate_row(g, 2) * o_win)

    q_aug = []
    for g in groups:
        x = jnp.where(causal_blk, imp[g] + jnp.where(forced, FORCE, 0.0), NEG)
        sel_t = jnp.zeros((nslc, tq), F32)
        for _ in range(n_sel):
            mx = jnp.max(x, axis=0, keepdims=True)
            first = jnp.min(jnp.where(x == mx, blk_f, float(nslc)), axis=0, keepdims=True)
            hit = blk_f == first
            sel_t = jnp.where(hit, 1.0, sel_t)
            x = jnp.where(hit, -3e38, x)
        penalty = ((1.0 - _dot_nt(eye, sel_t.astype(MXU_DT))) * NEG).astype(MXU_DT)
        q_aug.append(jnp.concatenate([q_groups[g], jnp.concatenate([penalty] * NSA_R, axis=0)], axis=1))

    def sel_scores(rows, g):
        return _dot_nt(ks_ref[0, rows, g * seg:(g + 1) * seg], q_aug[g])

    def sel_values(rows, g):
        return _with_sums(vs_ref[0, rows, group_cols(g)])

    def far_sweep(start, n_tiles, state):
        tiles = [pl.ds(pl.multiple_of(start + j * SEL_TK, SEL_TK), SEL_TK) for j in range(n_tiles)]
        s = [[sel_scores(rows, g) for rows in tiles] for g in groups]
        pieces = [[_softmax_piece(s[g][j], sel_values(rows, g)) for j, rows in enumerate(tiles)]
                  for g in groups]
        return tuple(_softmax_merge(state[g], pieces[g]) for g in groups)

    last = i // sub
    n_far = jnp.maximum(last - 1, 0)
    n_double = lax.shift_right_logical(n_far, 1)
    state = lax.fori_loop(0, n_double, lambda p, st: far_sweep(p * 2 * SEL_TK, 2, st),
                          tuple(_keys_softmax_init(rq) for _ in groups))
    state = lax.fori_loop(0, n_far & 1, lambda _, st: far_sweep(n_double * 2 * SEL_TK, 1, st), state)
    near = []
    for kn in (last - 1, last):
        kt = pl.ds(pl.multiple_of(jnp.maximum(kn, 0) * SEL_TK, SEL_TK), SEL_TK)
        kinds = []
        for j in range(sub):
            d = i - (kn * sub + j)
            kind = jnp.where(d < 0, TILE_NONE, jnp.where(d == 0, TILE_SEL_DIAG,
                                                         jnp.where(d == 1, TILE_SEL_PREV, TILE_ZERO)))
            kinds.append(jnp.where(kn < 0, TILE_NONE, kind))
        for g in groups:
            bias = jnp.concatenate([bias_tile(g, kd) for kd in kinds], axis=0)
            near.append((kt, g, sel_scores(kt, g) + bias))
    pieces = [[] for _ in groups]
    for kt, g, s in near:
        pieces[g].append(_softmax_piece(s, sel_values(kt, g)))
    state = [_softmax_merge(state[g], pieces[g]) for g in groups]

    out_t = []
    for g in groups:
        acc = state[g][1]
        out_t.append(o_fixed[g] + gate_row(g, 1) * (acc[:DH] * (1.0 / acc[DH:DH + 1])))
    out_t = jnp.concatenate(out_t, axis=0)
    for r in range(NSA_R):
        slab = out_t[:, r * tq:(r + 1) * tq].T
        for g in groups:
            h = g * NSA_R + r
            o_ref[0, :, h * DH:(h + 1) * DH] = slab[:, g * DH:(g + 1) * DH].astype(o_ref.dtype)


def nsa_attention(ya, yb, kc, vc, ks_aug, bias_c, bias_w, overlap_t):
    b, s, _ = ya.shape
    tq = NSA_TQ
    ncp = kc.shape[1]
    per_b_kv = lambda col: pl.BlockSpec((1, s, NSA_KV), lambda bi, i: (bi, 0, col))
    per_b_cmp = pl.BlockSpec((1, ncp, NSA_KV), lambda bi, i: (bi, 0, 0))
    return pl.pallas_call(
        functools.partial(_nsa_kernel, n_sel=min(N_SEL, overlap_t.shape[0])),
        out_shape=jax.ShapeDtypeStruct((b, s, NSA_W), MXU_DT),
        grid=(b, s // tq),
        in_specs=[pl.BlockSpec((1, tq, NSA_W), lambda bi, i: (bi, i, 0)),
                  per_b_cmp, per_b_cmp,
                  pl.BlockSpec((1, s, ks_aug.shape[2]), lambda bi, i: (bi, 0, 0)),
                  per_b_kv(_COL_VS), per_b_kv(_COL_VS + 1), per_b_kv(_COL_VS + 2),
                  pl.BlockSpec((1, tq, LANES), lambda bi, i: (bi, i, _COL_MISC)),
                  pl.BlockSpec((NSA_HEADS, ncp, tq), lambda bi, i: (0, 0, i)),
                  pl.BlockSpec(bias_w.shape, lambda bi, i: (0, 0, 0, 0)),
                  pl.BlockSpec(overlap_t.shape, lambda bi, i: (0, 0))],
        out_specs=pl.BlockSpec((1, tq, NSA_W), lambda bi, i: (bi, i, 0)),
        compiler_params=_params("parallel", "arbitrary"),
        name="nsa_attn",
    )(ya, kc, vc, ks_aug, ya, ya, ya, yb, bias_c, bias_w, overlap_t)


def _mix_kernel(x_ref, on_ref, om_ref, of_ref, wg_ref, wn_ref, wm_ref, wf_ref, wo_ref,
                g_ref, b_ref, o_ref, *, alpha):
    x = x_ref[...]
    xb = x.astype(MXU_DT)
    d = x.shape[1]
    merged = (jax.nn.sigmoid(_dot(xb, wg_ref[:, 0:d])) * _dot(on_ref[...], wn_ref[...])
              + jax.nn.sigmoid(_dot(xb, wg_ref[:, d:2 * d])) * _dot(om_ref[...], wm_ref[...])
              + jax.nn.sigmoid(_dot(xb, wg_ref[:, 2 * d:3 * d])) * _dot(of_ref[...], wf_ref[...]))
    y = _dot(merged.astype(MXU_DT), wo_ref[...])
    o_ref[...] = _layer_norm(alpha * x + y, g_ref[...], b_ref[...])


def mixer_out(x2, o_nsa, o_mla, o_fox, wg, wn, wm, wf, wo, ln_g, ln_b, alpha, tm=512):
    t, d = x2.shape
    row = lambda i: (i, 0)
    fixed = lambda i: (0, 0)
    resident = lambda a: pl.BlockSpec(a.shape, fixed, pipeline_mode=pl.Buffered(1))
    return pl.pallas_call(
        functools.partial(_mix_kernel, alpha=alpha),
        out_shape=jax.ShapeDtypeStruct((t, d), F32),
        grid=(t // tm,),
        in_specs=[pl.BlockSpec((tm, d), row),
                  pl.BlockSpec((tm, NSA_W), row),
                  pl.BlockSpec((tm, MLA_W), row),
                  pl.BlockSpec((tm, FOX_W), row),
                  resident(wg), resident(wn), resident(wm), resident(wf), resident(wo),
                  pl.BlockSpec((1, d), fixed), pl.BlockSpec((1, d), fixed)],
        out_specs=pl.BlockSpec((tm, d), row),
        compiler_params=_params("parallel"),
        name="mixer_out",
    )(x2, o_nsa, o_mla, o_fox, wg, wn, wm, wf, wo, ln_g, ln_b)


def _matmul_kernel(x_ref, w_ref, o_ref):
    o_ref[...] = _dot(x_ref[...].astype(MXU_DT), w_ref[...]).astype(o_ref.dtype)


def matmul(x2, w, out_dtype, tm):
    t, d = x2.shape
    n = w.shape[1]
    return pl.pallas_call(
        _matmul_kernel,
        out_shape=jax.ShapeDtypeStruct((t, n), out_dtype),
        grid=(t // tm,),
        in_specs=[pl.BlockSpec((tm, d), lambda i: (i, 0)), pl.BlockSpec((d, n), lambda i: (0, 0))],
        out_specs=pl.BlockSpec((tm, n), lambda i: (i, 0)),
        compiler_params=_params("parallel"),
        name="mem_kv_proj",
    )(x2, w)


def _xattn_kernel(x_ref, wq_ref, kv_ref, wo_ref, g_ref, b_ref, o_ref, *, alpha):
    x = x_ref[0]
    xb = x.astype(MXU_DT)
    half = x.shape[0] // 2
    q = jnp.concatenate([_dot(xb[:half], wq_ref[...]), _dot(xb[half:], wq_ref[...])], axis=0)
    q = (q * LOG2E).astype(MXU_DT)
    s = [_dot_nt(kv_ref[0, :, h * DH:(h + 1) * DH], q[:, h * DH:(h + 1) * DH]) for h in range(XA_HEADS)]
    outs = []
    for h in range(XA_HEADS):
        v = kv_ref[0, :, XA_W + h * DH:XA_W + (h + 1) * DH]
        p = jnp.exp2(s[h] - jnp.max(s[h], axis=0, keepdims=True)).astype(MXU_DT)
        acc = lax.dot_general(_with_sums(v), p, (((0,), (0,)), ((), ())), preferred_element_type=F32)
        outs.append(acc[:DH] * (1.0 / acc[DH:DH + 1]))
    o = jnp.concatenate(outs, axis=0).T.astype(MXU_DT)
    y = _dot(o, wo_ref[...])
    o_ref[0] = _layer_norm(alpha * x + y, g_ref[...], b_ref[...])


def cross_attention(x, kv, wq, wo, ln_g, ln_b, alpha, tm=512):
    b, s, d = x.shape
    m = kv.shape[1]
    fixed = lambda bi, i: (0, 0)
    return pl.pallas_call(
        functools.partial(_xattn_kernel, alpha=alpha),
        out_shape=jax.ShapeDtypeStruct((b, s, d), F32),
        grid=(b, s // tm),
        in_specs=[pl.BlockSpec((1, tm, d), lambda bi, i: (bi, i, 0)),
                  pl.BlockSpec(wq.shape, fixed),
                  pl.BlockSpec((1, m, 2 * XA_W), lambda bi, i: (bi, 0, 0)),
                  pl.BlockSpec(wo.shape, fixed),
                  pl.BlockSpec((1, d), fixed), pl.BlockSpec((1, d), fixed)],
        out_specs=pl.BlockSpec((1, tm, d), lambda bi, i: (bi, i, 0)),
        compiler_params=_params("parallel", "parallel"),
        name="cross_attn",
    )(x, wq, kv, wo, ln_g, ln_b)


def _mlp_kernel(x_ref, wu_ref, wd_ref, g_ref, b_ref, o_ref, *, alpha):
    x = x_ref[...]
    hdn = jnp.square(jnp.maximum(_dot(x.astype(MXU_DT), wu_ref[...]), 0.0))
    y = _dot(hdn.astype(MXU_DT), wd_ref[...])
    o_ref[...] = _layer_norm(alpha * x + y, g_ref[...], b_ref[...])


def mlp(x2, wu, wd, ln_g, ln_b, alpha, tm=512):
    t, d = x2.shape
    row = lambda i: (i, 0)
    fixed = lambda i: (0, 0)
    resident = lambda a: pl.BlockSpec(a.shape, fixed, pipeline_mode=pl.Buffered(1))
    return pl.pallas_call(
        functools.partial(_mlp_kernel, alpha=alpha),
        out_shape=jax.ShapeDtypeStruct((t, d), F32),
        grid=(t // tm,),
        in_specs=[pl.BlockSpec((tm, d), row), resident(wu), resident(wd),
                  pl.BlockSpec((1, d), fixed), pl.BlockSpec((1, d), fixed)],
        out_specs=pl.BlockSpec((tm, d), row),
        compiler_params=_params("parallel"),
        name="mlp",
    )(x2, wu, wd, ln_g, ln_b)


def _segment_cols(w, cols_per_head, head_offset, width=LANES):
    out = jnp.zeros(w.shape[:-1] + (len(cols_per_head) * width,), w.dtype)
    for h, cols in enumerate(cols_per_head):
        start = h * width + head_offset
        out = out.at[..., start:start + len(cols)].set(w[..., np.asarray(cols)])
    return out


def _prep_layer_weights(seq, w_in, cmp_pe, cmp_w1, cmp_w2, mla_q_norm, mla_w_uq, mla_kv_norm, mla_w_ukv,
                        fox_b_f, w_gate, w_br_nsa, w_br_mla, w_br_fox, w_mix_out, xa_w_q, xa_w_kv,
                        xa_w_o, mlp_w_up, mlp_w_down, ln_g, ln_b):
    depth, d, _ = w_in.shape
    sizes = (NSA_W, NSA_KV, NSA_KV, NSA_KV, NSA_KV, NSA_KV, NSA_KV, 3 * NSA_HEADS,
             MLA_Q_LORA, MLA_KV_LORA, MLA_ROPE, FOX_W, FOX_W, FOX_W, FOX_HEADS)
    offs = np.concatenate([[0], np.cumsum(sizes)])
    col = lambda n: w_in[:, :, offs[n]:offs[n + 1]]
    (nq, nkc, nvc, nks, nvs, nkw, nvw, ngate, cq, ckv, kr, fq, fk, fv, ff) = [col(n) for n in range(15)]
    qk_scale = DH ** -0.5
    wa = jnp.concatenate([nq * qk_scale, fq * qk_scale, fv, nvs, nkw, nvw], axis=2)
    pad = jnp.zeros((depth, d, _NB - (_OFF_FF + FOX_HEADS)), F32)
    wb = jnp.concatenate([cq, ckv, kr, ngate, ff, pad], axis=2)
    wc = jnp.concatenate([nkc, nvc], axis=2)
    wks = _segment_cols(nks, [np.arange(g * DH, (g + 1) * DH) for g in range(NSA_GROUPS)], 0,
                        DH + seq // L_SLC)
    wfk = _segment_cols(fk, [np.arange(h * DH, (h + 1) * DH) for h in range(FOX_HEADS)], 0)
    ff_lane = _OFF_FF - _COL_MISC * LANES
    b_row = jnp.zeros((depth, 1, LANES), F32).at[:, 0, ff_lane:ff_lane + FOX_HEADS].set(fox_b_f)

    dq = MLA_NOPE + MLA_ROPE
    half = MLA_ROPE // 2
    heads = range(MLA_HEADS)
    nope = [np.arange(h * dq, h * dq + MLA_NOPE) for h in heads]
    x1 = [np.arange(h * dq + MLA_NOPE, h * dq + MLA_NOPE + half) for h in heads]
    x2 = [c + half for c in x1]
    wqa = (_segment_cols(mla_w_uq, nope, 0) + _segment_cols(mla_w_uq, x1, MLA_NOPE)
           + _segment_cols(mla_w_uq, x2, MLA_NOPE + half))
    wqb = _segment_cols(mla_w_uq, x2, MLA_NOPE) + _segment_cols(mla_w_uq, x1, MLA_NOPE + half)
    dkv = MLA_NOPE + MLA_V
    wk = _segment_cols(mla_w_ukv, [np.arange(h * dkv, h * dkv + MLA_NOPE) for h in heads], 0)
    wv = mla_w_ukv[:, :, np.concatenate([np.arange(h * dkv + MLA_NOPE, (h + 1) * dkv) for h in heads])]
    pa = np.zeros((MLA_ROPE, MLA_HEADS * LANES), np.float32)
    pb = np.zeros((MLA_ROPE, MLA_HEADS * LANES), np.float32)
    for h in heads:
        for j in range(MLA_ROPE):
            pa[j, h * LANES + MLA_NOPE + j] = 1.0
            pb[j, h * LANES + MLA_NOPE + (j + half) % MLA_ROPE] = 1.0
    stack = lambda top, place: jnp.concatenate(
        [top, jnp.broadcast_to(jnp.asarray(np.concatenate([place, place])), (depth,) + (2 * MLA_ROPE, place.shape[1]))],
        axis=1)
    wka = stack(wk, pa)
    wkb = stack(jnp.zeros_like(wk), pb)

    eye_g = jnp.eye(NSA_GROUPS, dtype=F32)
    w1r = cmp_w1.reshape(depth, 2, L_CMP, DH, CMP_HIDDEN)
    def chunk_weights(w1_half):
        blk = jnp.einsum('ealdj,gh->ealgdhj', w1_half, eye_g)
        return blk.reshape(depth, 2, D_CMP * NSA_KV, NSA_GROUPS * CMP_HIDDEN)
    w1lo = chunk_weights(w1r[:, :, :D_CMP])
    w1hi = chunk_weights(w1r[:, :, D_CMP:])
    w2b = jnp.einsum('eajd,gh->eagjhd', cmp_w2, eye_g).reshape(depth, 2, NSA_GROUPS * CMP_HIDDEN, NSA_KV)
    pe = jnp.broadcast_to(cmp_pe.reshape(depth, 2, 2, D_CMP, 1, DH),
                          (depth, 2, 2, D_CMP, NSA_GROUPS, DH)).reshape(depth, 2, 2, D_CMP * NSA_KV)

    c = lambda a: a.astype(MXU_DT)
    return dict(
        wa=c(wa), wb=c(wb), wc=c(wc), wks=c(wks), wfk=c(wfk),
        pe=pe, w1lo=c(w1lo), w1hi=c(w1hi), w2b=c(w2b),
        q_norm=mla_q_norm[:, None, :], wqa=c(wqa), wqb=c(wqb),
        kv_norm=mla_kv_norm[:, None, :], wka=c(wka), wkb=c(wkb), wv=c(wv),
        b_row=b_row, wg=c(w_gate), wn=c(w_br_nsa), wm=c(w_br_mla), wf=c(w_br_fox), wo=c(w_mix_out),
        xq=c(xa_w_q * qk_scale), xkv=c(xa_w_kv), xo=c(xa_w_o),
        wu=c(mlp_w_up), wd=c(mlp_w_down),
        ln_g=ln_g[:, :, None, :], ln_b=ln_b[:, :, None, :],
    )


def _layer(x, mem2, w, tabs, alpha):
    b, s, d = x.shape
    t = b * s
    x2 = x.reshape(t, d)

    ya, yb, ykc, yvc, yks, yfk = in_proj(x2, w, tabs, s)
    ya3 = ya.reshape(b, s, _NA)
    yb3 = yb.reshape(b, s, _NB)

    nc = s // D_CMP
    kc = compress(ykc.reshape(b, nc, D_CMP * NSA_KV), w["pe"][0], w["w1lo"][0], w["w1hi"][0], w["w2b"][0])
    vc = compress(yvc.reshape(b, nc, D_CMP * NSA_KV), w["pe"][1], w["w1lo"][1], w["w1hi"][1], w["w2b"][1])
    o_nsa = nsa_attention(ya3, yb3, kc, vc, yks.reshape(b, s, -1),
                          tabs["bias_c"], tabs["bias_w"], tabs["overlap_t"])

    q_mla, k_mla, v_mla = mla_up(yb, w, tabs, s)
    wide = MLA_HEADS * LANES
    o_mla = causal_attention(q_mla.reshape(b, s, wide), 0, k_mla.reshape(b, s, wide),
                             v_mla.reshape(b, s, MLA_W), 0, MLA_HEADS, LANES)

    k_decay = fox_decay(yb3, w["b_row"], tabs["decay_place"])
    o_fox = causal_attention(ya3, _COL_FQ, yfk.reshape(b, s, FOX_HEADS * LANES), ya3, _COL_FQ + 1,
                             FOX_HEADS, DH, k_decay=k_decay)

    x2 = mixer_out(x2, o_nsa.reshape(t, NSA_W), o_mla.reshape(t, MLA_W), o_fox.reshape(t, FOX_W),
                   w["wg"], w["wn"], w["wm"], w["wf"], w["wo"], w["ln_g"][0], w["ln_b"][0], alpha)

    m = mem2.shape[0] // b
    kv = matmul(mem2, w["xkv"], MXU_DT, tm=min(512, mem2.shape[0])).reshape(b, m, 2 * XA_W)
    x3 = cross_attention(x2.reshape(b, s, d), kv, w["xq"], w["xo"], w["ln_g"][1], w["ln_b"][1], alpha)

    x2 = mlp(x3.reshape(t, d), w["wu"], w["wd"], w["ln_g"][2], w["ln_b"][2], alpha)
    return x2.reshape(b, s, d)


def _tables(t5_table, s):
    tq = NSA_TQ
    ki = jnp.arange(tq)[:, None]
    qi = jnp.arange(tq)[None, :]
    last = jnp.full((tq, tq), T5_BUCKETS - 1, jnp.int32)
    masked = jnp.full((tq, tq), -1, jnp.int32)
    kinds = [None] * N_TILE_KINDS
    kinds[TILE_DIAG] = jnp.where(qi >= ki, _t5_bucket(qi - ki), -1)
    kinds[TILE_PREV] = _t5_bucket(tq + qi - ki)
    kinds[TILE_FAR] = last
    kinds[TILE_EDGE] = jnp.where(qi < ki, last, masked)
    kinds[TILE_NONE] = masked
    kinds[TILE_SEL_DIAG] = kinds[TILE_DIAG]
    kinds[TILE_SEL_PREV] = kinds[TILE_PREV]
    kinds[TILE_ZERO] = last
    relative = [int(k in (TILE_SEL_DIAG, TILE_SEL_PREV, TILE_ZERO)) for k in range(N_TILE_KINDS)]
    idx_w = jnp.concatenate(kinds, axis=0).astype(jnp.int32)
    rel_w = jnp.concatenate([jnp.full((tq, tq), r, jnp.int32) for r in relative], axis=0)
    bias_w = bias_table(t5_table, idx_w, rel_w, tq).reshape(NSA_HEADS, N_TILE_KINDS, tq, tq)

    nc = s // D_CMP
    cmp_end = jnp.arange(nc) * D_CMP + (L_CMP - 1)
    dist_c = jnp.arange(s)[None, :] - cmp_end[:, None]
    idx_c = jnp.where(dist_c >= 0, _t5_bucket(dist_c), -1).astype(jnp.int32)
    bias_c = bias_table(t5_table, idx_c, jnp.zeros_like(idx_c), min(64, nc))

    n_slc = s // L_SLC
    block_onehot = (np.arange(s)[:, None] // L_SLC == np.arange(n_slc)[None, :]).astype(np.float32)
    block_cols = np.concatenate([np.zeros((s, DH), np.float32), block_onehot] * NSA_GROUPS, axis=1)
    ff_lane = _OFF_FF - _COL_MISC * LANES
    decay_place = np.zeros((3, LANES, FOX_HEADS * LANES), np.float32)
    for j in range(3):
        for h in range(FOX_HEADS):
            decay_place[j, ff_lane + h, h * LANES + DH + j] = 1.0
    c_lo = np.arange(nc)[:, None] * D_CMP
    s_lo = np.arange(n_slc)[None, :] * L_SLC
    overlap = np.maximum(np.minimum(c_lo + L_CMP, s_lo + L_SLC) - np.maximum(c_lo, s_lo), 0) / D_CMP
    overlap[nc - 1, :] = 0.0

    half = MLA_ROPE // 2
    inv = ROPE_THETA ** (-jnp.arange(half, dtype=F32) / half)
    ang = jnp.arange(s).astype(F32)[:, None] * inv[None, :]
    cos, sin = jnp.cos(ang), jnp.sin(ang)
    tail = jnp.zeros((s, LANES - MLA_NOPE - MLA_ROPE), F32)
    seg_c = jnp.concatenate([jnp.ones((s, MLA_NOPE), F32), cos, cos, tail], axis=1)
    seg_s = jnp.concatenate([jnp.zeros((s, MLA_NOPE), F32), -sin, sin, tail], axis=1)
    col_scale = np.ones((1, _NA), np.float32)
    col_scale[0, :NSA_W] = LOG2E
    col_scale[0, _COL_FQ * FOX_W:(_COL_FQ + 1) * FOX_W] = LOG2E
    return dict(bias_w=bias_w, bias_c=bias_c, overlap_t=jnp.asarray(overlap.T, MXU_DT),
                block_cols=jnp.asarray(block_cols), decay_place=jnp.asarray(decay_place, MXU_DT),
                rope_cos=jnp.tile(seg_c, (1, MLA_HEADS)), rope_sin=jnp.tile(seg_s, (1, MLA_HEADS)),
                col_scale=jnp.asarray(col_scale))


def kernel(x, mem, w_in, cmp_pe, cmp_w1, cmp_w2, t5_table, mla_q_norm, mla_w_uq, mla_kv_norm, mla_w_ukv, fox_b_f, w_gate, w_br_nsa, w_br_mla, w_br_fox, w_mix_out, xa_w_q, xa_w_kv, xa_w_o, mlp_w_up, mlp_w_down, ln_g, ln_b):
    depth = w_in.shape[0]
    b, s, d = x.shape
    alpha = (2 * depth) ** 0.25
    weights = _prep_layer_weights(s, w_in, cmp_pe, cmp_w1, cmp_w2, mla_q_norm, mla_w_uq, mla_kv_norm,
                                  mla_w_ukv, fox_b_f, w_gate, w_br_nsa, w_br_mla, w_br_fox, w_mix_out,
                                  xa_w_q, xa_w_kv, xa_w_o, mlp_w_up, mlp_w_down, ln_g, ln_b)
    tabs = _tables(t5_table, s)
    mem2 = mem.reshape(-1, d)

    def step(xc, w):
        return _layer(xc, mem2, w, tabs, alpha), None

    out, _ = lax.scan(step, x, weights)
    return out
```

```python
import functools
import math

import numpy as np
import jax
import jax.numpy as jnp
from jax import lax
from jax.experimental import pallas as pl
from jax.experimental.pallas import tpu as pltpu

D_MODEL = 1024
DH = 64
NSA_HEADS = 8
NSA_GROUPS = 2
NSA_R = NSA_HEADS // NSA_GROUPS
L_CMP = 32
D_CMP = 16
CMP_HIDDEN = 128
L_SLC = 64
N_SEL = 8
WINDOW = 512
MLA_HEADS = 4
MLA_NOPE = 64
MLA_ROPE = 32
MLA_V = 64
MLA_Q_LORA = 384
MLA_KV_LORA = 128
ROPE_THETA = 10000.0
FOX_HEADS = 4
XA_HEADS = 4
D_FF = 4 * D_MODEL
T5_BUCKETS = 32
T5_MAX_DIST = 128
NSA_W = NSA_HEADS * DH
NSA_KV = NSA_GROUPS * DH
MLA_W = MLA_HEADS * MLA_V
FOX_W = FOX_HEADS * DH
XA_W = XA_HEADS * DH
LN_EPS = 1e-5
RMS_EPS = 1e-6
NEG = -1e30
FORCE = 1e4
LOG2E = math.log2(math.e)

F32 = jnp.float32
MXU_DT = jnp.bfloat16
LANES = 128
NSA_TQ = 128
NSA_RQ = NSA_R * NSA_TQ
SEL_TK = 512
ATT_T = 512
VMEM_LIMIT = 48 * 1024 * 1024

TILE_DIAG = 0
TILE_PREV = 1
TILE_FAR = 2
TILE_EDGE = 3
TILE_NONE = 4
TILE_SEL_DIAG = 5
TILE_SEL_PREV = 6
TILE_ZERO = 7
N_TILE_KINDS = 8

_NA = NSA_W + 2 * FOX_W + 3 * NSA_KV
_COL_FQ = NSA_W // FOX_W
_COL_VS = (NSA_W + 2 * FOX_W) // NSA_KV
_OFF_CQ = 0
_OFF_CKV = MLA_Q_LORA
_OFF_KR = _OFF_CKV + MLA_KV_LORA
_OFF_NG = _OFF_KR + MLA_ROPE
_OFF_FF = _OFF_NG + 3 * NSA_HEADS
_NB = 640
_COL_MISC = _OFF_KR // LANES


def _dot(a, b):
    return jnp.dot(a, b, preferred_element_type=F32)


def _dot_nt(a, b):
    return lax.dot_general(a, b, (((1,), (1,)), ((), ())), preferred_element_type=F32)


def _params(*sem):
    return pltpu.CompilerParams(dimension_semantics=sem, vmem_limit_bytes=VMEM_LIMIT)


def _layer_norm(z, g, b):
    mu = jnp.mean(z, axis=-1, keepdims=True)
    zc = z - mu
    var = jnp.mean(zc * zc, axis=-1, keepdims=True)
    return zc * lax.rsqrt(var + LN_EPS) * g + b


def _rms_norm(z, g):
    return z * lax.rsqrt(jnp.mean(z * z, axis=-1, keepdims=True) + RMS_EPS) * g


def _t5_bucket(dist):
    n = jnp.maximum(dist, 0)
    max_exact = T5_BUCKETS // 2
    nf = jnp.maximum(n, 1).astype(F32)
    large = max_exact + (jnp.log(nf / max_exact) / math.log(T5_MAX_DIST / max_exact)
                         * (T5_BUCKETS - max_exact)).astype(jnp.int32)
    large = jnp.minimum(large, T5_BUCKETS - 1)
    return jnp.where(n < max_exact, n, large)


SUM_ROWS = 16
ACC_ROWS = DH + SUM_ROWS


def _with_ones(v):
    return jnp.concatenate([v, jnp.ones(v.shape, v.dtype)], axis=1)


def _with_sums(v):
    return jnp.concatenate([v, jnp.ones((v.shape[0], SUM_ROWS), v.dtype)], axis=1)


def _bias_table_kernel(t5_ref, idx_ref, rel_ref, o_ref):
    h = pl.program_id(0)
    idx = idx_ref[...]
    acc = jnp.full(idx.shape, NEG, F32)
    for b in range(T5_BUCKETS):
        acc = jnp.where(idx == b, t5_ref[b, h] * LOG2E, acc)
    o_ref[0] = acc - jnp.where(rel_ref[...] > 0, t5_ref[T5_BUCKETS - 1, h] * LOG2E, 0.0)


def bias_table(t5_table, idx, rel, tr):
    rows, cols = idx.shape
    return pl.pallas_call(
        _bias_table_kernel,
        out_shape=jax.ShapeDtypeStruct((NSA_HEADS, rows, cols), F32),
        grid=(NSA_HEADS, rows // tr),
        in_specs=[pl.BlockSpec(memory_space=pltpu.SMEM),
                  pl.BlockSpec((tr, cols), lambda h, i: (i, 0)),
                  pl.BlockSpec((tr, cols), lambda h, i: (i, 0))],
        out_specs=pl.BlockSpec((1, tr, cols), lambda h, i: (h, i, 0)),
        compiler_params=_params("parallel", "parallel"),
        name="bias_table",
    )(t5_table, idx, rel)


def _in_proj_kernel(x_ref, wa_ref, wb_ref, wc_ref, wks_ref, wfk_ref, sc_ref, blk_ref,
                    ya_ref, yb_ref, ykc_ref, yvc_ref, yks_ref, yfk_ref):
    xb = x_ref[...].astype(MXU_DT)
    ya_ref[...] = (_dot(xb, wa_ref[...]) * sc_ref[...]).astype(ya_ref.dtype)
    yb_ref[...] = _dot(xb, wb_ref[...])
    yc = _dot(xb, wc_ref[...])
    ykc_ref[...] = yc[:, :NSA_KV]
    yvc_ref[...] = yc[:, NSA_KV:]
    yks_ref[...] = (_dot(xb, wks_ref[...]) + blk_ref[...]).astype(yks_ref.dtype)
    yfk_ref[...] = _dot(xb, wfk_ref[...]).astype(yfk_ref.dtype)


def in_proj(x2, w, tabs, seq, tm=512):
    t, d = x2.shape
    nps = seq // tm
    row = lambda i: (i, 0)
    fixed = lambda i: (0, 0)
    weights = [w["wa"], w["wb"], w["wc"], w["wks"], w["wfk"]]
    widths = [(_NA, MXU_DT), (_NB, F32), (NSA_KV, F32), (NSA_KV, F32),
              (w["wks"].shape[1], MXU_DT), (w["wfk"].shape[1], MXU_DT)]
    return pl.pallas_call(
        _in_proj_kernel,
        out_shape=tuple(jax.ShapeDtypeStruct((t, n), dt) for n, dt in widths),
        grid=(t // tm,),
        in_specs=[pl.BlockSpec((tm, d), row)] + [pl.BlockSpec(a.shape, fixed) for a in weights]
        + [pl.BlockSpec((1, _NA), fixed),
           pl.BlockSpec((tm, tabs["block_cols"].shape[1]), lambda i: (i % nps, 0))],
        out_specs=tuple(pl.BlockSpec((tm, n), row) for n, _ in widths),
        compiler_params=_params("parallel"),
        name="in_proj",
    )(x2, *weights, tabs["col_scale"], tabs["block_cols"])


def _mla_up_kernel(yb_ref, qn_ref, wqa_ref, wqb_ref, kvn_ref, wka_ref, wkb_ref, wv_ref,
                   ct_ref, st_ref, q_ref, k_ref, v_ref):
    scale = (MLA_NOPE + MLA_ROPE) ** -0.5 * LOG2E
    ct = ct_ref[...]
    st = st_ref[...]
    cq = _rms_norm(yb_ref[:, _OFF_CQ:_OFF_CQ + MLA_Q_LORA], qn_ref[...]).astype(MXU_DT)
    q = _dot(cq, wqa_ref[...]) * ct + _dot(cq, wqb_ref[...]) * st
    q_ref[...] = (q * scale).astype(q_ref.dtype)
    ckv = _rms_norm(yb_ref[:, _OFF_CKV:_OFF_CKV + MLA_KV_LORA], kvn_ref[...]).astype(MXU_DT)
    v_ref[...] = _dot(ckv, wv_ref[...]).astype(v_ref.dtype)
    kr = yb_ref[:, _OFF_KR:_OFF_KR + MLA_ROPE]
    kr_hi = kr.astype(MXU_DT)
    kr_lo = (kr - kr_hi.astype(F32)).astype(MXU_DT)
    lhs = jnp.concatenate([ckv, kr_hi, kr_lo], axis=1)
    k_ref[...] = (_dot(lhs, wka_ref[...]) * ct + _dot(lhs, wkb_ref[...]) * st).astype(k_ref.dtype)


def mla_up(yb, w, tabs, seq, tm=512):
    t = yb.shape[0]
    nps = seq // tm
    row = lambda i: (i, 0)
    fixed = lambda i: (0, 0)
    pos = lambda i: (i % nps, 0)
    wide = MLA_HEADS * LANES
    ins = [w["q_norm"], w["wqa"], w["wqb"], w["kv_norm"], w["wka"], w["wkb"], w["wv"]]
    return pl.pallas_call(
        _mla_up_kernel,
        out_shape=(jax.ShapeDtypeStruct((t, wide), MXU_DT), jax.ShapeDtypeStruct((t, wide), MXU_DT),
                   jax.ShapeDtypeStruct((t, MLA_W), MXU_DT)),
        grid=(t // tm,),
        in_specs=[pl.BlockSpec((tm, _NB), row)] + [pl.BlockSpec(a.shape, fixed) for a in ins]
        + [pl.BlockSpec((tm, wide), pos), pl.BlockSpec((tm, wide), pos)],
        out_specs=(pl.BlockSpec((tm, wide), row), pl.BlockSpec((tm, wide), row),
                   pl.BlockSpec((tm, MLA_W), row)),
        compiler_params=_params("parallel"),
        name="mla_up",
    )(yb, *ins, tabs["rope_cos"], tabs["rope_sin"])


def _bf16_head(x):
    bits = lax.bitcast_convert_type(x, jnp.uint32) & jnp.uint32(0xFFFF0000)
    return lax.bitcast_convert_type(bits, F32)


def _fox_decay_kernel(f_ref, b_ref, place_ref, o_ref):
    z = f_ref[0] + b_ref[...]
    x = jnp.minimum(z, 0.0) - jnp.log1p(jnp.exp(-jnp.abs(z)))
    n = x.shape[0]
    row = lax.broadcasted_iota(jnp.int32, x.shape, 0)
    shift = 1
    while shift < n:
        x = x + jnp.where(row >= shift, pltpu.roll(x, shift, 0), 0.0)
        shift *= 2
    dec = -(x * LOG2E)
    hi = _bf16_head(dec)
    rest = dec - hi
    mid = _bf16_head(rest)
    terms = (hi, mid, rest - mid)
    o_ref[0] = sum(_dot(t.astype(MXU_DT), place_ref[j]) for j, t in enumerate(terms)).astype(o_ref.dtype)


def fox_decay(misc, b_row, place):
    b, s, _ = misc.shape
    wide = place.shape[2]
    return pl.pallas_call(
        _fox_decay_kernel,
        out_shape=jax.ShapeDtypeStruct((b, s, wide), MXU_DT),
        grid=(b,),
        in_specs=[pl.BlockSpec((1, s, LANES), lambda bi: (bi, 0, _COL_MISC)),
                  pl.BlockSpec((1, LANES), lambda bi: (0, 0)),
                  pl.BlockSpec(place.shape, lambda bi: (0, 0, 0))],
        out_specs=pl.BlockSpec((1, s, wide), lambda bi: (bi, 0, 0)),
        compiler_params=_params("parallel"),
        name="fox_decay",
    )(misc, b_row, place)


def _keys_softmax_update(m, acc, s, v_aug):
    m_new = jnp.maximum(m, jnp.max(s, axis=0, keepdims=True))
    a = jnp.exp2(m - m_new)
    p = jnp.exp2(s - m_new).astype(MXU_DT)
    pv = lax.dot_general(v_aug, p, (((0,), (0,)), ((), ())), preferred_element_type=F32)
    return m_new, a * acc + pv


def _keys_softmax_init(queries):
    return jnp.full((1, queries), NEG, F32), jnp.zeros((ACC_ROWS, queries), F32)


def _softmax_piece(s, v_aug):
    m = jnp.max(s, axis=0, keepdims=True)
    p = jnp.exp2(s - m).astype(MXU_DT)
    return m, lax.dot_general(v_aug, p, (((0,), (0,)), ((), ())), preferred_element_type=F32)


def _softmax_merge(state, pieces):
    m_old, acc = state
    m = m_old
    for mp, _ in pieces:
        m = jnp.maximum(m, mp)
    acc = acc * jnp.exp2(m_old - m)
    for mp, ap in pieces:
        acc = acc + ap * jnp.exp2(mp - m)
    return m, acc


def _causal_attn_kernel(*refs, n_heads, dq, decay):
    if decay:
        q_ref, k_ref, kd_ref, v_ref, o_ref = refs
    else:
        q_ref, k_ref, v_ref, o_ref = refs
    qi = pl.program_id(1)
    t = q_ref.shape[1]
    key = lax.broadcasted_iota(jnp.int32, (t, t), 0)
    qry = lax.broadcasted_iota(jnp.int32, (t, t), 1)

    qs = []
    for h in range(n_heads):
        q = q_ref[0, :, h * dq:(h + 1) * dq]
        qs.append(_with_ones(q) if decay else q)

    def scores(rows, h):
        k = k_ref[0, rows, h * LANES:(h + 1) * LANES]
        if decay:
            k = k + kd_ref[0, rows, h * LANES:(h + 1) * LANES]
        return _dot_nt(k, qs[h])

    def values(rows, h):
        return _with_sums(v_ref[0, rows, h * DH:(h + 1) * DH])

    def sweep(rows, state, mask):
        s = [scores(rows, h) for h in range(n_heads)]
        if mask is not None:
            s = [jnp.where(mask, sh, NEG) for sh in s]
        return tuple(_keys_softmax_update(*state[h], s[h], values(rows, h)) for h in range(n_heads))

    n_double = lax.shift_right_logical(qi, 1)
    state = lax.fori_loop(
        0, n_double, lambda p, st: sweep(pl.ds(pl.multiple_of(p * 2 * t, 2 * t), 2 * t), st, None),
        tuple(_keys_softmax_init(t) for _ in range(n_heads)))
    state = lax.fori_loop(
        0, qi & 1, lambda _, st: sweep(pl.ds(pl.multiple_of(n_double * 2 * t, t), t), st, None), state)
    state = sweep(pl.ds(pl.multiple_of(qi * t, t), t), state, key <= qry)
    outs = [acc[:DH] * (1.0 / acc[DH:DH + 1]) for _, acc in state]
    for j in range(n_heads // 2):
        slab = jnp.concatenate(outs[2 * j:2 * j + 2], axis=0).T
        o_ref[0, :, 2 * j * DH:(2 * j + 2) * DH] = slab.astype(o_ref.dtype)


def causal_attention(q, q_col, k, v, v_col, n_heads, dq, k_decay=None, t=ATT_T):
    b, s, _ = q.shape
    whole = pl.BlockSpec((1, s, n_heads * LANES), lambda bi, qi: (bi, 0, 0))
    keys = [k] if k_decay is None else [k, k_decay]
    return pl.pallas_call(
        functools.partial(_causal_attn_kernel, n_heads=n_heads, dq=dq, decay=k_decay is not None),
        out_shape=jax.ShapeDtypeStruct((b, s, n_heads * DH), MXU_DT),
        grid=(b, s // t),
        in_specs=[pl.BlockSpec((1, t, n_heads * dq), lambda bi, qi: (bi, qi, q_col))]
        + [whole] * len(keys)
        + [pl.BlockSpec((1, s, n_heads * DH), lambda bi, qi: (bi, 0, v_col))],
        out_specs=pl.BlockSpec((1, t, n_heads * DH), lambda bi, qi: (bi, qi, 0)),
        compiler_params=_params("parallel", "arbitrary"),
        name="causal_attn_decay" if k_decay is not None else "causal_attn",
    )(q, *keys, v)


def _compress_kernel(x_ref, pe_ref, wlo_ref, whi_ref, w2_ref, o_ref):
    x = x_ref[0]
    nc = x.shape[0]
    lo = _dot((x + pe_ref[0:1, :]).astype(MXU_DT), wlo_ref[...])
    hi = _dot((x + pe_ref[1:2, :]).astype(MXU_DT), whi_ref[...])
    hid = lo + pltpu.roll(hi, nc - 1, 0)
    act = jax.nn.gelu(hid)
    o_ref[0] = _dot(act.astype(MXU_DT), w2_ref[...]).astype(o_ref.dtype)


def compress(x, pe, wlo, whi, w2):
    b, nc, width = x.shape
    fixed = lambda bi: (0, 0)
    return pl.pallas_call(
        _compress_kernel,
        out_shape=jax.ShapeDtypeStruct((b, nc, NSA_KV), MXU_DT),
        grid=(b,),
        in_specs=[pl.BlockSpec((1, nc, width), lambda bi: (bi, 0, 0)),
                  pl.BlockSpec(pe.shape, fixed), pl.BlockSpec(wlo.shape, fixed),
                  pl.BlockSpec(whi.shape, fixed), pl.BlockSpec(w2.shape, fixed)],
        out_specs=pl.BlockSpec((1, nc, NSA_KV), lambda bi: (bi, 0, 0)),
        compiler_params=_params("parallel"),
        name="nsa_compress",
    )(x, pe, wlo, whi, w2)


def _nsa_kernel(q_ref, kc_ref, vc_ref, ks_ref, vs_ref, kw_ref, vw_ref, misc_ref,
                bc_ref, bw_ref, ovt_ref, o_ref, *, n_sel):
    i = pl.program_id(1)
    tq, rq = NSA_TQ, NSA_RQ
    nslc = ovt_ref.shape[0]
    seg = DH + nslc
    nwt = WINDOW // tq + 1
    sub = SEL_TK // tq
    t0 = i * tq
    g_off = _OFF_NG - _COL_MISC * LANES
    gates_t = jax.nn.sigmoid(misc_ref[0]).T

    lane = lax.broadcasted_iota(jnp.int32, (1, rq), 1)
    has_cmp = (t0 + (lane & (tq - 1))) >= (L_CMP - 1)

    blk = lax.broadcasted_iota(jnp.int32, (nslc, tq), 0)
    blk_f = blk.astype(F32)
    tpos = lax.broadcasted_iota(jnp.int32, (nslc, tq), 1) + t0
    cur = lax.shift_right_logical(tpos, int(math.log2(L_SLC)))
    forced = (blk == 0) | (blk == cur) | (blk == cur - 1)
    causal_blk = blk * L_SLC <= tpos
    eye = (lax.broadcasted_iota(jnp.int32, (tq, tq), 0)
           == lax.broadcasted_iota(jnp.int32, (tq, tq), 1)).astype(MXU_DT)

    def window_tile_kind(d):
        static = {0: TILE_DIAG, 1: TILE_PREV, nwt - 1: TILE_EDGE}.get(d, TILE_FAR)
        return jnp.where(i - d < 0, TILE_NONE, static)

    def group_cols(g):
        return slice(g * DH, (g + 1) * DH)

    groups = range(NSA_GROUPS)
    q_groups = [jnp.concatenate([q_ref[0, :, (g * NSA_R + r) * DH:(g * NSA_R + r + 1) * DH]
                                 for r in range(NSA_R)], axis=0) for g in groups]

    def group_lanes(tile_of_head, g):
        return jnp.concatenate([tile_of_head(g * NSA_R + r) for r in range(NSA_R)], axis=1)

    def bias_tile(g, kind):
        return group_lanes(lambda h: bw_ref[h, kind], g)

    s_cmp = [_dot_nt(kc_ref[0, :, group_cols(g)], q_groups[g]) + group_lanes(lambda h: bc_ref[h], g)
             for g in groups]
    win_rows = [pl.ds(pl.multiple_of(jnp.maximum(i - d, 0) * tq, tq), tq) for d in range(nwt - 1, -1, -1)]
    s_win = []
    for g in groups:
        kcat = jnp.concatenate([kw_ref[0, r, group_cols(g)] for r in win_rows], axis=0)
        bias = jnp.concatenate([bias_tile(g, window_tile_kind(d)) for d in range(nwt - 1, -1, -1)], axis=0)
        s_win.append(_dot_nt(kcat, q_groups[g]) + bias)

    def gate_row(g, j):
        return jnp.concatenate(
            [gates_t[g_off + 3 * (g * NSA_R + r) + j:g_off + 3 * (g * NSA_R + r) + j + 1] for r in range(NSA_R)],
            axis=1)

    def weighted_values(v, p):
        return lax.dot_general(_with_sums(v), p, (((0,), (0,)), ((), ())), preferred_element_type=F32)

    o_cmp, imp = [], []
    for g in groups:
        e = jnp.exp2(s_cmp[g] - jnp.max(s_cmp[g], axis=0, keepdims=True)).astype(MXU_DT)
        acc = weighted_values(vc_ref[0, :, group_cols(g)], e)
        inv = jnp.where(has_cmp, 1.0 / acc[DH:DH + 1], 0.0)
        o_cmp.append(acc[:DH] * inv)
        imp4 = _dot(ovt_ref[...], e) * inv
        imp.append(sum(imp4[:, r * tq:(r + 1) * tq] for r in range(1, NSA_R)) + imp4[:, 0:tq])

    o_fixed = []
    for g in groups:
        vcat = jnp.concatenate([vw_ref[0, r, group_cols(g)] for r in win_rows], axis=0)
        p = jnp.exp2(s_win[g] - jnp.max(s_win[g], axis=0, keepdims=True)).astype(MXU_DT)
        acc = weighted_values(vcat, p)
        o_win = acc[:DH] * (1.0 / acc[DH:DH + 1])
        o_fixed.append(gate_row(g, 0) * o_cmp[g] + gate_row(g, 2) * o_win)

    q_aug = []
    for g in groups:
        x = jnp.where(causal_blk, imp[g] + jnp.where(forced, FORCE, 0.0), NEG)
        sel_t = jnp.zeros((nslc, tq), F32)
        for _ in range(n_sel):
            mx = jnp.max(x, axis=0, keepdims=True)
            first = jnp.min(jnp.where(x == mx, blk_f, float(nslc)), axis=0, keepdims=True)
            hit = blk_f == first
            sel_t = jnp.where(hit, 1.0, sel_t)
            x = jnp.where(hit, -3e38, x)
        penalty = ((1.0 - _dot_nt(eye, sel_t.astype(MXU_DT))) * NEG).astype(MXU_DT)
        q_aug.append(jnp.concatenate([q_groups[g], jnp.concatenate([penalty] * NSA_R, axis=0)], axis=1))

    def sel_scores(rows, g):
        return _dot_nt(ks_ref[0, rows, g * seg:(g + 1) * seg], q_aug[g])

    def sel_values(rows, g):
        return _with_sums(vs_ref[0, rows, group_cols(g)])

    def far_sweep(start, n_tiles, state):
        tiles = [pl.ds(pl.multiple_of(start + j * SEL_TK, SEL_TK), SEL_TK) for j in range(n_tiles)]
        s = [[sel_scores(rows, g) for rows in tiles] for g in groups]
        pieces = [[_softmax_piece(s[g][j], sel_values(rows, g)) for j, rows in enumerate(tiles)]
                  for g in groups]
        return tuple(_softmax_merge(state[g], pieces[g]) for g in groups)

    last = i // sub
    n_far = jnp.maximum(last - 1, 0)
    n_double = lax.shift_right_logical(n_far, 1)
    state = lax.fori_loop(0, n_double, lambda p, st: far_sweep(p * 2 * SEL_TK, 2, st),
                          tuple(_keys_softmax_init(rq) for _ in groups))
    state = lax.fori_loop(0, n_far & 1, lambda _, st: far_sweep(n_double * 2 * SEL_TK, 1, st), state)
    near = []
    for kn in (last - 1, last):
        kt = pl.ds(pl.multiple_of(jnp.maximum(kn, 0) * SEL_TK, SEL_TK), SEL_TK)
        kinds = []
        for j in range(sub):
            d = i - (kn * sub + j)
            kind = jnp.where(d < 0, TILE_NONE, jnp.where(d == 0, TILE_SEL_DIAG,
                                                         jnp.where(d == 1, TILE_SEL_PREV, TILE_ZERO)))
            kinds.append(jnp.where(kn < 0, TILE_NONE, kind))
        for g in groups:
            bias = jnp.concatenate([bias_tile(g, kd) for kd in kinds], axis=0)
            near.append((kt, g, sel_scores(kt, g) + bias))
    pieces = [[] for _ in groups]
    for kt, g, s in near:
        pieces[g].append(_softmax_piece(s, sel_values(kt, g)))
    state = [_softmax_merge(state[g], pieces[g]) for g in groups]

    out_t = []
    for g in groups:
        acc = state[g][1]
        out_t.append(o_fixed[g] + gate_row(g, 1) * (acc[:DH] * (1.0 / acc[DH:DH + 1])))
    out_t = jnp.concatenate(out_t, axis=0)
    for r in range(NSA_R):
        slab = out_t[:, r * tq:(r + 1) * tq].T
        for g in groups:
            h = g * NSA_R + r
            o_ref[0, :, h * DH:(h + 1) * DH] = slab[:, g * DH:(g + 1) * DH].astype(o_ref.dtype)


def nsa_attention(ya, yb, kc, vc, ks_aug, bias_c, bias_w, overlap_t):
    b, s, _ = ya.shape
    tq = NSA_TQ
    ncp = kc.shape[1]
    per_b_kv = lambda col: pl.BlockSpec((1, s, NSA_KV), lambda bi, i: (bi, 0, col))
    per_b_cmp = pl.BlockSpec((1, ncp, NSA_KV), lambda bi, i: (bi, 0, 0))
    return pl.pallas_call(
        functools.partial(_nsa_kernel, n_sel=min(N_SEL, overlap_t.shape[0])),
        out_shape=jax.ShapeDtypeStruct((b, s, NSA_W), MXU_DT),
        grid=(b, s // tq),
        in_specs=[pl.BlockSpec((1, tq, NSA_W), lambda bi, i: (bi, i, 0)),
                  per_b_cmp, per_b_cmp,
                  pl.BlockSpec((1, s, ks_aug.shape[2]), lambda bi, i: (bi, 0, 0)),
                  per_b_kv(_COL_VS), per_b_kv(_COL_VS + 1), per_b_kv(_COL_VS + 2),
                  pl.BlockSpec((1, tq, LANES), lambda bi, i: (bi, i, _COL_MISC)),
                  pl.BlockSpec((NSA_HEADS, ncp, tq), lambda bi, i: (0, 0, i)),
                  pl.BlockSpec(bias_w.shape, lambda bi, i: (0, 0, 0, 0)),
                  pl.BlockSpec(overlap_t.shape, lambda bi, i: (0, 0))],
        out_specs=pl.BlockSpec((1, tq, NSA_W), lambda bi, i: (bi, i, 0)),
        compiler_params=_params("parallel", "arbitrary"),
        name="nsa_attn",
    )(ya, kc, vc, ks_aug, ya, ya, ya, yb, bias_c, bias_w, overlap_t)


def _mix_kernel(x_ref, on_ref, om_ref, of_ref, wg_ref, wn_ref, wm_ref, wf_ref, wo_ref,
                g_ref, b_ref, o_ref, *, alpha):
    x = x_ref[...]
    xb = x.astype(MXU_DT)
    d = x.shape[1]
    merged = (jax.nn.sigmoid(_dot(xb, wg_ref[:, 0:d])) * _dot(on_ref[...], wn_ref[...])
              + jax.nn.sigmoid(_dot(xb, wg_ref[:, d:2 * d])) * _dot(om_ref[...], wm_ref[...])
              + jax.nn.sigmoid(_dot(xb, wg_ref[:, 2 * d:3 * d])) * _dot(of_ref[...], wf_ref[...]))
    y = _dot(merged.astype(MXU_DT), wo_ref[...])
    o_ref[...] = _layer_norm(alpha * x + y, g_ref[...], b_ref[...])


def mixer_out(x2, o_nsa, o_mla, o_fox, wg, wn, wm, wf, wo, ln_g, ln_b, alpha, tm=512):
    t, d = x2.shape
    row = lambda i: (i, 0)
    fixed = lambda i: (0, 0)
    resident = lambda a: pl.BlockSpec(a.shape, fixed, pipeline_mode=pl.Buffered(1))
    return pl.pallas_call(
        functools.partial(_mix_kernel, alpha=alpha),
        out_shape=jax.ShapeDtypeStruct((t, d), F32),
        grid=(t // tm,),
        in_specs=[pl.BlockSpec((tm, d), row),
                  pl.BlockSpec((tm, NSA_W), row),
                  pl.BlockSpec((tm, MLA_W), row),
                  pl.BlockSpec((tm, FOX_W), row),
                  resident(wg), resident(wn), resident(wm), resident(wf), resident(wo),
                  pl.BlockSpec((1, d), fixed), pl.BlockSpec((1, d), fixed)],
        out_specs=pl.BlockSpec((tm, d), row),
        compiler_params=_params("parallel"),
        name="mixer_out",
    )(x2, o_nsa, o_mla, o_fox, wg, wn, wm, wf, wo, ln_g, ln_b)


def _matmul_kernel(x_ref, w_ref, o_ref):
    o_ref[...] = _dot(x_ref[...].astype(MXU_DT), w_ref[...]).astype(o_ref.dtype)


def matmul(x2, w, out_dtype, tm):
    t, d = x2.shape
    n = w.shape[1]
    return pl.pallas_call(
        _matmul_kernel,
        out_shape=jax.ShapeDtypeStruct((t, n), out_dtype),
        grid=(t // tm,),
        in_specs=[pl.BlockSpec((tm, d), lambda i: (i, 0)), pl.BlockSpec((d, n), lambda i: (0, 0))],
        out_specs=pl.BlockSpec((tm, n), lambda i: (i, 0)),
        compiler_params=_params("parallel"),
        name="mem_kv_proj",
    )(x2, w)


def _xattn_kernel(x_ref, wq_ref, kv_ref, wo_ref, g_ref, b_ref, o_ref, *, alpha):
    x = x_ref[0]
    xb = x.astype(MXU_DT)
    half = x.shape[0] // 2
    q = jnp.concatenate([_dot(xb[:half], wq_ref[...]), _dot(xb[half:], wq_ref[...])], axis=0)
    q = (q * LOG2E).astype(MXU_DT)
    s = [_dot_nt(kv_ref[0, :, h * DH:(h + 1) * DH], q[:, h * DH:(h + 1) * DH]) for h in range(XA_HEADS)]
    outs = []
    for h in range(XA_HEADS):
        v = kv_ref[0, :, XA_W + h * DH:XA_W + (h + 1) * DH]
        p = jnp.exp2(s[h] - jnp.max(s[h], axis=0, keepdims=True)).astype(MXU_DT)
        acc = lax.dot_general(_with_sums(v), p, (((0,), (0,)), ((), ())), preferred_element_type=F32)
        outs.append(acc[:DH] * (1.0 / acc[DH:DH + 1]))
    o = jnp.concatenate(outs, axis=0).T.astype(MXU_DT)
    y = _dot(o, wo_ref[...])
    o_ref[0] = _layer_norm(alpha * x + y, g_ref[...], b_ref[...])


def cross_attention(x, kv, wq, wo, ln_g, ln_b, alpha, tm=512):
    b, s, d = x.shape
    m = kv.shape[1]
    fixed = lambda bi, i: (0, 0)
    return pl.pallas_call(
        functools.partial(_xattn_kernel, alpha=alpha),
        out_shape=jax.ShapeDtypeStruct((b, s, d), F32),
        grid=(b, s // tm),
        in_specs=[pl.BlockSpec((1, tm, d), lambda bi, i: (bi, i, 0)),
                  pl.BlockSpec(wq.shape, fixed),
                  pl.BlockSpec((1, m, 2 * XA_W), lambda bi, i: (bi, 0, 0)),
                  pl.BlockSpec(wo.shape, fixed),
                  pl.BlockSpec((1, d), fixed), pl.BlockSpec((1, d), fixed)],
        out_specs=pl.BlockSpec((1, tm, d), lambda bi, i: (bi, i, 0)),
        compiler_params=_params("parallel", "parallel"),
        name="cross_attn",
    )(x, wq, kv, wo, ln_g, ln_b)


def _mlp_kernel(x_ref, wu_ref, wd_ref, g_ref, b_ref, o_ref, *, alpha):
    x = x_ref[...]
    hdn = jnp.square(jnp.maximum(_dot(x.astype(MXU_DT), wu_ref[...]), 0.0))
    y = _dot(hdn.astype(MXU_DT), wd_ref[...])
    o_ref[...] = _layer_norm(alpha * x + y, g_ref[...], b_ref[...])


def mlp(x2, wu, wd, ln_g, ln_b, alpha, tm=512):
    t, d = x2.shape
    row = lambda i: (i, 0)
    fixed = lambda i: (0, 0)
    resident = lambda a: pl.BlockSpec(a.shape, fixed, pipeline_mode=pl.Buffered(1))
    return pl.pallas_call(
        functools.partial(_mlp_kernel, alpha=alpha),
        out_shape=jax.ShapeDtypeStruct((t, d), F32),
        grid=(t // tm,),
        in_specs=[pl.BlockSpec((tm, d), row), resident(wu), resident(wd),
                  pl.BlockSpec((1, d), fixed), pl.BlockSpec((1, d), fixed)],
        out_specs=pl.BlockSpec((tm, d), row),
        compiler_params=_params("parallel"),
        name="mlp",
    )(x2, wu, wd, ln_g, ln_b)


def _segment_cols(w, cols_per_head, head_offset, width=LANES):
    out = jnp.zeros(w.shape[:-1] + (len(cols_per_head) * width,), w.dtype)
    for h, cols in enumerate(cols_per_head):
        start = h * width + head_offset
        out = out.at[..., start:start + len(cols)].set(w[..., np.asarray(cols)])
    return out


def _prep_layer_weights(seq, w_in, cmp_pe, cmp_w1, cmp_w2, mla_q_norm, mla_w_uq, mla_kv_norm, mla_w_ukv,
                        fox_b_f, w_gate, w_br_nsa, w_br_mla, w_br_fox, w_mix_out, xa_w_q, xa_w_kv,
                        xa_w_o, mlp_w_up, mlp_w_down, ln_g, ln_b):
    depth, d, _ = w_in.shape
    sizes = (NSA_W, NSA_KV, NSA_KV, NSA_KV, NSA_KV, NSA_KV, NSA_KV, 3 * NSA_HEADS,
             MLA_Q_LORA, MLA_KV_LORA, MLA_ROPE, FOX_W, FOX_W, FOX_W, FOX_HEADS)
    offs = np.concatenate([[0], np.cumsum(sizes)])
    col = lambda n: w_in[:, :, offs[n]:offs[n + 1]]
    (nq, nkc, nvc, nks, nvs, nkw, nvw, ngate, cq, ckv, kr, fq, fk, fv, ff) = [col(n) for n in range(15)]
    qk_scale = DH ** -0.5
    wa = jnp.concatenate([nq * qk_scale, fq * qk_scale, fv, nvs, nkw, nvw], axis=2)
    pad = jnp.zeros((depth, d, _NB - (_OFF_FF + FOX_HEADS)), F32)
    wb = jnp.concatenate([cq, ckv, kr, ngate, ff, pad], axis=2)
    wc = jnp.concatenate([nkc, nvc], axis=2)
    wks = _segment_cols(nks, [np.arange(g * DH, (g + 1) * DH) for g in range(NSA_GROUPS)], 0,
                        DH + seq // L_SLC)
    wfk = _segment_cols(fk, [np.arange(h * DH, (h + 1) * DH) for h in range(FOX_HEADS)], 0)
    ff_lane = _OFF_FF - _COL_MISC * LANES
    b_row = jnp.zeros((depth, 1, LANES), F32).at[:, 0, ff_lane:ff_lane + FOX_HEADS].set(fox_b_f)

    dq = MLA_NOPE + MLA_ROPE
    half = MLA_ROPE // 2
    heads = range(MLA_HEADS)
    nope = [np.arange(h * dq, h * dq + MLA_NOPE) for h in heads]
    x1 = [np.arange(h * dq + MLA_NOPE, h * dq + MLA_NOPE + half) for h in heads]
    x2 = [c + half for c in x1]
    wqa = (_segment_cols(mla_w_uq, nope, 0) + _segment_cols(mla_w_uq, x1, MLA_NOPE)
           + _segment_cols(mla_w_uq, x2, MLA_NOPE + half))
    wqb = _segment_cols(mla_w_uq, x2, MLA_NOPE) + _segment_cols(mla_w_uq, x1, MLA_NOPE + half)
    dkv = MLA_NOPE + MLA_V
    wk = _segment_cols(mla_w_ukv, [np.arange(h * dkv, h * dkv + MLA_NOPE) for h in heads], 0)
    wv = mla_w_ukv[:, :, np.concatenate([np.arange(h * dkv + MLA_NOPE, (h + 1) * dkv) for h in heads])]
    pa = np.zeros((MLA_ROPE, MLA_HEADS * LANES), np.float32)
    pb = np.zeros((MLA_ROPE, MLA_HEADS * LANES), np.float32)
    for h in heads:
        for j in range(MLA_ROPE):
            pa[j, h * LANES + MLA_NOPE + j] = 1.0
            pb[j, h * LANES + MLA_NOPE + (j + half) % MLA_ROPE] = 1.0
    stack = lambda top, place: jnp.concatenate(
        [top, jnp.broadcast_to(jnp.asarray(np.concatenate([place, place])), (depth,) + (2 * MLA_ROPE, place.shape[1]))],
        axis=1)
    wka = stack(wk, pa)
    wkb = stack(jnp.zeros_like(wk), pb)

    eye_g = jnp.eye(NSA_GROUPS, dtype=F32)
    w1r = cmp_w1.reshape(depth, 2, L_CMP, DH, CMP_HIDDEN)
    def chunk_weights(w1_half):
        blk = jnp.einsum('ealdj,gh->ealgdhj', w1_half, eye_g)
        return blk.reshape(depth, 2, D_CMP * NSA_KV, NSA_GROUPS * CMP_HIDDEN)
    w1lo = chunk_weights(w1r[:, :, :D_CMP])
    w1hi = chunk_weights(w1r[:, :, D_CMP:])
    w2b = jnp.einsum('eajd,gh->eagjhd', cmp_w2, eye_g).reshape(depth, 2, NSA_GROUPS * CMP_HIDDEN, NSA_KV)
    pe = jnp.broadcast_to(cmp_pe.reshape(depth, 2, 2, D_CMP, 1, DH),
                          (depth, 2, 2, D_CMP, NSA_GROUPS, DH)).reshape(depth, 2, 2, D_CMP * NSA_KV)

    c = lambda a: a.astype(MXU_DT)
    return dict(
        wa=c(wa), wb=c(wb), wc=c(wc), wks=c(wks), wfk=c(wfk),
        pe=pe, w1lo=c(w1lo), w1hi=c(w1hi), w2b=c(w2b),
        q_norm=mla_q_norm[:, None, :], wqa=c(wqa), wqb=c(wqb),
        kv_norm=mla_kv_norm[:, None, :], wka=c(wka), wkb=c(wkb), wv=c(wv),
        b_row=b_row, wg=c(w_gate), wn=c(w_br_nsa), wm=c(w_br_mla), wf=c(w_br_fox), wo=c(w_mix_out),
        xq=c(xa_w_q * qk_scale), xkv=c(xa_w_kv), xo=c(xa_w_o),
        wu=c(mlp_w_up), wd=c(mlp_w_down),
        ln_g=ln_g[:, :, None, :], ln_b=ln_b[:, :, None, :],
    )


def _layer(x, mem2, w, tabs, alpha):
    b, s, d = x.shape
    t = b * s
    x2 = x.reshape(t, d)

    ya, yb, ykc, yvc, yks, yfk = in_proj(x2, w, tabs, s)
    ya3 = ya.reshape(b, s, _NA)
    yb3 = yb.reshape(b, s, _NB)

    nc = s // D_CMP
    kc = compress(ykc.reshape(b, nc, D_CMP * NSA_KV), w["pe"][0], w["w1lo"][0], w["w1hi"][0], w["w2b"][0])
    vc = compress(yvc.reshape(b, nc, D_CMP * NSA_KV), w["pe"][1], w["w1lo"][1], w["w1hi"][1], w["w2b"][1])
    o_nsa = nsa_attention(ya3, yb3, kc, vc, yks.reshape(b, s, -1),
                          tabs["bias_c"], tabs["bias_w"], tabs["overlap_t"])

    q_mla, k_mla, v_mla = mla_up(yb, w, tabs, s)
    wide = MLA_HEADS * LANES
    o_mla = causal_attention(q_mla.reshape(b, s, wide), 0, k_mla.reshape(b, s, wide),
                             v_mla.reshape(b, s, MLA_W), 0, MLA_HEADS, LANES)

    k_decay = fox_decay(yb3, w["b_row"], tabs["decay_place"])
    o_fox = causal_attention(ya3, _COL_FQ, yfk.reshape(b, s, FOX_HEADS * LANES), ya3, _COL_FQ + 1,
                             FOX_HEADS, DH, k_decay=k_decay)

    x2 = mixer_out(x2, o_nsa.reshape(t, NSA_W), o_mla.reshape(t, MLA_W), o_fox.reshape(t, FOX_W),
                   w["wg"], w["wn"], w["wm"], w["wf"], w["wo"], w["ln_g"][0], w["ln_b"][0], alpha)

    m = mem2.shape[0] // b
    kv = matmul(mem2, w["xkv"], MXU_DT, tm=min(512, mem2.shape[0])).reshape(b, m, 2 * XA_W)
    x3 = cross_attention(x2.reshape(b, s, d), kv, w["xq"], w["xo"], w["ln_g"][1], w["ln_b"][1], alpha)

    x2 = mlp(x3.reshape(t, d), w["wu"], w["wd"], w["ln_g"][2], w["ln_b"][2], alpha)
    return x2.reshape(b, s, d)


def _tables(t5_table, s):
    tq = NSA_TQ
    ki = jnp.arange(tq)[:, None]
    qi = jnp.arange(tq)[None, :]
    last = jnp.full((tq, tq), T5_BUCKETS - 1, jnp.int32)
    masked = jnp.full((tq, tq), -1, jnp.int32)
    kinds = [None] * N_TILE_KINDS
    kinds[TILE_DIAG] = jnp.where(qi >= ki, _t5_bucket(qi - ki), -1)
    kinds[TILE_PREV] = _t5_bucket(tq + qi - ki)
    kinds[TILE_FAR] = last
    kinds[TILE_EDGE] = jnp.where(qi < ki, last, masked)
    kinds[TILE_NONE] = masked
    kinds[TILE_SEL_DIAG] = kinds[TILE_DIAG]
    kinds[TILE_SEL_PREV] = kinds[TILE_PREV]
    kinds[TILE_ZERO] = last
    relative = [int(k in (TILE_SEL_DIAG, TILE_SEL_PREV, TILE_ZERO)) for k in range(N_TILE_KINDS)]
    idx_w = jnp.concatenate(kinds, axis=0).astype(jnp.int32)
    rel_w = jnp.concatenate([jnp.full((tq, tq), r, jnp.int32) for r in relative], axis=0)
    bias_w = bias_table(t5_table, idx_w, rel_w, tq).reshape(NSA_HEADS, N_TILE_KINDS, tq, tq)

    nc = s // D_CMP
    cmp_end = jnp.arange(nc) * D_CMP + (L_CMP - 1)
    dist_c = jnp.arange(s)[None, :] - cmp_end[:, None]
    idx_c = jnp.where(dist_c >= 0, _t5_bucket(dist_c), -1).astype(jnp.int32)
    bias_c = bias_table(t5_table, idx_c, jnp.zeros_like(idx_c), min(64, nc))

    n_slc = s // L_SLC
    block_onehot = (np.arange(s)[:, None] // L_SLC == np.arange(n_slc)[None, :]).astype(np.float32)
    block_cols = np.concatenate([np.zeros((s, DH), np.float32), block_onehot] * NSA_GROUPS, axis=1)
    ff_lane = _OFF_FF - _COL_MISC * LANES
    decay_place = np.zeros((3, LANES, FOX_HEADS * LANES), np.float32)
    for j in range(3):
        for h in range(FOX_HEADS):
            decay_place[j, ff_lane + h, h * LANES + DH + j] = 1.0
    c_lo = np.arange(nc)[:, None] * D_CMP
    s_lo = np.arange(n_slc)[None, :] * L_SLC
    overlap = np.maximum(np.minimum(c_lo + L_CMP, s_lo + L_SLC) - np.maximum(c_lo, s_lo), 0) / D_CMP
    overlap[nc - 1, :] = 0.0

    half = MLA_ROPE // 2
    inv = ROPE_THETA ** (-jnp.arange(half, dtype=F32) / half)
    ang = jnp.arange(s).astype(F32)[:, None] * inv[None, :]
    cos, sin = jnp.cos(ang), jnp.sin(ang)
    tail = jnp.zeros((s, LANES - MLA_NOPE - MLA_ROPE), F32)
    seg_c = jnp.concatenate([jnp.ones((s, MLA_NOPE), F32), cos, cos, tail], axis=1)
    seg_s = jnp.concatenate([jnp.zeros((s, MLA_NOPE), F32), -sin, sin, tail], axis=1)
    col_scale = np.ones((1, _NA), np.float32)
    col_scale[0, :NSA_W] = LOG2E
    col_scale[0, _COL_FQ * FOX_W:(_COL_FQ + 1) * FOX_W] = LOG2E
    return dict(bias_w=bias_w, bias_c=bias_c, overlap_t=jnp.asarray(overlap.T, MXU_DT),
                block_cols=jnp.asarray(block_cols), decay_place=jnp.asarray(decay_place, MXU_DT),
                rope_cos=jnp.tile(seg_c, (1, MLA_HEADS)), rope_sin=jnp.tile(seg_s, (1, MLA_HEADS)),
                col_scale=jnp.asarray(col_scale))


def kernel(x, mem, w_in, cmp_pe, cmp_w1, cmp_w2, t5_table, mla_q_norm, mla_w_uq, mla_kv_norm, mla_w_ukv, fox_b_f, w_gate, w_br_nsa, w_br_mla, w_br_fox, w_mix_out, xa_w_q, xa_w_kv, xa_w_o, mlp_w_up, mlp_w_down, ln_g, ln_b):
    depth = w_in.shape[0]
    b, s, d = x.shape
    alpha = (2 * depth) ** 0.25
    weights = _prep_layer_weights(s, w_in, cmp_pe, cmp_w1, cmp_w2, mla_q_norm, mla_w_uq, mla_kv_norm,
                                  mla_w_ukv, fox_b_f, w_gate, w_br_nsa, w_br_mla, w_br_fox, w_mix_out,
                                  xa_w_q, xa_w_kv, xa_w_o, mlp_w_up, mlp_w_down, ln_g, ln_b)
    tabs = _tables(t5_table, s)
    mem2 = mem.reshape(-1, d)

    for layer in range(depth):
        x = _layer(x, mem2, {name: a[layer] for name, a in weights.items()}, tabs, alpha)
    return x
```

```python
import functools
import math

import numpy as np
import jax
import jax.numpy as jnp
from jax import lax
from jax.experimental import pallas as pl
from jax.experimental.pallas import tpu as pltpu

D_MODEL = 1024
DH = 64
NSA_HEADS = 8
NSA_GROUPS = 2
NSA_R = NSA_HEADS // NSA_GROUPS
L_CMP = 32
D_CMP = 16
CMP_HIDDEN = 128
L_SLC = 64
N_SEL = 8
WINDOW = 512
MLA_HEADS = 4
MLA_NOPE = 64
MLA_ROPE = 32
MLA_V = 64
MLA_Q_LORA = 384
MLA_KV_LORA = 128
ROPE_THETA = 10000.0
FOX_HEADS = 4
XA_HEADS = 4
D_FF = 4 * D_MODEL
T5_BUCKETS = 32
T5_MAX_DIST = 128
NSA_W = NSA_HEADS * DH
NSA_KV = NSA_GROUPS * DH
MLA_W = MLA_HEADS * MLA_V
FOX_W = FOX_HEADS * DH
XA_W = XA_HEADS * DH
LN_EPS = 1e-5
RMS_EPS = 1e-6
NEG = -1e30
FORCE = 1e4
LOG2E = math.log2(math.e)

F32 = jnp.float32
MXU_DT = jnp.bfloat16
LANES = 128
NSA_TQ = 128
NSA_RQ = NSA_R * NSA_TQ
SEL_TK = 512
ATT_T = 512
VMEM_LIMIT = 48 * 1024 * 1024

TILE_DIAG = 0
TILE_PREV = 1
TILE_FAR = 2
TILE_EDGE = 3
TILE_NONE = 4
TILE_SEL_DIAG = 5
TILE_SEL_PREV = 6
TILE_ZERO = 7
N_TILE_KINDS = 8

_NA = NSA_W + 2 * FOX_W + 3 * NSA_KV
_COL_FQ = NSA_W // FOX_W
_COL_VS = (NSA_W + 2 * FOX_W) // NSA_KV
_OFF_CQ = 0
_OFF_CKV = MLA_Q_LORA
_OFF_KR = _OFF_CKV + MLA_KV_LORA
_OFF_NG = _OFF_KR + MLA_ROPE
_OFF_FF = _OFF_NG + 3 * NSA_HEADS
_NB = 640
_COL_MISC = _OFF_KR // LANES


def _dot(a, b):
    return jnp.dot(a, b, preferred_element_type=F32)


def _dot_nt(a, b):
    return lax.dot_general(a, b, (((1,), (1,)), ((), ())), preferred_element_type=F32)


def _params(*sem):
    return pltpu.CompilerParams(dimension_semantics=sem, vmem_limit_bytes=VMEM_LIMIT)


def _layer_norm(z, g, b):
    mu = jnp.mean(z, axis=-1, keepdims=True)
    zc = z - mu
    var = jnp.mean(zc * zc, axis=-1, keepdims=True)
    return zc * lax.rsqrt(var + LN_EPS) * g + b


def _rms_norm(z, g):
    return z * lax.rsqrt(jnp.mean(z * z, axis=-1, keepdims=True) + RMS_EPS) * g


def _t5_bucket(dist):
    n = jnp.maximum(dist, 0)
    max_exact = T5_BUCKETS // 2
    nf = jnp.maximum(n, 1).astype(F32)
    large = max_exact + (jnp.log(nf / max_exact) / math.log(T5_MAX_DIST / max_exact)
                         * (T5_BUCKETS - max_exact)).astype(jnp.int32)
    large = jnp.minimum(large, T5_BUCKETS - 1)
    return jnp.where(n < max_exact, n, large)


SUM_ROWS = 16
ACC_ROWS = DH + SUM_ROWS


def _with_ones(v):
    return jnp.concatenate([v, jnp.ones(v.shape, v.dtype)], axis=1)


def _with_sums(v):
    return jnp.concatenate([v, jnp.ones((v.shape[0], SUM_ROWS), v.dtype)], axis=1)


def _bias_table_kernel(t5_ref, idx_ref, rel_ref, o_ref):
    h = pl.program_id(0)
    idx = idx_ref[...]
    acc = jnp.full(idx.shape, NEG, F32)
    for b in range(T5_BUCKETS):
        acc = jnp.where(idx == b, t5_ref[b, h] * LOG2E, acc)
    o_ref[0] = acc - jnp.where(rel_ref[...] > 0, t5_ref[T5_BUCKETS - 1, h] * LOG2E, 0.0)


def bias_table(t5_table, idx, rel, tr):
    rows, cols = idx.shape
    return pl.pallas_call(
        _bias_table_kernel,
        out_shape=jax.ShapeDtypeStruct((NSA_HEADS, rows, cols), F32),
        grid=(NSA_HEADS, rows // tr),
        in_specs=[pl.BlockSpec(memory_space=pltpu.SMEM),
                  pl.BlockSpec((tr, cols), lambda h, i: (i, 0)),
                  pl.BlockSpec((tr, cols), lambda h, i: (i, 0))],
        out_specs=pl.BlockSpec((1, tr, cols), lambda h, i: (h, i, 0)),
        compiler_params=_params("parallel", "parallel"),
        name="bias_table",
    )(t5_table, idx, rel)


def _in_proj_kernel(x_ref, wa_ref, wb_ref, wc_ref, wks_ref, wfk_ref, sc_ref, blk_ref,
                    ya_ref, yb_ref, ykc_ref, yvc_ref, yks_ref, yfk_ref):
    xb = x_ref[...].astype(MXU_DT)
    ya_ref[...] = (_dot(xb, wa_ref[...]) * sc_ref[...]).astype(ya_ref.dtype)
    yb_ref[...] = _dot(xb, wb_ref[...])
    yc = _dot(xb, wc_ref[...])
    ykc_ref[...] = yc[:, :NSA_KV]
    yvc_ref[...] = yc[:, NSA_KV:]
    yks_ref[...] = (_dot(xb, wks_ref[...]) + blk_ref[...]).astype(yks_ref.dtype)
    yfk_ref[...] = _dot(xb, wfk_ref[...]).astype(yfk_ref.dtype)


def in_proj(x2, w, tabs, seq, tm=512):
    t, d = x2.shape
    nps = seq // tm
    row = lambda i: (i, 0)
    fixed = lambda i: (0, 0)
    weights = [w["wa"], w["wb"], w["wc"], w["wks"], w["wfk"]]
    widths = [(_NA, MXU_DT), (_NB, F32), (NSA_KV, F32), (NSA_KV, F32),
              (w["wks"].shape[1], MXU_DT), (w["wfk"].shape[1], MXU_DT)]
    return pl.pallas_call(
        _in_proj_kernel,
        out_shape=tuple(jax.ShapeDtypeStruct((t, n), dt) for n, dt in widths),
        grid=(t // tm,),
        in_specs=[pl.BlockSpec((tm, d), row)] + [pl.BlockSpec(a.shape, fixed) for a in weights]
        + [pl.BlockSpec((1, _NA), fixed),
           pl.BlockSpec((tm, tabs["block_cols"].shape[1]), lambda i: (i % nps, 0))],
        out_specs=tuple(pl.BlockSpec((tm, n), row) for n, _ in widths),
        compiler_params=_params("parallel"),
        name="in_proj",
    )(x2, *weights, tabs["col_scale"], tabs["block_cols"])


def _mla_up_kernel(yb_ref, qn_ref, wqa_ref, wqb_ref, kvn_ref, wka_ref, wkb_ref, wv_ref,
                   ct_ref, st_ref, q_ref, k_ref, v_ref):
    scale = (MLA_NOPE + MLA_ROPE) ** -0.5 * LOG2E
    ct = ct_ref[...]
    st = st_ref[...]
    cq = _rms_norm(yb_ref[:, _OFF_CQ:_OFF_CQ + MLA_Q_LORA], qn_ref[...]).astype(MXU_DT)
    q = _dot(cq, wqa_ref[...]) * ct + _dot(cq, wqb_ref[...]) * st
    q_ref[...] = (q * scale).astype(q_ref.dtype)
    ckv = _rms_norm(yb_ref[:, _OFF_CKV:_OFF_CKV + MLA_KV_LORA], kvn_ref[...]).astype(MXU_DT)
    v_ref[...] = _dot(ckv, wv_ref[...]).astype(v_ref.dtype)
    kr = yb_ref[:, _OFF_KR:_OFF_KR + MLA_ROPE]
    kr_hi = kr.astype(MXU_DT)
    kr_lo = (kr - kr_hi.astype(F32)).astype(MXU_DT)
    lhs = jnp.concatenate([ckv, kr_hi, kr_lo], axis=1)
    k_ref[...] = (_dot(lhs, wka_ref[...]) * ct + _dot(lhs, wkb_ref[...]) * st).astype(k_ref.dtype)


def mla_up(yb, w, tabs, seq, tm=512):
    t = yb.shape[0]
    nps = seq // tm
    row = lambda i: (i, 0)
    fixed = lambda i: (0, 0)
    pos = lambda i: (i % nps, 0)
    wide = MLA_HEADS * LANES
    ins = [w["q_norm"], w["wqa"], w["wqb"], w["kv_norm"], w["wka"], w["wkb"], w["wv"]]
    return pl.pallas_call(
        _mla_up_kernel,
        out_shape=(jax.ShapeDtypeStruct((t, wide), MXU_DT), jax.ShapeDtypeStruct((t, wide), MXU_DT),
                   jax.ShapeDtypeStruct((t, MLA_W), MXU_DT)),
        grid=(t // tm,),
        in_specs=[pl.BlockSpec((tm, _NB), row)] + [pl.BlockSpec(a.shape, fixed) for a in ins]
        + [pl.BlockSpec((tm, wide), pos), pl.BlockSpec((tm, wide), pos)],
        out_specs=(pl.BlockSpec((tm, wide), row), pl.BlockSpec((tm, wide), row),
                   pl.BlockSpec((tm, MLA_W), row)),
        compiler_params=_params("parallel"),
        name="mla_up",
    )(yb, *ins, tabs["rope_cos"], tabs["rope_sin"])


def _bf16_head(x):
    bits = lax.bitcast_convert_type(x, jnp.uint32) & jnp.uint32(0xFFFF0000)
    return lax.bitcast_convert_type(bits, F32)


def _fox_decay_kernel(f_ref, b_ref, place_ref, o_ref):
    z = f_ref[0] + b_ref[...]
    x = jnp.minimum(z, 0.0) - jnp.log1p(jnp.exp(-jnp.abs(z)))
    n = x.shape[0]
    row = lax.broadcasted_iota(jnp.int32, x.shape, 0)
    shift = 1
    while shift < n:
        x = x + jnp.where(row >= shift, pltpu.roll(x, shift, 0), 0.0)
        shift *= 2
    dec = -(x * LOG2E)
    hi = _bf16_head(dec)
    rest = dec - hi
    mid = _bf16_head(rest)
    terms = (hi, mid, rest - mid)
    o_ref[0] = sum(_dot(t.astype(MXU_DT), place_ref[j]) for j, t in enumerate(terms)).astype(o_ref.dtype)


def fox_decay(misc, b_row, place):
    b, s, _ = misc.shape
    wide = place.shape[2]
    return pl.pallas_call(
        _fox_decay_kernel,
        out_shape=jax.ShapeDtypeStruct((b, s, wide), MXU_DT),
        grid=(b,),
        in_specs=[pl.BlockSpec((1, s, LANES), lambda bi: (bi, 0, _COL_MISC)),
                  pl.BlockSpec((1, LANES), lambda bi: (0, 0)),
                  pl.BlockSpec(place.shape, lambda bi: (0, 0, 0))],
        out_specs=pl.BlockSpec((1, s, wide), lambda bi: (bi, 0, 0)),
        compiler_params=_params("parallel"),
        name="fox_decay",
    )(misc, b_row, place)


def _keys_softmax_update(m, acc, s, v_aug):
    m_new = jnp.maximum(m, jnp.max(s, axis=0, keepdims=True))
    a = jnp.exp2(m - m_new)
    p = jnp.exp2(s - m_new).astype(MXU_DT)
    pv = lax.dot_general(v_aug, p, (((0,), (0,)), ((), ())), preferred_element_type=F32)
    return m_new, a * acc + pv


def _keys_softmax_init(queries):
    return jnp.full((1, queries), NEG, F32), jnp.zeros((ACC_ROWS, queries), F32)


def _softmax_piece(s, v_aug):
    m = jnp.max(s, axis=0, keepdims=True)
    p = jnp.exp2(s - m).astype(MXU_DT)
    return m, lax.dot_general(v_aug, p, (((0,), (0,)), ((), ())), preferred_element_type=F32)


def _softmax_merge(state, pieces):
    m_old, acc = state
    m = m_old
    for mp, _ in pieces:
        m = jnp.maximum(m, mp)
    acc = acc * jnp.exp2(m_old - m)
    for mp, ap in pieces:
        acc = acc + ap * jnp.exp2(mp - m)
    return m, acc


def _causal_attn_kernel(*refs, n_heads, dq, decay):
    if decay:
        q_ref, k_ref, kd_ref, v_ref, o_ref = refs
    else:
        q_ref, k_ref, v_ref, o_ref = refs
    qi = pl.program_id(1)
    t = q_ref.shape[1]
    key = lax.broadcasted_iota(jnp.int32, (t, t), 0)
    qry = lax.broadcasted_iota(jnp.int32, (t, t), 1)

    qs = []
    for h in range(n_heads):
        q = q_ref[0, :, h * dq:(h + 1) * dq]
        qs.append(_with_ones(q) if decay else q)

    def scores(rows, h):
        k = k_ref[0, rows, h * LANES:(h + 1) * LANES]
        if decay:
            k = k + kd_ref[0, rows, h * LANES:(h + 1) * LANES]
        return _dot_nt(k, qs[h])

    def values(rows, h):
        return _with_sums(v_ref[0, rows, h * DH:(h + 1) * DH])

    def sweep(rows, state, mask):
        s = [scores(rows, h) for h in range(n_heads)]
        if mask is not None:
            s = [jnp.where(mask, sh, NEG) for sh in s]
        return tuple(_keys_softmax_update(*state[h], s[h], values(rows, h)) for h in range(n_heads))

    n_double = lax.shift_right_logical(qi, 1)
    state = lax.fori_loop(
        0, n_double, lambda p, st: sweep(pl.ds(pl.multiple_of(p * 2 * t, 2 * t), 2 * t), st, None),
        tuple(_keys_softmax_init(t) for _ in range(n_heads)))
    state = lax.fori_loop(
        0, qi & 1, lambda _, st: sweep(pl.ds(pl.multiple_of(n_double * 2 * t, t), t), st, None), state)
    state = sweep(pl.ds(pl.multiple_of(qi * t, t), t), state, key <= qry)
    outs = [acc[:DH] * (1.0 / acc[DH:DH + 1]) for _, acc in state]
    for j in range(n_heads // 2):
        slab = jnp.concatenate(outs[2 * j:2 * j + 2], axis=0).T
        o_ref[0, :, 2 * j * DH:(2 * j + 2) * DH] = slab.astype(o_ref.dtype)


def causal_attention(q, q_col, k, v, v_col, n_heads, dq, k_decay=None, t=ATT_T):
    b, s, _ = q.shape
    whole = pl.BlockSpec((1, s, n_heads * LANES), lambda bi, qi: (bi, 0, 0))
    keys = [k] if k_decay is None else [k, k_decay]
    return pl.pallas_call(
        functools.partial(_causal_attn_kernel, n_heads=n_heads, dq=dq, decay=k_decay is not None),
        out_shape=jax.ShapeDtypeStruct((b, s, n_heads * DH), MXU_DT),
        grid=(b, s // t),
        in_specs=[pl.BlockSpec((1, t, n_heads * dq), lambda bi, qi: (bi, qi, q_col))]
        + [whole] * len(keys)
        + [pl.BlockSpec((1, s, n_heads * DH), lambda bi, qi: (bi, 0, v_col))],
        out_specs=pl.BlockSpec((1, t, n_heads * DH), lambda bi, qi: (bi, qi, 0)),
        compiler_params=_params("parallel", "arbitrary"),
        name="causal_attn_decay" if k_decay is not None else "causal_attn",
    )(q, *keys, v)


def _compress_kernel(x_ref, pe_ref, wlo_ref, whi_ref, w2_ref, o_ref):
    x = x_ref[0]
    nc = x.shape[0]
    lo = _dot((x + pe_ref[0:1, :]).astype(MXU_DT), wlo_ref[...])
    hi = _dot((x + pe_ref[1:2, :]).astype(MXU_DT), whi_ref[...])
    hid = lo + pltpu.roll(hi, nc - 1, 0)
    act = jax.nn.gelu(hid)
    o_ref[0] = _dot(act.astype(MXU_DT), w2_ref[...]).astype(o_ref.dtype)


def compress(x, pe, wlo, whi, w2):
    b, nc, width = x.shape
    fixed = lambda bi: (0, 0)
    return pl.pallas_call(
        _compress_kernel,
        out_shape=jax.ShapeDtypeStruct((b, nc, NSA_KV), MXU_DT),
        grid=(b,),
        in_specs=[pl.BlockSpec((1, nc, width), lambda bi: (bi, 0, 0)),
                  pl.BlockSpec(pe.shape, fixed), pl.BlockSpec(wlo.shape, fixed),
                  pl.BlockSpec(whi.shape, fixed), pl.BlockSpec(w2.shape, fixed)],
        out_specs=pl.BlockSpec((1, nc, NSA_KV), lambda bi: (bi, 0, 0)),
        compiler_params=_params("parallel"),
        name="nsa_compress",
    )(x, pe, wlo, whi, w2)


def _nsa_kernel(q_ref, kc_ref, vc_ref, ks_ref, vs_ref, kw_ref, vw_ref, misc_ref,
                bc_ref, bw_ref, ovt_ref, o_ref, *, n_sel):
    i = pl.program_id(1)
    tq, rq = NSA_TQ, NSA_RQ
    nslc = ovt_ref.shape[0]
    seg = DH + nslc
    nwt = WINDOW // tq + 1
    sub = SEL_TK // tq
    t0 = i * tq
    g_off = _OFF_NG - _COL_MISC * LANES
    gates_t = jax.nn.sigmoid(misc_ref[0]).T

    lane = lax.broadcasted_iota(jnp.int32, (1, rq), 1)
    has_cmp = (t0 + (lane & (tq - 1))) >= (L_CMP - 1)

    blk = lax.broadcasted_iota(jnp.int32, (nslc, tq), 0)
    blk_f = blk.astype(F32)
    tpos = lax.broadcasted_iota(jnp.int32, (nslc, tq), 1) + t0
    cur = lax.shift_right_logical(tpos, int(math.log2(L_SLC)))
    forced = (blk == 0) | (blk == cur) | (blk == cur - 1)
    causal_blk = blk * L_SLC <= tpos
    eye = (lax.broadcasted_iota(jnp.int32, (tq, tq), 0)
           == lax.broadcasted_iota(jnp.int32, (tq, tq), 1)).astype(MXU_DT)

    def window_tile_kind(d):
        static = {0: TILE_DIAG, 1: TILE_PREV, nwt - 1: TILE_EDGE}.get(d, TILE_FAR)
        return jnp.where(i - d < 0, TILE_NONE, static)

    def group_cols(g):
        return slice(g * DH, (g + 1) * DH)

    groups = range(NSA_GROUPS)
    q_groups = [jnp.concatenate([q_ref[0, :, (g * NSA_R + r) * DH:(g * NSA_R + r + 1) * DH]
                                 for r in range(NSA_R)], axis=0) for g in groups]

    def group_lanes(tile_of_head, g):
        return jnp.concatenate([tile_of_head(g * NSA_R + r) for r in range(NSA_R)], axis=1)

    def bias_tile(g, kind):
        return group_lanes(lambda h: bw_ref[h, kind], g)

    s_cmp = [_dot_nt(kc_ref[0, :, group_cols(g)], q_groups[g]) + group_lanes(lambda h: bc_ref[h], g)
             for g in groups]
    win_rows = [pl.ds(pl.multiple_of(jnp.maximum(i - d, 0) * tq, tq), tq) for d in range(nwt - 1, -1, -1)]
    s_win = []
    for g in groups:
        kcat = jnp.concatenate([kw_ref[0, r, group_cols(g)] for r in win_rows], axis=0)
        bias = jnp.concatenate([bias_tile(g, window_tile_kind(d)) for d in range(nwt - 1, -1, -1)], axis=0)
        s_win.append(_dot_nt(kcat, q_groups[g]) + bias)

    def gate_row(g, j):
        return jnp.concatenate(
            [gates_t[g_off + 3 * (g * NSA_R + r) + j:g_off + 3 * (g * NSA_R + r) + j + 1] for r in range(NSA_R)],
            axis=1)

    def weighted_values(v, p):
        return lax.dot_general(_with_sums(v), p, (((0,), (0,)), ((), ())), preferred_element_type=F32)

    o_cmp, imp = [], []
    for g in groups:
        e = jnp.exp2(s_cmp[g] - jnp.max(s_cmp[g], axis=0, keepdims=True)).astype(MXU_DT)
        acc = weighted_values(vc_ref[0, :, group_cols(g)], e)
        inv = jnp.where(has_cmp, 1.0 / acc[DH:DH + 1], 0.0)
        o_cmp.append(acc[:DH] * inv)
        imp4 = _dot(ovt_ref[...], e) * inv
        imp.append(sum(imp4[:, r * tq:(r + 1) * tq] for r in range(1, NSA_R)) + imp4[:, 0:tq])

    o_fixed = []
    for g in groups:
        vcat = jnp.concatenate([vw_ref[0, r, group_cols(g)] for r in win_rows], axis=0)
        p = jnp.exp2(s_win[g] - jnp.max(s_win[g], axis=0, keepdims=True)).astype(MXU_DT)
        acc = weighted_values(vcat, p)
        o_win = acc[:DH] * (1.0 / acc[DH:DH + 1])
        o_fixed.append(gate_row(g, 0) * o_cmp[g] + gate_row(g, 2) * o_win)

    q_aug = []
    for g in groups:
        x = jnp.where(causal_blk, imp[g] + jnp.where(forced, FORCE, 0.0), NEG)
        sel_t = jnp.zeros((nslc, tq), F32)
        for _ in range(n_sel):
            mx = jnp.max(x, axis=0, keepdims=True)
            first = jnp.min(jnp.where(x == mx, blk_f, float(nslc)), axis=0, keepdims=True)
            hit = blk_f == first
            sel_t = jnp.where(hit, 1.0, sel_t)
            x = jnp.where(hit, -3e38, x)
        penalty = ((1.0 - _dot_nt(eye, sel_t.astype(MXU_DT))) * NEG).astype(MXU_DT)
        q_aug.append(jnp.concatenate([q_groups[g], jnp.concatenate([penalty] * NSA_R, axis=0)], axis=1))

    def sel_scores(rows, g):
        return _dot_nt(ks_ref[0, rows, g * seg:(g + 1) * seg], q_aug[g])

    def sel_values(rows, g):
        return _with_sums(vs_ref[0, rows, group_cols(g)])

    def far_sweep(start, n_tiles, state):
        tiles = [pl.ds(pl.multiple_of(start + j * SEL_TK, SEL_TK), SEL_TK) for j in range(n_tiles)]
        s = [[sel_scores(rows, g) for rows in tiles] for g in groups]
        pieces = [[_softmax_piece(s[g][j], sel_values(rows, g)) for j, rows in enumerate(tiles)]
                  for g in groups]
        return tuple(_softmax_merge(state[g], pieces[g]) for g in groups)

    last = i // sub
    n_far = jnp.maximum(last - 1, 0)
    n_double = lax.shift_right_logical(n_far, 1)
    state = lax.fori_loop(0, n_double, lambda p, st: far_sweep(p * 2 * SEL_TK, 2, st),
                          tuple(_keys_softmax_init(rq) for _ in groups))
    state = lax.fori_loop(0, n_far & 1, lambda _, st: far_sweep(n_double * 2 * SEL_TK, 1, st), state)
    near = []
    for kn in (last - 1, last):
        kt = pl.ds(pl.multiple_of(jnp.maximum(kn, 0) * SEL_TK, SEL_TK), SEL_TK)
        kinds = []
        for j in range(sub):
            d = i - (kn * sub + j)
            kind = jnp.where(d < 0, TILE_NONE, jnp.where(d == 0, TILE_SEL_DIAG,
                                                         jnp.where(d == 1, TILE_SEL_PREV, TILE_ZERO)))
            kinds.append(jnp.where(kn < 0, TILE_NONE, kind))
        for g in groups:
            bias = jnp.concatenate([bias_tile(g, kd) for kd in kinds], axis=0)
            near.append((kt, g, sel_scores(kt, g) + bias))
    pieces = [[] for _ in groups]
    for kt, g, s in near:
        pieces[g].append(_softmax_piece(s, sel_values(kt, g)))
    state = [_softmax_merge(state[g], pieces[g]) for g in groups]

    out_t = []
    for g in groups:
        acc = state[g][1]
        out_t.append(o_fixed[g] + gate_row(g, 1) * (acc[:DH] * (1.0 / acc[DH:DH + 1])))
    out_t = jnp.concatenate(out_t, axis=0)
    for r in range(NSA_R):
        slab = out_t[:, r * tq:(r + 1) * tq].T
        for g in groups:
            h = g * NSA_R + r
            o_ref[0, :, h * DH:(h + 1) * DH] = slab[:, g * DH:(g + 1) * DH].astype(o_ref.dtype)


def nsa_attention(ya, yb, kc, vc, ks_aug, bias_c, bias_w, overlap_t):
    b, s, _ = ya.shape
    tq = NSA_TQ
    ncp = kc.shape[1]
    per_b_kv = lambda col: pl.BlockSpec((1, s, NSA_KV), lambda bi, i: (bi, 0, col))
    per_b_cmp = pl.BlockSpec((1, ncp, NSA_KV), lambda bi, i: (bi, 0, 0))
    return pl.pallas_call(
        functools.partial(_nsa_kernel, n_sel=min(N_SEL, overlap_t.shape[0])),
        out_shape=jax.ShapeDtypeStruct((b, s, NSA_W), MXU_DT),
        grid=(b, s // tq),
        in_specs=[pl.BlockSpec((1, tq, NSA_W), lambda bi, i: (bi, i, 0)),
                  per_b_cmp, per_b_cmp,
                  pl.BlockSpec((1, s, ks_aug.shape[2]), lambda bi, i: (bi, 0, 0)),
                  per_b_kv(_COL_VS), per_b_kv(_COL_VS + 1), per_b_kv(_COL_VS + 2),
                  pl.BlockSpec((1, tq, LANES), lambda bi, i: (bi, i, _COL_MISC)),
                  pl.BlockSpec((NSA_HEADS, ncp, tq), lambda bi, i: (0, 0, i)),
                  pl.BlockSpec(bias_w.shape, lambda bi, i: (0, 0, 0, 0)),
                  pl.BlockSpec(overlap_t.shape, lambda bi, i: (0, 0))],
        out_specs=pl.BlockSpec((1, tq, NSA_W), lambda bi, i: (bi, i, 0)),
        compiler_params=_params("parallel", "arbitrary"),
        name="nsa_attn",
    )(ya, kc, vc, ks_aug, ya, ya, ya, yb, bias_c, bias_w, overlap_t)


def _mix_kernel(x_ref, on_ref, om_ref, of_ref, wg_ref, wn_ref, wm_ref, wf_ref, wo_ref,
                g_ref, b_ref, o_ref, *, alpha):
    half = x_ref.shape[0] // 2
    d = x_ref.shape[1]
    for r in (slice(0, half), slice(half, 2 * half)):
        x = x_ref[r]
        xb = x.astype(MXU_DT)
        merged = (jax.nn.sigmoid(_dot(xb, wg_ref[:, 0:d])) * _dot(on_ref[r], wn_ref[...])
                  + jax.nn.sigmoid(_dot(xb, wg_ref[:, d:2 * d])) * _dot(om_ref[r], wm_ref[...])
                  + jax.nn.sigmoid(_dot(xb, wg_ref[:, 2 * d:3 * d])) * _dot(of_ref[r], wf_ref[...]))
        y = _dot(merged.astype(MXU_DT), wo_ref[...])
        o_ref[r] = _layer_norm(alpha * x + y, g_ref[...], b_ref[...])


def mixer_out(x2, o_nsa, o_mla, o_fox, wg, wn, wm, wf, wo, ln_g, ln_b, alpha, tm=512):
    t, d = x2.shape
    row = lambda i: (i, 0)
    fixed = lambda i: (0, 0)
    resident = lambda a: pl.BlockSpec(a.shape, fixed, pipeline_mode=pl.Buffered(1))
    return pl.pallas_call(
        functools.partial(_mix_kernel, alpha=alpha),
        out_shape=jax.ShapeDtypeStruct((t, d), F32),
        grid=(t // tm,),
        in_specs=[pl.BlockSpec((tm, d), row),
                  pl.BlockSpec((tm, NSA_W), row),
                  pl.BlockSpec((tm, MLA_W), row),
                  pl.BlockSpec((tm, FOX_W), row),
                  resident(wg), resident(wn), resident(wm), resident(wf), resident(wo),
                  pl.BlockSpec((1, d), fixed), pl.BlockSpec((1, d), fixed)],
        out_specs=pl.BlockSpec((tm, d), row),
        compiler_params=_params("parallel"),
        name="mixer_out",
    )(x2, o_nsa, o_mla, o_fox, wg, wn, wm, wf, wo, ln_g, ln_b)


def _matmul_kernel(x_ref, w_ref, o_ref):
    o_ref[...] = _dot(x_ref[...].astype(MXU_DT), w_ref[...]).astype(o_ref.dtype)


def matmul(x2, w, out_dtype, tm):
    t, d = x2.shape
    n = w.shape[1]
    return pl.pallas_call(
        _matmul_kernel,
        out_shape=jax.ShapeDtypeStruct((t, n), out_dtype),
        grid=(t // tm,),
        in_specs=[pl.BlockSpec((tm, d), lambda i: (i, 0)), pl.BlockSpec((d, n), lambda i: (0, 0))],
        out_specs=pl.BlockSpec((tm, n), lambda i: (i, 0)),
        compiler_params=_params("parallel"),
        name="mem_kv_proj",
    )(x2, w)


def _xattn_kernel(x_ref, wq_ref, kv_ref, wo_ref, g_ref, b_ref, o_ref, *, alpha):
    x = x_ref[0]
    xb = x.astype(MXU_DT)
    half = x.shape[0] // 2
    q = jnp.concatenate([_dot(xb[:half], wq_ref[...]), _dot(xb[half:], wq_ref[...])], axis=0)
    q = (q * LOG2E).astype(MXU_DT)
    s = [_dot_nt(kv_ref[0, :, h * DH:(h + 1) * DH], q[:, h * DH:(h + 1) * DH]) for h in range(XA_HEADS)]
    outs = []
    for h in range(XA_HEADS):
        v = kv_ref[0, :, XA_W + h * DH:XA_W + (h + 1) * DH]
        p = jnp.exp2(s[h] - jnp.max(s[h], axis=0, keepdims=True)).astype(MXU_DT)
        acc = lax.dot_general(_with_sums(v), p, (((0,), (0,)), ((), ())), preferred_element_type=F32)
        outs.append(acc[:DH] * (1.0 / acc[DH:DH + 1]))
    o = jnp.concatenate(outs, axis=0).T.astype(MXU_DT)
    y = _dot(o, wo_ref[...])
    o_ref[0] = _layer_norm(alpha * x + y, g_ref[...], b_ref[...])


def cross_attention(x, kv, wq, wo, ln_g, ln_b, alpha, tm=512):
    b, s, d = x.shape
    m = kv.shape[1]
    fixed = lambda bi, i: (0, 0)
    return pl.pallas_call(
        functools.partial(_xattn_kernel, alpha=alpha),
        out_shape=jax.ShapeDtypeStruct((b, s, d), F32),
        grid=(b, s // tm),
        in_specs=[pl.BlockSpec((1, tm, d), lambda bi, i: (bi, i, 0)),
                  pl.BlockSpec(wq.shape, fixed),
                  pl.BlockSpec((1, m, 2 * XA_W), lambda bi, i: (bi, 0, 0)),
                  pl.BlockSpec(wo.shape, fixed),
                  pl.BlockSpec((1, d), fixed), pl.BlockSpec((1, d), fixed)],
        out_specs=pl.BlockSpec((1, tm, d), lambda bi, i: (bi, i, 0)),
        compiler_params=_params("parallel", "parallel"),
        name="cross_attn",
    )(x, wq, kv, wo, ln_g, ln_b)


def _mlp_kernel(x_ref, wu_ref, wd_ref, g_ref, b_ref, o_ref, *, alpha):
    half = x_ref.shape[0] // 2
    for r in (slice(0, half), slice(half, 2 * half)):
        x = x_ref[r]
        hdn = jnp.square(jnp.maximum(_dot(x.astype(MXU_DT), wu_ref[...]), 0.0))
        y = _dot(hdn.astype(MXU_DT), wd_ref[...])
        o_ref[r] = _layer_norm(alpha * x + y, g_ref[...], b_ref[...])


def mlp(x2, wu, wd, ln_g, ln_b, alpha, tm=512):
    t, d = x2.shape
    row = lambda i: (i, 0)
    fixed = lambda i: (0, 0)
    resident = lambda a: pl.BlockSpec(a.shape, fixed, pipeline_mode=pl.Buffered(1))
    return pl.pallas_call(
        functools.partial(_mlp_kernel, alpha=alpha),
        out_shape=jax.ShapeDtypeStruct((t, d), F32),
        grid=(t // tm,),
        in_specs=[pl.BlockSpec((tm, d), row), resident(wu), resident(wd),
                  pl.BlockSpec((1, d), fixed), pl.BlockSpec((1, d), fixed)],
        out_specs=pl.BlockSpec((tm, d), row),
        compiler_params=_params("parallel"),
        name="mlp",
    )(x2, wu, wd, ln_g, ln_b)


def _segment_cols(w, cols_per_head, head_offset, width=LANES):
    out = jnp.zeros(w.shape[:-1] + (len(cols_per_head) * width,), w.dtype)
    for h, cols in enumerate(cols_per_head):
        start = h * width + head_offset
        out = out.at[..., start:start + len(cols)].set(w[..., np.asarray(cols)])
    return out


def _prep_layer_weights(seq, w_in, cmp_pe, cmp_w1, cmp_w2, mla_q_norm, mla_w_uq, mla_kv_norm, mla_w_ukv,
                        fox_b_f, w_gate, w_br_nsa, w_br_mla, w_br_fox, w_mix_out, xa_w_q, xa_w_kv,
                        xa_w_o, mlp_w_up, mlp_w_down, ln_g, ln_b):
    depth, d, _ = w_in.shape
    sizes = (NSA_W, NSA_KV, NSA_KV, NSA_KV, NSA_KV, NSA_KV, NSA_KV, 3 * NSA_HEADS,
             MLA_Q_LORA, MLA_KV_LORA, MLA_ROPE, FOX_W, FOX_W, FOX_W, FOX_HEADS)
    offs = np.concatenate([[0], np.cumsum(sizes)])
    col = lambda n: w_in[:, :, offs[n]:offs[n + 1]]
    (nq, nkc, nvc, nks, nvs, nkw, nvw, ngate, cq, ckv, kr, fq, fk, fv, ff) = [col(n) for n in range(15)]
    qk_scale = DH ** -0.5
    wa = jnp.concatenate([nq * qk_scale, fq * qk_scale, fv, nvs, nkw, nvw], axis=2)
    pad = jnp.zeros((depth, d, _NB - (_OFF_FF + FOX_HEADS)), F32)
    wb = jnp.concatenate([cq, ckv, kr, ngate, ff, pad], axis=2)
    wc = jnp.concatenate([nkc, nvc], axis=2)
    wks = _segment_cols(nks, [np.arange(g * DH, (g + 1) * DH) for g in range(NSA_GROUPS)], 0,
                        DH + seq // L_SLC)
    wfk = _segment_cols(fk, [np.arange(h * DH, (h + 1) * DH) for h in range(FOX_HEADS)], 0)
    ff_lane = _OFF_FF - _COL_MISC * LANES
    b_row = jnp.zeros((depth, 1, LANES), F32).at[:, 0, ff_lane:ff_lane + FOX_HEADS].set(fox_b_f)

    dq = MLA_NOPE + MLA_ROPE
    half = MLA_ROPE // 2
    heads = range(MLA_HEADS)
    nope = [np.arange(h * dq, h * dq + MLA_NOPE) for h in heads]
    x1 = [np.arange(h * dq + MLA_NOPE, h * dq + MLA_NOPE + half) for h in heads]
    x2 = [c + half for c in x1]
    wqa = (_segment_cols(mla_w_uq, nope, 0) + _segment_cols(mla_w_uq, x1, MLA_NOPE)
           + _segment_cols(mla_w_uq, x2, MLA_NOPE + half))
    wqb = _segment_cols(mla_w_uq, x2, MLA_NOPE) + _segment_cols(mla_w_uq, x1, MLA_NOPE + half)
    dkv = MLA_NOPE + MLA_V
    wk = _segment_cols(mla_w_ukv, [np.arange(h * dkv, h * dkv + MLA_NOPE) for h in heads], 0)
    wv = mla_w_ukv[:, :, np.concatenate([np.arange(h * dkv + MLA_NOPE, (h + 1) * dkv) for h in heads])]
    pa = np.zeros((MLA_ROPE, MLA_HEADS * LANES), np.float32)
    pb = np.zeros((MLA_ROPE, MLA_HEADS * LANES), np.float32)
    for h in heads:
        for j in range(MLA_ROPE):
            pa[j, h * LANES + MLA_NOPE + j] = 1.0
            pb[j, h * LANES + MLA_NOPE + (j + half) % MLA_ROPE] = 1.0
    stack = lambda top, place: jnp.concatenate(
        [top, jnp.broadcast_to(jnp.asarray(np.concatenate([place, place])), (depth,) + (2 * MLA_ROPE, place.shape[1]))],
        axis=1)
    wka = stack(wk, pa)
    wkb = stack(jnp.zeros_like(wk), pb)

    eye_g = jnp.eye(NSA_GROUPS, dtype=F32)
    w1r = cmp_w1.reshape(depth, 2, L_CMP, DH, CMP_HIDDEN)
    def chunk_weights(w1_half):
        blk = jnp.einsum('ealdj,gh->ealgdhj', w1_half, eye_g)
        return blk.reshape(depth, 2, D_CMP * NSA_KV, NSA_GROUPS * CMP_HIDDEN)
    w1lo = chunk_weights(w1r[:, :, :D_CMP])
    w1hi = chunk_weights(w1r[:, :, D_CMP:])
    w2b = jnp.einsum('eajd,gh->eagjhd', cmp_w2, eye_g).reshape(depth, 2, NSA_GROUPS * CMP_HIDDEN, NSA_KV)
    pe = jnp.broadcast_to(cmp_pe.reshape(depth, 2, 2, D_CMP, 1, DH),
                          (depth, 2, 2, D_CMP, NSA_GROUPS, DH)).reshape(depth, 2, 2, D_CMP * NSA_KV)

    c = lambda a: a.astype(MXU_DT)
    return dict(
        wa=c(wa), wb=c(wb), wc=c(wc), wks=c(wks), wfk=c(wfk),
        pe=pe, w1lo=c(w1lo), w1hi=c(w1hi), w2b=c(w2b),
        q_norm=mla_q_norm[:, None, :], wqa=c(wqa), wqb=c(wqb),
        kv_norm=mla_kv_norm[:, None, :], wka=c(wka), wkb=c(wkb), wv=c(wv),
        b_row=b_row, wg=c(w_gate), wn=c(w_br_nsa), wm=c(w_br_mla), wf=c(w_br_fox), wo=c(w_mix_out),
        xq=c(xa_w_q * qk_scale), xkv=c(xa_w_kv), xo=c(xa_w_o),
        wu=c(mlp_w_up), wd=c(mlp_w_down),
        ln_g=ln_g[:, :, None, :], ln_b=ln_b[:, :, None, :],
    )


def _layer(x, mem2, w, tabs, alpha):
    b, s, d = x.shape
    t = b * s
    x2 = x.reshape(t, d)

    ya, yb, ykc, yvc, yks, yfk = in_proj(x2, w, tabs, s)
    ya3 = ya.reshape(b, s, _NA)
    yb3 = yb.reshape(b, s, _NB)

    nc = s // D_CMP
    kc = compress(ykc.reshape(b, nc, D_CMP * NSA_KV), w["pe"][0], w["w1lo"][0], w["w1hi"][0], w["w2b"][0])
    vc = compress(yvc.reshape(b, nc, D_CMP * NSA_KV), w["pe"][1], w["w1lo"][1], w["w1hi"][1], w["w2b"][1])
    o_nsa = nsa_attention(ya3, yb3, kc, vc, yks.reshape(b, s, -1),
                          tabs["bias_c"], tabs["bias_w"], tabs["overlap_t"])

    q_mla, k_mla, v_mla = mla_up(yb, w, tabs, s)
    wide = MLA_HEADS * LANES
    o_mla = causal_attention(q_mla.reshape(b, s, wide), 0, k_mla.reshape(b, s, wide),
                             v_mla.reshape(b, s, MLA_W), 0, MLA_HEADS, LANES)

    k_decay = fox_decay(yb3, w["b_row"], tabs["decay_place"])
    o_fox = causal_attention(ya3, _COL_FQ, yfk.reshape(b, s, FOX_HEADS * LANES), ya3, _COL_FQ + 1,
                             FOX_HEADS, DH, k_decay=k_decay)

    x2 = mixer_out(x2, o_nsa.reshape(t, NSA_W), o_mla.reshape(t, MLA_W), o_fox.reshape(t, FOX_W),
                   w["wg"], w["wn"], w["wm"], w["wf"], w["wo"], w["ln_g"][0], w["ln_b"][0], alpha)

    m = mem2.shape[0] // b
    kv = matmul(mem2, w["xkv"], MXU_DT, tm=min(512, mem2.shape[0])).reshape(b, m, 2 * XA_W)
    x3 = cross_attention(x2.reshape(b, s, d), kv, w["xq"], w["xo"], w["ln_g"][1], w["ln_b"][1], alpha)

    x2 = mlp(x3.reshape(t, d), w["wu"], w["wd"], w["ln_g"][2], w["ln_b"][2], alpha)
    return x2.reshape(b, s, d)


def _tables(t5_table, s):
    tq = NSA_TQ
    ki = jnp.arange(tq)[:, None]
    qi = jnp.arange(tq)[None, :]
    last = jnp.full((tq, tq), T5_BUCKETS - 1, jnp.int32)
    masked = jnp.full((tq, tq), -1, jnp.int32)
    kinds = [None] * N_TILE_KINDS
    kinds[TILE_DIAG] = jnp.where(qi >= ki, _t5_bucket(qi - ki), -1)
    kinds[TILE_PREV] = _t5_bucket(tq + qi - ki)
    kinds[TILE_FAR] = last
    kinds[TILE_EDGE] = jnp.where(qi < ki, last, masked)
    kinds[TILE_NONE] = masked
    kinds[TILE_SEL_DIAG] = kinds[TILE_DIAG]
    kinds[TILE_SEL_PREV] = kinds[TILE_PREV]
    kinds[TILE_ZERO] = last
    relative = [int(k in (TILE_SEL_DIAG, TILE_SEL_PREV, TILE_ZERO)) for k in range(N_TILE_KINDS)]
    idx_w = jnp.concatenate(kinds, axis=0).astype(jnp.int32)
    rel_w = jnp.concatenate([jnp.full((tq, tq), r, jnp.int32) for r in relative], axis=0)
    bias_w = bias_table(t5_table, idx_w, rel_w, tq).reshape(NSA_HEADS, N_TILE_KINDS, tq, tq)

    nc = s // D_CMP
    cmp_end = jnp.arange(nc) * D_CMP + (L_CMP - 1)
    dist_c = jnp.arange(s)[None, :] - cmp_end[:, None]
    idx_c = jnp.where(dist_c >= 0, _t5_bucket(dist_c), -1).astype(jnp.int32)
    bias_c = bias_table(t5_table, idx_c, jnp.zeros_like(idx_c), min(64, nc))

    n_slc = s // L_SLC
    block_onehot = (np.arange(s)[:, None] // L_SLC == np.arange(n_slc)[None, :]).astype(np.float32)
    block_cols = np.concatenate([np.zeros((s, DH), np.float32), block_onehot] * NSA_GROUPS, axis=1)
    ff_lane = _OFF_FF - _COL_MISC * LANES
    decay_place = np.zeros((3, LANES, FOX_HEADS * LANES), np.float32)
    for j in range(3):
        for h in range(FOX_HEADS):
            decay_place[j, ff_lane + h, h * LANES + DH + j] = 1.0
    c_lo = np.arange(nc)[:, None] * D_CMP
    s_lo = np.arange(n_slc)[None, :] * L_SLC
    overlap = np.maximum(np.minimum(c_lo + L_CMP, s_lo + L_SLC) - np.maximum(c_lo, s_lo), 0) / D_CMP
    overlap[nc - 1, :] = 0.0

    half = MLA_ROPE // 2
    inv = ROPE_THETA ** (-jnp.arange(half, dtype=F32) / half)
    ang = jnp.arange(s).astype(F32)[:, None] * inv[None, :]
    cos, sin = jnp.cos(ang), jnp.sin(ang)
    tail = jnp.zeros((s, LANES - MLA_NOPE - MLA_ROPE), F32)
    seg_c = jnp.concatenate([jnp.ones((s, MLA_NOPE), F32), cos, cos, tail], axis=1)
    seg_s = jnp.concatenate([jnp.zeros((s, MLA_NOPE), F32), -sin, sin, tail], axis=1)
    col_scale = np.ones((1, _NA), np.float32)
    col_scale[0, :NSA_W] = LOG2E
    col_scale[0, _COL_FQ * FOX_W:(_COL_FQ + 1) * FOX_W] = LOG2E
    return dict(bias_w=bias_w, bias_c=bias_c, overlap_t=jnp.asarray(overlap.T, MXU_DT),
                block_cols=jnp.asarray(block_cols), decay_place=jnp.asarray(decay_place, MXU_DT),
                rope_cos=jnp.tile(seg_c, (1, MLA_HEADS)), rope_sin=jnp.tile(seg_s, (1, MLA_HEADS)),
                col_scale=jnp.asarray(col_scale))


def kernel(x, mem, w_in, cmp_pe, cmp_w1, cmp_w2, t5_table, mla_q_norm, mla_w_uq, mla_kv_norm, mla_w_ukv, fox_b_f, w_gate, w_br_nsa, w_br_mla, w_br_fox, w_mix_out, xa_w_q, xa_w_kv, xa_w_o, mlp_w_up, mlp_w_down, ln_g, ln_b):
    depth = w_in.shape[0]
    b, s, d = x.shape
    alpha = (2 * depth) ** 0.25
    weights = _prep_layer_weights(s, w_in, cmp_pe, cmp_w1, cmp_w2, mla_q_norm, mla_w_uq, mla_kv_norm,
                                  mla_w_ukv, fox_b_f, w_gate, w_br_nsa, w_br_mla, w_br_fox, w_mix_out,
                                  xa_w_q, xa_w_kv, xa_w_o, mlp_w_up, mlp_w_down, ln_g, ln_b)
    tabs = _tables(t5_table, s)
    mem2 = mem.reshape(-1, d)

    for layer in range(depth):
        x = _layer(x, mem2, {name: a[layer] for name, a in weights.items()}, tabs, alpha)
    return x
```
